```python
import jax, jax.numpy as jnp
from jax import lax
import numpy as np

D_MODEL = 1024
BATCH = 16
SEQ = 4096
DEPTH = 4

ATT_HEADS = 8
ATT_HEAD_DIM = 64
ATT_WIDTH = ATT_HEADS * ATT_HEAD_DIM
Q_BLOCK = 128
CONV_WIDTH = 512
CONV_TAPS = 31
SC_WIDTH = 512
SC_TAPS = 3
N_BRANCH = 3
_IN_SIZES = (ATT_WIDTH, ATT_WIDTH, ATT_WIDTH, ATT_HEADS,
             CONV_WIDTH, CONV_WIDTH, SC_WIDTH, SC_WIDTH, SC_WIDTH)
IN_COLS = sum(_IN_SIZES)
IN_SPLITS = [sum(_IN_SIZES[:i + 1]) for i in range(len(_IN_SIZES) - 1)]
N_EXPERTS = 32
N_GROUPS = 8
EXPERTS_PER_GROUP = N_EXPERTS // N_GROUPS
TOP_K = 2
EXPERT_FF = 512
DISPATCH_BLOCK = 256
DEEPNORM_ALPHA = (2 * DEPTH) ** 0.25
DEEPNORM_BETA = (8 * DEPTH) ** -0.25
LN_EPS = 1e-5

kernel_name = "fox_conformer_shortconv_grouped_moe_deepnorm"


def layer_norm(x, g, b):
    xf = x.astype(jnp.float32)
    mu = jnp.mean(xf, axis=-1, keepdims=True)
    xc = xf - mu
    var = jnp.mean(xc * xc, axis=-1, keepdims=True)
    y = xc * lax.rsqrt(var + LN_EPS) * g.astype(jnp.float32) + b.astype(jnp.float32)
    return y.astype(x.dtype)


def causal_depthwise_conv(x, w):
    taps, chans = w.shape
    return lax.conv_general_dilated(
        x, w[:, None, :].astype(x.dtype), window_strides=(1,),
        padding=((taps - 1, 0),), dimension_numbers=("NWC", "WIO", "NWC"),
        feature_group_count=chans)


def fox_attention(q, k, v, f_logit):
    B, S, H, Dh = q.shape
    scale = Dh ** -0.5
    c = jnp.cumsum(jax.nn.log_sigmoid(f_logit.astype(jnp.float32)), axis=1)
    c = jnp.transpose(c, (0, 2, 1))
    q = jnp.transpose(q, (0, 2, 1, 3))
    k = jnp.transpose(k, (0, 2, 1, 3))
    v = jnp.transpose(v, (0, 2, 1, 3))
    outs = []
    for i in range(S // Q_BLOCK):
        lo, hi = i * Q_BLOCK, (i + 1) * Q_BLOCK
        qb = q[:, :, lo:hi]
        kb = k[:, :, :hi]
        vb = v[:, :, :hi]
        s = jnp.einsum("bhqd,bhkd->bhqk", qb, kb).astype(jnp.float32) * scale
        s = s + c[:, :, lo:hi, None] - c[:, :, None, :hi]
        mask = (lo + jnp.arange(Q_BLOCK))[:, None] >= jnp.arange(hi)[None, :]
        s = jnp.where(mask, s, -jnp.inf)
        p = jax.nn.softmax(s, axis=-1).astype(vb.dtype)
        outs.append(jnp.einsum("bhqk,bhkd->bhqd", p, vb))
    o = jnp.concatenate(outs, axis=2)
    return jnp.transpose(o, (0, 2, 1, 3)).reshape(B, S, H * Dh)


def hybrid_mixer(x, w_in, b_forget, conv_w, conv_b, conv_ln_g, conv_ln_b, sc_w,
                 w_branch, w_gate, b_gate, w_out):
    B, S, _ = x.shape
    z = x @ w_in
    q, k, v, f, glu_a, glu_g, sc_b, sc_c, sc_h = jnp.split(z, IN_SPLITS, axis=-1)
    shp = (B, S, ATT_HEADS, ATT_HEAD_DIM)
    br_att = fox_attention(q.reshape(shp), k.reshape(shp), v.reshape(shp), f + b_forget)
    u = glu_a * jax.nn.sigmoid(glu_g)
    u = causal_depthwise_conv(u, conv_w) + conv_b
    br_conv = jax.nn.silu(layer_norm(u, conv_ln_g, conv_ln_b))
    br_sc = sc_b * causal_depthwise_conv(sc_c * sc_h, sc_w)
    merged = None
    for n, br in enumerate((br_att, br_conv, br_sc)):
        gate = jax.nn.sigmoid(x @ w_gate[n] + b_gate[n])
        term = gate * (br @ w_branch[n])
        merged = term if merged is None else merged + term
    return merged @ w_out


def grouped_moe(x, w_router, b_router, w_up, w_down):
    B, S, D = x.shape
    T = B * S
    M = DISPATCH_BLOCK
    xt = x.reshape(T, D)
    logits = (xt @ w_router).astype(jnp.float32) + b_router.astype(jnp.float32)
    probs = jax.nn.softmax(logits, axis=-1).reshape(T, N_GROUPS, EXPERTS_PER_GROUP)
    vals, idx = lax.top_k(probs, TOP_K)
    best = jnp.argmax(vals.sum(-1), axis=-1)
    sel_vals = jnp.take_along_axis(vals, best[:, None, None], axis=1)[:, 0]
    sel_idx = jnp.take_along_axis(idx, best[:, None, None], axis=1)[:, 0] \
        + best[:, None] * EXPERTS_PER_GROUP
    gates = sel_vals / jnp.sum(sel_vals, axis=-1, keepdims=True)
    A = T * TOP_K
    e_flat = sel_idx.reshape(A).astype(jnp.int32)
    g_flat = gates.reshape(A)
    tok_flat = jnp.arange(A, dtype=jnp.int32) // TOP_K
    order = jnp.argsort(e_flat)
    e_sorted = e_flat[order]
    counts = jnp.bincount(e_flat, length=N_EXPERTS).astype(jnp.int32)
    padded = (counts + M - 1) // M * M
    pad_end = jnp.cumsum(padded)
    pad_start = pad_end - padded
    start = jnp.cumsum(counts) - counts
    dest = pad_start[e_sorted] + jnp.arange(A, dtype=jnp.int32) - start[e_sorted]
    n_blocks = (A + N_EXPERTS * (M - 1) + M - 1) // M
    P = n_blocks * M
    slot_tok = jnp.full((P,), T, jnp.int32).at[dest].set(tok_flat[order])
    slot_gate = jnp.zeros((P,), jnp.float32).at[dest].set(g_flat[order])
    block_expert = jnp.minimum(
        jnp.searchsorted(pad_end, jnp.arange(n_blocks, dtype=jnp.int32) * M, side="right"),
        N_EXPERTS - 1)
    x_pad = jnp.concatenate([xt, jnp.zeros((1, D), xt.dtype)], axis=0)
    xs = x_pad[slot_tok].reshape(n_blocks, M, D)

    def expert_block(args):
        xb, e = args
        a, g = jnp.split(xb @ w_up[e], 2, axis=-1)
        return (jax.nn.silu(g) * a) @ w_down[e]

    ys = lax.map(expert_block, (xs, block_expert)).reshape(P, D)
    out = jax.ops.segment_sum(ys * slot_gate[:, None].astype(ys.dtype), slot_tok,
                              num_segments=T + 1)[:T]
    return out.reshape(B, S, D)


def setup_inputs(seed: int = 0) -> dict:
    key = jax.random.key(seed)
    ks = jax.random.split(key, 24)
    f32 = jnp.float32
    L, D = DEPTH, D_MODEL
    nrm = lambda k, shape: jax.random.normal(k, shape, f32)
    x = nrm(ks[0], (BATCH, SEQ, D))
    w_in = nrm(ks[1], (L, D, IN_COLS)) * D ** -0.5
    b_forget = 3.0 + 0.1 * nrm(ks[2], (L, ATT_HEADS))
    conv_w = nrm(ks[3], (L, CONV_TAPS, CONV_WIDTH)) * CONV_TAPS ** -0.5
    conv_b = 0.02 * nrm(ks[4], (L, CONV_WIDTH))
    conv_ln_g = 1.0 + 0.02 * nrm(ks[5], (L, CONV_WIDTH))
    conv_ln_b = 0.02 * nrm(ks[6], (L, CONV_WIDTH))
    sc_w = nrm(ks[7], (L, SC_TAPS, SC_WIDTH)) * SC_TAPS ** -0.5
    w_branch = nrm(ks[8], (L, N_BRANCH, ATT_WIDTH, D)) * (ATT_WIDTH ** -0.5 * DEEPNORM_BETA)
    w_gate = nrm(ks[9], (L, N_BRANCH, D, D)) * D ** -0.5
    b_gate = 0.02 * nrm(ks[10], (L, N_BRANCH, D))
    w_out = nrm(ks[11], (L, D, D)) * (D ** -0.5 * DEEPNORM_BETA)
    ln1_g = 1.0 + 0.02 * nrm(ks[12], (L, D))
    ln1_b = 0.02 * nrm(ks[13], (L, D))
    w_up = nrm(ks[14], (L, N_EXPERTS, D, 2 * EXPERT_FF)) * D ** -0.5
    w_down = nrm(ks[15], (L, N_EXPERTS, EXPERT_FF, D)) * (EXPERT_FF ** -0.5 * DEEPNORM_BETA)
    ln2_g = 1.0 + 0.02 * nrm(ks[16], (L, D))
    ln2_b = 0.02 * nrm(ks[17], (L, D))
    w_router = nrm(ks[18], (D, N_EXPERTS)) * D ** -0.5
    b_router = 0.01 * nrm(ks[19], (N_EXPERTS,))
    return {"x": x, "w_in": w_in, "b_forget": b_forget, "conv_w": conv_w,
            "conv_b": conv_b, "conv_ln_g": conv_ln_g, "conv_ln_b": conv_ln_b,
            "sc_w": sc_w, "w_branch": w_branch, "w_gate": w_gate, "b_gate": b_gate,
            "w_out": w_out, "ln1_g": ln1_g, "ln1_b": ln1_b, "w_up": w_up,
            "w_down": w_down, "ln2_g": ln2_g, "ln2_b": ln2_b,
            "w_router": w_router, "b_router": b_router}


def reference(x, w_in, b_forget, conv_w, conv_b, conv_ln_g, conv_ln_b, sc_w,
              w_branch, w_gate, b_gate, w_out, ln1_g, ln1_b, w_up, w_down,
              ln2_g, ln2_b, w_router, b_router):
    for l in range(DEPTH):
        h = hybrid_mixer(x, w_in[l], b_forget[l], conv_w[l], conv_b[l], conv_ln_g[l],
                         conv_ln_b[l], sc_w[l], w_branch[l], w_gate[l], b_gate[l], w_out[l])
        x = layer_norm(DEEPNORM_ALPHA * x + h, ln1_g[l], ln1_b[l])
        m = grouped_moe(x, w_router, b_router, w_up[l], w_down[l])
        x = layer_norm(DEEPNORM_ALPHA * x + m, ln2_g[l], ln2_b[l])
    return x
```

```python
import functools

import jax
import jax.numpy as jnp
from jax import lax
from jax.experimental import pallas as pl
from jax.experimental.pallas import tpu as pltpu

F32 = jnp.float32
BF16 = jnp.bfloat16

ATT_HEADS = 8
ATT_HEAD_DIM = 64
WIDTH = 512
CONV_TAPS = 31
SC_TAPS = 3
N_EXPERTS = 32
EXPERTS_PER_GROUP = 4
EXPERT_FF = 512
DEPTH = 4
DEEPNORM_ALPHA = (2 * DEPTH) ** 0.25
LN_EPS = 1e-5

LANES = 128
SUBLANES = 8
VMEM_LIMIT = 56 * 1024 * 1024

TM_PROJ = 512
TM_CONV = 512
CONV_HALO = 32
CONV_CHUNK = 64
TQ = 256
TK = 512
EXPERT_BLOCK = 512
TM_ROUTE = 256
NEG_BIG = -1e30


def _sigmoid(x):
    return 1.0 / (1.0 + jnp.exp(-x))


def _log_sigmoid(x):
    return jnp.minimum(x, 0.0) - jnp.log(1.0 + jnp.exp(-jnp.abs(x)))


def _split3(x):
    hi = x.astype(BF16)
    r1 = x - hi.astype(F32)
    mid = r1.astype(BF16)
    lo = (r1 - mid.astype(F32)).astype(BF16)
    return hi, mid, lo


def _layer_norm(y, g, b):
    mu = jnp.mean(y, axis=-1, keepdims=True)
    yc = y - mu
    var = jnp.mean(yc * yc, axis=-1, keepdims=True)
    return yc * lax.rsqrt(var + LN_EPS) * g + b


def _in_proj_kernel(x_ref, wm_ref, wf_ref, bf_ref, scw_ref, tri_ref,
                    q_ref, k_ref, v_ref, u_ref, brsc_ref, c_ref, ct_ref,
                    carry_c, carry_p, *, tiles_per_seq):
    i = pl.program_id(0)

    @pl.when(i % tiles_per_seq == 0)
    def _():
        carry_c[...] = jnp.zeros_like(carry_c)
        carry_p[...] = jnp.zeros_like(carry_p)

    tm = x_ref.shape[0]
    xb = x_ref[...].astype(BF16)

    def mm(j):
        return jnp.dot(xb, wm_ref[:, j * WIDTH:(j + 1) * WIDTH], preferred_element_type=F32)

    q_ref[...] = (mm(0) * (ATT_HEAD_DIM ** -0.5)).astype(BF16)
    k_ref[...] = mm(1).astype(BF16)
    v_ref[...] = mm(2).astype(BF16)
    u_ref[...] = mm(3) * _sigmoid(mm(4))

    p = mm(6) * mm(7)
    row = lax.broadcasted_iota(jnp.int32, p.shape, 0)
    prev = carry_p[...]
    p1 = jnp.where(row == 0, prev[7:8, :], pltpu.roll(p, 1, axis=0))
    p2 = pltpu.roll(p, 2, axis=0)
    p2 = jnp.where(row == 0, prev[6:7, :], jnp.where(row == 1, prev[7:8, :], p2))
    y = scw_ref[0:1, :] * p2 + scw_ref[1:2, :] * p1 + scw_ref[2:3, :] * p
    brsc_ref[...] = (mm(5) * y).astype(BF16)
    carry_p[...] = p[tm - SUBLANES:tm, :]

    zf = jnp.dot(xb, wf_ref[...], preferred_element_type=F32)
    ls = _log_sigmoid(zf + bf_ref[...])
    hi, mid, lo = _split3(ls)
    tri = tri_ref[...]
    c = (jnp.dot(tri, hi, preferred_element_type=F32)
         + jnp.dot(tri, mid, preferred_element_type=F32)
         + jnp.dot(tri, lo, preferred_element_type=F32)) + carry_c[...]
    c_ref[...] = c
    carry_c[...] = c[tm - 1:tm, :]
    ct_ref[...] = c.T[0:SUBLANES, :]


def _in_proj(x2, wm, wf, bf, scw, seq):
    t, d = x2.shape
    tm = min(TM_PROJ, seq)
    assert seq % tm == 0 and t % seq == 0
    tri = jnp.tril(jnp.ones((tm, tm), F32)).astype(BF16)
    const = lambda shape: pl.BlockSpec(shape, lambda i: (0,) * len(shape))
    rows = lambda w: pl.BlockSpec((tm, w), lambda i: (i, 0))
    return pl.pallas_call(
        functools.partial(_in_proj_kernel, tiles_per_seq=seq // tm),
        grid=(t // tm,),
        in_specs=[rows(d), const(wm.shape), const(wf.shape), const(bf.shape),
                  const(scw.shape), const(tri.shape)],
        out_specs=[rows(WIDTH), rows(WIDTH), rows(WIDTH), rows(WIDTH), rows(WIDTH), rows(LANES),
                   pl.BlockSpec((SUBLANES, tm), lambda i: (0, i))],
        out_shape=[jax.ShapeDtypeStruct((t, WIDTH), BF16),
                   jax.ShapeDtypeStruct((t, WIDTH), BF16),
                   jax.ShapeDtypeStruct((t, WIDTH), BF16),
                   jax.ShapeDtypeStruct((t, WIDTH), F32),
                   jax.ShapeDtypeStruct((t, WIDTH), BF16),
                   jax.ShapeDtypeStruct((t, LANES), F32),
                   jax.ShapeDtypeStruct((SUBLANES, t), F32)],
        scratch_shapes=[pltpu.VMEM((1, LANES), F32), pltpu.VMEM((SUBLANES, WIDTH), F32)],
        compiler_params=pltpu.CompilerParams(dimension_semantics=("arbitrary",),
                                             vmem_limit_bytes=VMEM_LIMIT),
        name="in_proj",
    )(x2, wm, wf, bf, scw, tri)


def _conv31_kernel(u_ref, w_ref, cb_ref, g_ref, b_ref, o_ref, ext, *, tiles_per_seq):
    i = pl.program_id(0)
    tm = u_ref.shape[0]

    @pl.when(i % tiles_per_seq == 0)
    def _():
        ext[0:CONV_HALO, :] = jnp.zeros((CONV_HALO, WIDTH), F32)

    ext[CONV_HALO:CONV_HALO + tm, :] = u_ref[...]
    base = CONV_HALO - (CONV_TAPS - 1)
    for c0 in range(0, tm, CONV_CHUNK):
        acc = jnp.zeros((CONV_CHUNK, WIDTH), F32)
        for k in range(CONV_TAPS):
            acc = acc + w_ref[k:k + 1, :] * ext[c0 + base + k:c0 + base + k + CONV_CHUNK, :]
        y = _layer_norm(acc + cb_ref[...], g_ref[...], b_ref[...])
        o_ref[c0:c0 + CONV_CHUNK, :] = (y * _sigmoid(y)).astype(BF16)
    ext[0:CONV_HALO, :] = ext[tm:tm + CONV_HALO, :]


def _conv31(u, w, cb, g, b, seq):
    t = u.shape[0]
    tm = min(TM_CONV, seq)
    assert seq % tm == 0 and tm % CONV_CHUNK == 0
    const = lambda shape: pl.BlockSpec(shape, lambda i: (0,) * len(shape))
    return pl.pallas_call(
        functools.partial(_conv31_kernel, tiles_per_seq=seq // tm),
        grid=(t // tm,),
        in_specs=[pl.BlockSpec((tm, WIDTH), lambda i: (i, 0)), const(w.shape), const(cb.shape),
                  const(g.shape), const(b.shape)],
        out_specs=pl.BlockSpec((tm, WIDTH), lambda i: (i, 0)),
        out_shape=jax.ShapeDtypeStruct((t, WIDTH), BF16),
        scratch_shapes=[pltpu.VMEM((CONV_HALO + tm, WIDTH), F32)],
        compiler_params=pltpu.CompilerParams(dimension_semantics=("arbitrary",),
                                             vmem_limit_bytes=VMEM_LIMIT),
        name="conv31",
    )(u, w, cb, g, b)


def _fox_kernel(q_ref, k_ref, v_ref, c_ref, ct_ref, o_ref, *, tk):
    hp = pl.program_id(1)
    qi = pl.program_id(2)
    tq = q_ref.shape[0]
    lane = lax.broadcasted_iota(jnp.int32, (tq, LANES), 1)
    first = lane < ATT_HEAD_DIM
    q2 = q_ref[...]
    zero = jnp.zeros_like(q2)
    qs = (jnp.where(first, q2, zero), jnp.where(first, zero, q2))
    c_tile = c_ref[...]
    clane = lax.broadcasted_iota(jnp.int32, c_tile.shape, 1)
    ct_cols = [jnp.sum(jnp.where(clane == 2 * hp + h, c_tile, 0.0), axis=1, keepdims=True)
               for h in range(2)]
    q_lo = qi * tq

    def tile(j, carry, masked):
        start = pl.multiple_of(j * tk, tk)
        k2 = k_ref[pl.ds(start, tk), :]
        v2 = v_ref[pl.ds(start, tk), :]
        cs_all = ct_ref[:, pl.ds(start, tk)]
        head = lax.broadcasted_iota(jnp.int32, cs_all.shape, 0)
        out = []
        for h in range(2):
            m, l, acc = carry[h]
            cs = jnp.sum(jnp.where(head == 2 * hp + h, cs_all, 0.0), axis=0, keepdims=True)
            s = lax.dot_general(qs[h], k2, (((1,), (1,)), ((), ())), preferred_element_type=F32)
            s = s + ct_cols[h] - cs
            if masked:
                r = q_lo + lax.broadcasted_iota(jnp.int32, s.shape, 0)
                col = start + lax.broadcasted_iota(jnp.int32, s.shape, 1)
                s = jnp.where(r >= col, s, NEG_BIG)
            m_new = jnp.maximum(m, jnp.max(s, axis=1, keepdims=True))
            a = jnp.exp(m - m_new)
            p = jnp.exp(s - m_new)
            l = a * l + jnp.sum(p, axis=1, keepdims=True)
            acc = a * acc + jnp.dot(p.astype(BF16), v2, preferred_element_type=F32)
            out.append((m_new, l, acc))
        return tuple(out)

    init = tuple((jnp.full((tq, 1), NEG_BIG, F32), jnp.zeros((tq, 1), F32),
                  jnp.zeros((tq, LANES), F32)) for _ in range(2))
    n_full = q_lo // tk
    carry = lax.fori_loop(0, n_full, lambda j, c: tile(j, c, False), init)
    carry = tile(n_full, carry, True)
    (_, l0, a0), (_, l1, a1) = carry
    o_ref[...] = jnp.where(first, a0 / l0, a1 / l1).astype(BF16)


def _fox_attention(q, k, v, c, ct, batch, seq):
    t = q.shape[0]
    tq = min(TQ, seq)
    tk = min(TK, seq)
    assert seq % tq == 0 and seq % tk == 0 and tk % tq == 0
    nq = seq // tq
    pairs = ATT_HEADS // 2
    return pl.pallas_call(
        functools.partial(_fox_kernel, tk=tk),
        grid=(batch, pairs, nq),
        in_specs=[pl.BlockSpec((tq, LANES), lambda b, hp, qi: (b * nq + qi, hp)),
                  pl.BlockSpec((seq, LANES), lambda b, hp, qi: (b, hp)),
                  pl.BlockSpec((seq, LANES), lambda b, hp, qi: (b, hp)),
                  pl.BlockSpec((tq, LANES), lambda b, hp, qi: (b * nq + qi, 0)),
                  pl.BlockSpec((SUBLANES, seq), lambda b, hp, qi: (0, b))],
        out_specs=pl.BlockSpec((tq, LANES), lambda b, hp, qi: (b * nq + qi, hp)),
        out_shape=jax.ShapeDtypeStruct((t, WIDTH), BF16),
        compiler_params=pltpu.CompilerParams(
            dimension_semantics=("arbitrary", "arbitrary", "arbitrary"),
            vmem_limit_bytes=VMEM_LIMIT),
        name="fox_attn",
    )(q, k, v, c, ct)


def _partner(x, lane, bit):
    up = pltpu.roll(x, LANES - bit, axis=1)
    down = pltpu.roll(x, bit, axis=1)
    return jnp.where((lane & bit) == 0, up, down)


def _merge_kernel(x_ref, att_ref, conv_ref, sc_ref, wg_ref, bg_ref, wb_ref, wo_ref,
                  g1_ref, b1_ref, wr_ref, br_ref, tril_ref,
                  x1_ref, route_ref, counts_ref, carry):
    i = pl.program_id(0)

    @pl.when(i == 0)
    def _():
        carry[...] = jnp.zeros_like(carry)

    x = x_ref[...]
    xb = x.astype(BF16)
    merged = None
    for n, br_ref_n in enumerate((att_ref, conv_ref, sc_ref)):
        gate = _sigmoid(jnp.dot(xb, wg_ref[n], preferred_element_type=F32) + bg_ref[n])
        term = gate * jnp.dot(br_ref_n[...], wb_ref[n], preferred_element_type=F32)
        merged = term if merged is None else merged + term
    h = jnp.dot(merged.astype(BF16), wo_ref[...], preferred_element_type=F32)
    x1 = _layer_norm(DEEPNORM_ALPHA * x + h, g1_ref[...], b1_ref[...])
    x1_ref[...] = x1

    x_hi = x1.astype(BF16)
    x_lo = (x1 - x_hi.astype(F32)).astype(BF16)
    logits = (jnp.dot(x_hi, wr_ref[0], preferred_element_type=F32)
              + jnp.dot(x_lo, wr_ref[0], preferred_element_type=F32)
              + jnp.dot(x_hi, wr_ref[1], preferred_element_type=F32)) + br_ref[...]
    mx = jnp.max(logits, axis=1, keepdims=True)
    ex = jnp.exp(logits - mx)
    p = ex / jnp.sum(ex, axis=1, keepdims=True)

    lane = lax.broadcasted_iota(jnp.int32, p.shape, 1)
    sub = lane & (EXPERTS_PER_GROUP - 1)
    rank = jnp.zeros(p.shape, jnp.int32)
    for d in range(1, EXPERTS_PER_GROUP):
        below = pltpu.roll(p, d, axis=1)
        above = pltpu.roll(p, LANES - d, axis=1)
        rank = rank + jnp.where((sub >= d) & (below >= p), 1, 0)
        rank = rank + jnp.where((sub + d < EXPERTS_PER_GROUP) & (above > p), 1, 0)
    top2 = rank < 2
    score = jnp.where(top2, p, 0.0)
    score = score + _partner(score, lane, 1)
    score = score + _partner(score, lane, 2)
    best = jnp.max(score, axis=1, keepdims=True)
    group = lane >> 2
    best_group = jnp.min(jnp.where(score == best, group, LANES), axis=1, keepdims=True)
    sel = top2 & (group == best_group)
    gate = jnp.where(sel, p / best, 0.0)

    sel_b = jnp.where(sel, 1.0, 0.0).astype(BF16)
    before = jnp.dot(tril_ref[...], sel_b, preferred_element_type=F32) + carry[...]
    tm = x.shape[0]
    new_carry = before[tm - 1:tm, :] + sel_b[tm - 1:tm, :].astype(F32)
    carry[...] = new_carry
    counts_ref[...] = new_carry

    e_lo = jnp.min(jnp.where(sel, lane, LANES), axis=1, keepdims=True)
    e_hi = jnp.max(jnp.where(sel, lane, -1), axis=1, keepdims=True)
    is_lo = lane == e_lo
    is_hi = lane == e_hi
    pick = lambda m, a: jnp.sum(jnp.where(m, a, 0.0), axis=1, keepdims=True)
    route = jnp.where(lane == 0, e_lo.astype(F32), 0.0)
    route = jnp.where(lane == 1, e_hi.astype(F32), route)
    route = jnp.where(lane == 2, pick(is_lo, before), route)
    route = jnp.where(lane == 3, pick(is_hi, before), route)
    route = jnp.where(lane == 4, pick(is_lo, gate), route)
    route = jnp.where(lane == 5, pick(is_hi, gate), route)
    route_ref[...] = route


def _merge(x2, att, conv, sc, wg, bg, wb, wo, g1, b1, wr, br):
    t, d = x2.shape
    tm = min(TM_PROJ, t)
    assert t % tm == 0
    tril = jnp.tril(jnp.ones((tm, tm), F32), -1).astype(BF16)
    const = lambda shape: pl.BlockSpec(shape, lambda i: (0,) * len(shape),
                                       pipeline_mode=pl.Buffered(1))
    rows = lambda w: pl.BlockSpec((tm, w), lambda i: (i, 0))
    return pl.pallas_call(
        _merge_kernel,
        grid=(t // tm,),
        in_specs=[rows(d), rows(WIDTH), rows(WIDTH), rows(WIDTH),
                  const(wg.shape), const(bg.shape), const(wb.shape), const(wo.shape),
                  const(g1.shape), const(b1.shape), const(wr.shape), const(br.shape),
                  const(tril.shape)],
        out_specs=[rows(d), rows(LANES), pl.BlockSpec((1, LANES), lambda i: (0, 0))],
        out_shape=[jax.ShapeDtypeStruct((t, d), F32),
                   jax.ShapeDtypeStruct((t, LANES), F32),
                   jax.ShapeDtypeStruct((1, LANES), F32)],
        scratch_shapes=[pltpu.VMEM((1, LANES), F32)],
        compiler_params=pltpu.CompilerParams(dimension_semantics=("arbitrary",),
                                             vmem_limit_bytes=VMEM_LIMIT),
        name="merge",
    )(x2, att, conv, sc, wg, bg, wb, wo, g1, b1, wr, br, tril)


def _dispatch_kernel(pos_ref, fill_ref, x_ref, xs_ref, idx, fidx, zrow, sem, isem):
    i = pl.program_id(0)
    n = idx.shape[0]

    @pl.when(i == 0)
    def _():
        zrow[...] = jnp.zeros_like(zrow)
        cp = pltpu.make_async_copy(fill_ref, fidx, isem)
        cp.start()
        cp.wait()
        nfill = fidx.shape[0]

        def issue(r, _):
            pltpu.make_async_copy(zrow, xs_ref.at[pl.ds(fidx[r], 1)], sem).start()
            return 0

        def drain(r, _):
            pltpu.make_async_copy(zrow, xs_ref.at[pl.ds(0, 1)], sem).wait()
            return 0

        lax.fori_loop(0, nfill, issue, 0)
        lax.fori_loop(0, nfill, drain, 0)

    cp = pltpu.make_async_copy(pos_ref.at[i], idx, isem)
    cp.start()
    cp.wait()
    base = i * (n // 2)

    def issue(r, _):
        src = x_ref.at[pl.ds(base + r // 2, 1)]
        pltpu.make_async_copy(src, xs_ref.at[pl.ds(idx[r], 1)], sem).start()
        return 0

    def drain(r, _):
        pltpu.make_async_copy(x_ref.at[pl.ds(0, 1)], xs_ref.at[pl.ds(0, 1)], sem).wait()
        return 0

    lax.fori_loop(0, n, issue, 0)
    lax.fori_loop(0, n, drain, 0)


def _dispatch(x1, pos, fill, n_rows):
    t, d = x1.shape
    tm = min(TM_ROUTE, t)
    assert t % tm == 0
    pos2 = pos.reshape(t // tm, 2 * tm)
    return pl.pallas_call(
        _dispatch_kernel,
        grid=(t // tm,),
        in_specs=[pl.BlockSpec(memory_space=pl.ANY), pl.BlockSpec(memory_space=pl.ANY),
                  pl.BlockSpec(memory_space=pl.ANY)],
        out_specs=pl.BlockSpec(memory_space=pl.ANY),
        out_shape=jax.ShapeDtypeStruct((n_rows, d), F32),
        scratch_shapes=[pltpu.SMEM((2 * tm,), jnp.int32), pltpu.SMEM(fill.shape, jnp.int32),
                        pltpu.VMEM((1, d), F32), pltpu.SemaphoreType.DMA, pltpu.SemaphoreType.DMA],
        compiler_params=pltpu.CompilerParams(dimension_semantics=("arbitrary",),
                                             has_side_effects=True),
        name="dispatch",
    )(pos2, fill, x1)


def _expert_kernel(be_ref, nb_ref, xs_ref, wu_ref, wd_ref, ys_ref):
    b = pl.program_id(0)

    @pl.when(b < nb_ref[0])
    def _():
        xb = xs_ref[...].astype(BF16)
        hidden = jnp.dot(xb, wu_ref[0], preferred_element_type=F32)
        a = hidden[:, :EXPERT_FF]
        g = hidden[:, EXPERT_FF:]
        act = (g * _sigmoid(g) * a).astype(BF16)
        ys_ref[...] = jnp.dot(act, wd_ref[0], preferred_element_type=F32)

    @pl.when(b >= nb_ref[0])
    def _():
        ys_ref[...] = jnp.zeros_like(ys_ref)


def _experts(xs, wu, wd, block_expert, n_used):
    n_rows, d = xs.shape
    n_blocks = n_rows // EXPERT_BLOCK

    def live(b, nb):
        return jnp.minimum(b, nb[0] - 1)

    return pl.pallas_call(
        _expert_kernel,
        grid_spec=pltpu.PrefetchScalarGridSpec(
            num_scalar_prefetch=2,
            grid=(n_blocks,),
            in_specs=[pl.BlockSpec((EXPERT_BLOCK, d), lambda b, be, nb: (live(b, nb), 0)),
                      pl.BlockSpec((1,) + wu.shape[1:], lambda b, be, nb: (be[live(b, nb)], 0, 0)),
                      pl.BlockSpec((1,) + wd.shape[1:], lambda b, be, nb: (be[live(b, nb)], 0, 0))],
            out_specs=pl.BlockSpec((EXPERT_BLOCK, d), lambda b, be, nb: (b, 0))),
        out_shape=jax.ShapeDtypeStruct((n_rows, d), F32),
        compiler_params=pltpu.CompilerParams(dimension_semantics=("arbitrary",),
                                             vmem_limit_bytes=VMEM_LIMIT),
        name="experts",
    )(block_expert, n_used, xs, wu, wd)


def _combine_kernel(pos_ref, x1_ref, route_ref, ys_ref, g2_ref, b2_ref, o_ref,
                    idx, rows, sem, isem):
    i = pl.program_id(0)
    tm = x1_ref.shape[0]
    cp = pltpu.make_async_copy(pos_ref.at[i], idx, isem)
    cp.start()
    cp.wait()

    def issue(r, _):
        dst = rows.at[r % 2, pl.ds(r // 2, 1)]
        pltpu.make_async_copy(ys_ref.at[pl.ds(idx[r], 1)], dst, sem).start()
        return 0

    def drain(r, _):
        pltpu.make_async_copy(ys_ref.at[pl.ds(0, 1)], rows.at[0, pl.ds(0, 1)], sem).wait()
        return 0

    lax.fori_loop(0, 2 * tm, issue, 0)
    lax.fori_loop(0, 2 * tm, drain, 0)
    route = route_ref[...]
    m = route[:, 4:5] * rows[0] + route[:, 5:6] * rows[1]
    o_ref[...] = _layer_norm(DEEPNORM_ALPHA * x1_ref[...] + m, g2_ref[...], b2_ref[...])


def _combine(x1, route, pos, ys, g2, b2):
    t, d = x1.shape
    tm = min(TM_ROUTE, t)
    pos2 = pos.reshape(t // tm, 2 * tm)
    const = lambda shape: pl.BlockSpec(shape, lambda i: (0,) * len(shape))
    return pl.pallas_call(
        _combine_kernel,
        grid=(t // tm,),
        in_specs=[pl.BlockSpec(memory_space=pl.ANY),
                  pl.BlockSpec((tm, d), lambda i: (i, 0)),
                  pl.BlockSpec((tm, LANES), lambda i: (i, 0)),
                  pl.BlockSpec(memory_space=pl.ANY),
                  const(g2.shape), const(b2.shape)],
        out_specs=pl.BlockSpec((tm, d), lambda i: (i, 0)),
        out_shape=jax.ShapeDtypeStruct((t, d), F32),
        scratch_shapes=[pltpu.SMEM((2 * tm,), jnp.int32), pltpu.VMEM((2, tm, d), F32),
                        pltpu.SemaphoreType.DMA, pltpu.SemaphoreType.DMA],
        compiler_params=pltpu.CompilerParams(dimension_semantics=("arbitrary",),
                                             vmem_limit_bytes=VMEM_LIMIT),
        name="combine",
    )(pos2, x1, route, ys, g2, b2)


def _routing_tables(route, counts, n_blocks):
    t = route.shape[0]
    cnt = counts[0, :N_EXPERTS].astype(jnp.int32)
    padded = (cnt + EXPERT_BLOCK - 1) // EXPERT_BLOCK * EXPERT_BLOCK
    pad_end = jnp.cumsum(padded)
    pad_start = pad_end - padded
    experts = route[:, 0:2].astype(jnp.int32)
    ranks = route[:, 2:4].astype(jnp.int32)
    pos = (pad_start[experts] + ranks).reshape(2 * t)
    blk = jnp.arange(n_blocks, dtype=jnp.int32) * EXPERT_BLOCK
    block_expert = jnp.minimum(jnp.searchsorted(pad_end, blk, side="right"),
                               N_EXPERTS - 1).astype(jnp.int32)
    n_used = (pad_end[-1:] // EXPERT_BLOCK).astype(jnp.int32)
    n_rows = n_blocks * EXPERT_BLOCK
    fill_start = jnp.concatenate([pad_start + cnt, pad_end[-1:]])
    fill_cnt = jnp.concatenate([padded - cnt, n_rows - pad_end[-1:]])
    fill_end = jnp.cumsum(fill_cnt)
    j = jnp.arange(n_rows - 2 * t, dtype=jnp.int32)
    e = jnp.minimum(jnp.searchsorted(fill_end, j, side="right"), N_EXPERTS)
    fill = (fill_start[e] + (j - (fill_end[e] - fill_cnt[e]))).astype(jnp.int32)
    return pos, block_expert, n_used, fill


def kernel(x, w_in, b_forget, conv_w, conv_b, conv_ln_g, conv_ln_b, sc_w, w_branch, w_gate, b_gate,
           w_out, ln1_g, ln1_b, w_up, w_down, ln2_g, ln2_b, w_router, b_router):
    batch, seq, d = x.shape
    t = batch * seq
    depth = w_in.shape[0]
    x2 = x.reshape(t, d)

    nf = ATT_HEADS
    w_main = jnp.concatenate([w_in[:, :, :3 * WIDTH], w_in[:, :, 3 * WIDTH + nf:]], axis=2).astype(BF16)
    w_f = jnp.pad(w_in[:, :, 3 * WIDTH:3 * WIDTH + nf], ((0, 0), (0, 0), (0, LANES - nf))).astype(BF16)
    b_f = jnp.pad(b_forget, ((0, 0), (0, LANES - nf)))[:, None, :]
    wg = w_gate.astype(BF16)
    wb = w_branch.astype(BF16)
    wo = w_out.astype(BF16)
    wu = w_up.astype(BF16)
    wd = w_down.astype(BF16)
    wr_pad = jnp.pad(w_router, ((0, 0), (0, LANES - N_EXPERTS)))
    wr_hi = wr_pad.astype(BF16)
    wr_lo = (wr_pad - wr_hi.astype(F32)).astype(BF16)
    wr = jnp.stack([wr_hi, wr_lo])
    br = jnp.pad(b_router, (0, LANES - N_EXPERTS), constant_values=NEG_BIG)[None, :]

    n_blocks = (2 * t + N_EXPERTS * (EXPERT_BLOCK - 1) + EXPERT_BLOCK - 1) // EXPERT_BLOCK
    n_rows = n_blocks * EXPERT_BLOCK

    for l in range(depth):
        q, k, v, u, br_sc, c, ct = _in_proj(x2, w_main[l], w_f[l], b_f[l], sc_w[l], seq)
        br_conv = _conv31(u, conv_w[l], conv_b[l][None, :], conv_ln_g[l][None, :],
                          conv_ln_b[l][None, :], seq)
        br_att = _fox_attention(q, k, v, c, ct, batch, seq)
        x1, route, counts = _merge(x2, br_att, br_conv, br_sc, wg[l], b_gate[l][:, None, :], wb[l],
                                   wo[l], ln1_g[l][None, :], ln1_b[l][None, :], wr, br)
        pos, block_expert, n_used, fill = _routing_tables(route, counts, n_blocks)
        xs = _dispatch(x1, pos, fill, n_rows)
        ys = _experts(xs, wu[l], wd[l], block_expert, n_used)
        x2 = _combine(x1, route, pos, ys, ln2_g[l][None, :], ln2_b[l][None, :])
    return x2.reshape(batch, seq, d)
```

```python
import functools

import numpy as np
import jax
import jax.numpy as jnp
from jax import lax
from jax.experimental import pallas as pl
from jax.experimental.pallas import tpu as pltpu

F32 = jnp.float32
BF16 = jnp.bfloat16

ATT_HEADS = 8
ATT_HEAD_DIM = 64
WIDTH = 512
CONV_TAPS = 31
SC_TAPS = 3
N_EXPERTS = 32
EXPERTS_PER_GROUP = 4
EXPERT_FF = 512
DEPTH = 4
DEEPNORM_ALPHA = (2 * DEPTH) ** 0.25
LN_EPS = 1e-5

LANES = 128
SUBLANES = 8
VMEM_LIMIT = 56 * 1024 * 1024

TM_PROJ = 512
TM_CONV = 512
CONV_HALO = 32
CONV_CHUNK = 64
TQ = 512
TK = 512
EXPERT_BLOCK = 512
TM_ROUTE = 512
ZERO_ROWS = 256
NEG_BIG = -1e30
LOG2E = 1.4426950408889634


def _sigmoid(x):
    return 1.0 / (1.0 + jnp.exp(-x))


def _log_sigmoid(x):
    return jnp.minimum(x, 0.0) - jnp.log(1.0 + jnp.exp(-jnp.abs(x)))


def _split3(x):
    hi = x.astype(BF16)
    r1 = x - hi.astype(F32)
    mid = r1.astype(BF16)
    lo = (r1 - mid.astype(F32)).astype(BF16)
    return hi, mid, lo


def _layer_norm(y, g, b):
    mu = jnp.mean(y, axis=-1, keepdims=True)
    yc = y - mu
    var = jnp.mean(yc * yc, axis=-1, keepdims=True)
    return yc * lax.rsqrt(var + LN_EPS) * g + b


def _bias_placement():
    pairs = ATT_HEADS // 2
    place = np.zeros((3 * LANES, 2 * pairs * LANES), np.float32)
    ones = np.zeros((1, 2 * pairs * LANES), np.float32)
    for p in range(pairs):
        for h in range(2):
            for piece in range(3):
                src = piece * LANES + 2 * p + h
                place[src, p * LANES + 6 * h + 3 + piece] = 1.0
                place[src, (pairs + p) * LANES + 6 * h + piece] = -1.0
                ones[0, p * LANES + 6 * h + piece] = 1.0
                ones[0, (pairs + p) * LANES + 6 * h + 3 + piece] = 1.0
    return jnp.asarray(place, BF16), jnp.asarray(ones, F32)


def _in_proj_kernel(x_ref, wm_ref, wvt_ref, wf_ref, bf_ref, scw_ref, tri_ref, place_ref, ones_ref,
                    qf_ref, kf_ref, vt_ref, u_ref, brsc_ref,
                    carry_c, carry_p, *, tiles_per_seq):
    i = pl.program_id(0)

    @pl.when(i % tiles_per_seq == 0)
    def _():
        carry_c[...] = jnp.zeros_like(carry_c)
        carry_p[...] = jnp.zeros_like(carry_p)

    tm = x_ref.shape[0]
    xb = x_ref[...].astype(BF16)

    def mm(j):
        return jnp.dot(xb, wm_ref[:, j * WIDTH:(j + 1) * WIDTH], preferred_element_type=F32)

    q = (mm(0) * (ATT_HEAD_DIM ** -0.5 * LOG2E)).astype(BF16)
    k = mm(1).astype(BF16)
    vt_ref[...] = lax.dot_general(wvt_ref[...], xb, (((1,), (1,)), ((), ())),
                                  preferred_element_type=F32).astype(BF16)
    u_ref[...] = mm(2) * _sigmoid(mm(3))

    p = mm(5) * mm(6)
    row = lax.broadcasted_iota(jnp.int32, p.shape, 0)
    prev = carry_p[...]
    p1 = jnp.where(row == 0, prev[7:8, :], pltpu.roll(p, 1, axis=0))
    p2 = pltpu.roll(p, 2, axis=0)
    p2 = jnp.where(row == 0, prev[6:7, :], jnp.where(row == 1, prev[7:8, :], p2))
    y = scw_ref[0:1, :] * p2 + scw_ref[1:2, :] * p1 + scw_ref[2:3, :] * p
    brsc_ref[...] = (mm(4) * y).astype(BF16)
    carry_p[...] = p[tm - SUBLANES:tm, :]

    zf = jnp.dot(xb, wf_ref[...], preferred_element_type=F32)
    ls = _log_sigmoid(zf + bf_ref[...])
    hi, mid, lo = _split3(ls)
    tri = tri_ref[...]
    c = (jnp.dot(tri, hi, preferred_element_type=F32)
         + jnp.dot(tri, mid, preferred_element_type=F32)
         + jnp.dot(tri, lo, preferred_element_type=F32)) + carry_c[...]
    carry_c[...] = c[tm - 1:tm, :]

    bias = ones_ref[...]
    for n, piece in enumerate(_split3(c * LOG2E)):
        bias = bias + jnp.dot(piece, place_ref[n * LANES:(n + 1) * LANES, :],
                              preferred_element_type=F32)
    bias = bias.astype(BF16)
    pairs = ATT_HEADS // 2
    for p in range(pairs):
        lo_, hi_ = p * LANES, (p + 1) * LANES
        qf_ref[:, 2 * lo_:2 * lo_ + LANES] = q[:, lo_:hi_]
        qf_ref[:, 2 * lo_ + LANES:2 * hi_] = bias[:, lo_:hi_]
        kf_ref[:, 2 * lo_:2 * lo_ + LANES] = k[:, lo_:hi_]
        kf_ref[:, 2 * lo_ + LANES:2 * hi_] = bias[:, (pairs + p) * LANES:(pairs + p + 1) * LANES]


def _in_proj(x2, wm, wvt, wf, bf, scw, seq):
    t, d = x2.shape
    tm = min(TM_PROJ, seq)
    assert seq % tm == 0 and t % seq == 0
    tri = jnp.tril(jnp.ones((tm, tm), F32)).astype(BF16)
    place, ones = _bias_placement()
    const = lambda shape: pl.BlockSpec(shape, lambda i: (0,) * len(shape))
    rows = lambda w: pl.BlockSpec((tm, w), lambda i: (i, 0))
    return pl.pallas_call(
        functools.partial(_in_proj_kernel, tiles_per_seq=seq // tm),
        grid=(t // tm,),
        in_specs=[rows(d), const(wm.shape), const(wvt.shape), const(wf.shape), const(bf.shape),
                  const(scw.shape), const(tri.shape), const(place.shape), const(ones.shape)],
        out_specs=[rows(2 * WIDTH), rows(2 * WIDTH), pl.BlockSpec((WIDTH, tm), lambda i: (0, i)),
                   rows(WIDTH), rows(WIDTH)],
        out_shape=[jax.ShapeDtypeStruct((t, 2 * WIDTH), BF16),
                   jax.ShapeDtypeStruct((t, 2 * WIDTH), BF16),
                   jax.ShapeDtypeStruct((WIDTH, t), BF16),
                   jax.ShapeDtypeStruct((t, WIDTH), F32),
                   jax.ShapeDtypeStruct((t, WIDTH), BF16)],
        scratch_shapes=[pltpu.VMEM((1, LANES), F32), pltpu.VMEM((SUBLANES, WIDTH), F32)],
        compiler_params=pltpu.CompilerParams(dimension_semantics=("arbitrary",),
                                             vmem_limit_bytes=VMEM_LIMIT),
        name="in_proj",
    )(x2, wm, wvt, wf, bf, scw, tri, place, ones)


def _conv31_kernel(u_ref, w_ref, cb_ref, g_ref, b_ref, o_ref, ext, *, tiles_per_seq):
    i = pl.program_id(0)
    tm = u_ref.shape[0]

    @pl.when(i % tiles_per_seq == 0)
    def _():
        ext[0:CONV_HALO, :] = jnp.zeros((CONV_HALO, WIDTH), F32)

    ext[CONV_HALO:CONV_HALO + tm, :] = u_ref[...]
    base = CONV_HALO - (CONV_TAPS - 1)
    for c0 in range(0, tm, CONV_CHUNK):
        acc = jnp.zeros((CONV_CHUNK, WIDTH), F32)
        for k in range(CONV_TAPS):
            acc = acc + w_ref[k:k + 1, :] * ext[c0 + base + k:c0 + base + k + CONV_CHUNK, :]
        y = _layer_norm(acc + cb_ref[...], g_ref[...], b_ref[...])
        o_ref[c0:c0 + CONV_CHUNK, :] = (y * _sigmoid(y)).astype(BF16)
    ext[0:CONV_HALO, :] = ext[tm:tm + CONV_HALO, :]


def _conv31(u, w, cb, g, b, seq):
    t = u.shape[0]
    tm = min(TM_CONV, seq)
    assert seq % tm == 0 and tm % CONV_CHUNK == 0
    const = lambda shape: pl.BlockSpec(shape, lambda i: (0,) * len(shape))
    return pl.pallas_call(
        functools.partial(_conv31_kernel, tiles_per_seq=seq // tm),
        grid=(t // tm,),
        in_specs=[pl.BlockSpec((tm, WIDTH), lambda i: (i, 0)), const(w.shape), const(cb.shape),
                  const(g.shape), const(b.shape)],
        out_specs=pl.BlockSpec((tm, WIDTH), lambda i: (i, 0)),
        out_shape=jax.ShapeDtypeStruct((t, WIDTH), BF16),
        scratch_shapes=[pltpu.VMEM((CONV_HALO + tm, WIDTH), F32)],
        compiler_params=pltpu.CompilerParams(dimension_semantics=("arbitrary",),
                                             vmem_limit_bytes=VMEM_LIMIT),
        name="conv31",
    )(u, w, cb, g, b)


def _fox_kernel(qf_ref, kf_ref, vt_ref, o_ref, acc0_ref, acc1_ref, *, tk):
    qi = pl.program_id(2)
    tq = qf_ref.shape[0]
    lane = lax.broadcasted_iota(jnp.int32, (tq, 2 * LANES), 1)
    bias_lane = lane - LANES
    qfull = qf_ref[...]
    zero = jnp.zeros_like(qfull)
    qs = [jnp.where(((lane >= h * ATT_HEAD_DIM) & (lane < (h + 1) * ATT_HEAD_DIM))
                    | ((bias_lane >= 6 * h) & (bias_lane < 6 * h + 6)), qfull, zero)
          for h in range(2)]
    q_lo = qi * tq
    accs = (acc0_ref, acc1_ref)
    for acc in accs:
        acc[...] = jnp.zeros(acc.shape, F32)

    def tiles(jobs, carry):
        work = []
        for j, masked in jobs:
            start = pl.multiple_of(j * tk, tk)
            kfull = kf_ref[pl.ds(start, tk), :]
            ss = [lax.dot_general(kfull, qs[h], (((1,), (1,)), ((), ())),
                                  preferred_element_type=F32) for h in range(2)]
            work.append((start, masked, ss))
        carry = list(carry)
        for start, masked, ss in work:
            vt = vt_ref[:, pl.ds(start, tk)]
            if masked:
                key = start + lax.broadcasted_iota(jnp.int32, ss[0].shape, 0)
                qry = q_lo + lax.broadcasted_iota(jnp.int32, ss[0].shape, 1)
                ss = [jnp.where(key <= qry, s, NEG_BIG) for s in ss]
            for h in range(2):
                m_old, l_old = carry[h]
                m_new = jnp.maximum(m_old, jnp.max(ss[h], axis=0, keepdims=True))
                a = jnp.exp2(m_old - m_new)
                p = jnp.exp2(ss[h] - m_new)
                l_new = a * l_old + jnp.sum(p.reshape(tk // SUBLANES, SUBLANES, tq), axis=0)
                accs[h][...] = a * accs[h][...] + jnp.dot(vt, p.astype(BF16),
                                                          preferred_element_type=F32)
                carry[h] = (m_new, l_new)
        return tuple(carry)

    init = tuple((jnp.full((1, tq), NEG_BIG, F32), jnp.zeros((SUBLANES, tq), F32)) for _ in range(2))
    n_full = q_lo // tk
    carry = lax.fori_loop(0, n_full // 2,
                          lambda j, c: tiles([(2 * j, False), (2 * j + 1, False)], c), init)
    (_, l0), (_, l1) = lax.cond(
        n_full % 2 == 1,
        lambda c: tiles([(n_full - 1, False), (n_full, True)], c),
        lambda c: tiles([(n_full, True)], c),
        carry)
    row = lax.broadcasted_iota(jnp.int32, (LANES, tq), 0)
    out_t = jnp.where(row < ATT_HEAD_DIM,
                      acc0_ref[...] / jnp.sum(l0, axis=0, keepdims=True),
                      acc1_ref[...] / jnp.sum(l1, axis=0, keepdims=True))
    o_ref[...] = out_t.T.astype(BF16)


def _fox_attention(qf, kf, vt, batch, seq):
    t = qf.shape[0]
    tq = min(TQ, seq)
    tk = min(TK, seq)
    assert seq % tq == 0 and seq % tk == 0 and tk % tq == 0
    nq = seq // tq
    pairs = ATT_HEADS // 2
    return pl.pallas_call(
        functools.partial(_fox_kernel, tk=tk),
        grid=(batch, pairs, nq),
        in_specs=[pl.BlockSpec((tq, 2 * LANES), lambda b, hp, qi: (b * nq + qi, hp)),
                  pl.BlockSpec((seq, 2 * LANES), lambda b, hp, qi: (b, hp)),
                  pl.BlockSpec((LANES, seq), lambda b, hp, qi: (hp, b))],
        out_specs=pl.BlockSpec((tq, LANES), lambda b, hp, qi: (b * nq + qi, hp)),
        out_shape=jax.ShapeDtypeStruct((t, WIDTH), BF16),
        scratch_shapes=[pltpu.VMEM((LANES, tq), F32), pltpu.VMEM((LANES, tq), F32)],
        compiler_params=pltpu.CompilerParams(
            dimension_semantics=("arbitrary", "arbitrary", "arbitrary"),
            vmem_limit_bytes=VMEM_LIMIT),
        name="fox_attn",
    )(qf, kf, vt)


def _partner(x, lane, bit):
    up = pltpu.roll(x, LANES - bit, axis=1)
    down = pltpu.roll(x, bit, axis=1)
    return jnp.where((lane & bit) == 0, up, down)


def _merge_kernel(x_ref, att_ref, conv_ref, sc_ref, wg_ref, bg_ref, wb_ref, wo_ref,
                  g1_ref, b1_ref, wr_ref, br_ref, tril_ref,
                  x1_ref, route_ref, counts_ref, carry):
    i = pl.program_id(0)

    @pl.when(i == 0)
    def _():
        carry[...] = jnp.zeros_like(carry)

    x = x_ref[...]
    xb = x.astype(BF16)
    merged = None
    for n, br_ref_n in enumerate((att_ref, conv_ref, sc_ref)):
        gate = _sigmoid(jnp.dot(xb, wg_ref[n], preferred_element_type=F32) + bg_ref[n])
        term = gate * jnp.dot(br_ref_n[...], wb_ref[n], preferred_element_type=F32)
        merged = term if merged is None else merged + term
    h = jnp.dot(merged.astype(BF16), wo_ref[...], preferred_element_type=F32)
    x1 = _layer_norm(DEEPNORM_ALPHA * x + h, g1_ref[...], b1_ref[...])
    x1_ref[...] = x1

    x_hi = x1.astype(BF16)
    x_lo = (x1 - x_hi.astype(F32)).astype(BF16)
    logits = (jnp.dot(x_hi, wr_ref[0], preferred_element_type=F32)
              + jnp.dot(x_lo, wr_ref[0], preferred_element_type=F32)
              + jnp.dot(x_hi, wr_ref[1], preferred_element_type=F32)) + br_ref[...]
    mx = jnp.max(logits, axis=1, keepdims=True)
    ex = jnp.exp(logits - mx)
    p = ex / jnp.sum(ex, axis=1, keepdims=True)

    lane = lax.broadcasted_iota(jnp.int32, p.shape, 1)
    sub = lane & (EXPERTS_PER_GROUP - 1)
    rank = jnp.zeros(p.shape, jnp.int32)
    for d in range(1, EXPERTS_PER_GROUP):
        below = pltpu.roll(p, d, axis=1)
        above = pltpu.roll(p, LANES - d, axis=1)
        rank = rank + jnp.where((sub >= d) & (below >= p), 1, 0)
        rank = rank + jnp.where((sub + d < EXPERTS_PER_GROUP) & (above > p), 1, 0)
    top2 = rank < 2
    score = jnp.where(top2, p, 0.0)
    score = score + _partner(score, lane, 1)
    score = score + _partner(score, lane, 2)
    best = jnp.max(score, axis=1, keepdims=True)
    group = lane >> 2
    best_group = jnp.min(jnp.where(score == best, group, LANES), axis=1, keepdims=True)
    sel = top2 & (group == best_group)
    gate = jnp.where(sel, p / best, 0.0)

    sel_b = jnp.where(sel, 1.0, 0.0).astype(BF16)
    before = jnp.dot(tril_ref[...], sel_b, preferred_element_type=F32) + carry[...]
    tm = x.shape[0]
    new_carry = before[tm - 1:tm, :] + sel_b[tm - 1:tm, :].astype(F32)
    carry[...] = new_carry
    counts_ref[...] = new_carry

    e_lo = jnp.min(jnp.where(sel, lane, LANES), axis=1, keepdims=True)
    e_hi = jnp.max(jnp.where(sel, lane, -1), axis=1, keepdims=True)
    is_lo = lane == e_lo
    is_hi = lane == e_hi
    pick = lambda m, a: jnp.sum(jnp.where(m, a, 0.0), axis=1, keepdims=True)
    route = jnp.where(lane == 0, e_lo.astype(F32), 0.0)
    route = jnp.where(lane == 1, e_hi.astype(F32), route)
    route = jnp.where(lane == 2, pick(is_lo, before), route)
    route = jnp.where(lane == 3, pick(is_hi, before), route)
    route = jnp.where(lane == 4, pick(is_lo, gate), route)
    route = jnp.where(lane == 5, pick(is_hi, gate), route)
    route_ref[...] = route


def _merge(x2, att, conv, sc, wg, bg, wb, wo, g1, b1, wr, br):
    t, d = x2.shape
    tm = min(TM_PROJ, t)
    assert t % tm == 0
    tril = jnp.tril(jnp.ones((tm, tm), F32), -1).astype(BF16)
    const = lambda shape: pl.BlockSpec(shape, lambda i: (0,) * len(shape),
                                       pipeline_mode=pl.Buffered(1))
    rows = lambda w: pl.BlockSpec((tm, w), lambda i: (i, 0))
    return pl.pallas_call(
        _merge_kernel,
        grid=(t // tm,),
        in_specs=[rows(d), rows(WIDTH), rows(WIDTH), rows(WIDTH),
                  const(wg.shape), const(bg.shape), const(wb.shape), const(wo.shape),
                  const(g1.shape), const(b1.shape), const(wr.shape), const(br.shape),
                  const(tril.shape)],
        out_specs=[rows(d), rows(LANES), pl.BlockSpec((1, LANES), lambda i: (0, 0))],
        out_shape=[jax.ShapeDtypeStruct((t, d), F32),
                   jax.ShapeDtypeStruct((t, LANES), F32),
                   jax.ShapeDtypeStruct((1, LANES), F32)],
        scratch_shapes=[pltpu.VMEM((1, LANES), F32)],
        compiler_params=pltpu.CompilerParams(dimension_semantics=("arbitrary",),
                                             vmem_limit_bytes=VMEM_LIMIT),
        name="merge",
    )(x2, att, conv, sc, wg, bg, wb, wo, g1, b1, wr, br, tril)


def _index_prefetch(pos_ref, idx, isem, i, n_steps):
    slot = i % 2

    @pl.when(i == 0)
    def _():
        pltpu.make_async_copy(pos_ref.at[0], idx.at[0], isem.at[0]).start()

    @pl.when(i + 1 < n_steps)
    def _():
        pltpu.make_async_copy(pos_ref.at[i + 1], idx.at[1 - slot], isem.at[1 - slot]).start()

    pltpu.make_async_copy(pos_ref.at[i], idx.at[slot], isem.at[slot]).wait()
    return slot


def _dispatch_kernel(fs_ref, fl_ref, pos_ref, x_ref, xs_ref, idx, zeros, sem, isem):
    i = pl.program_id(0)
    tm = x_ref.shape[0]
    zb = zeros.shape[0]
    slot = _index_prefetch(pos_ref, idx, isem, i, pl.num_programs(0))

    @pl.when(i == 0)
    def _():
        zeros[...] = jnp.zeros_like(zeros)

        def region(e, _):
            start = fs_ref[e]
            length = fl_ref[e]
            head = jnp.minimum(length, (SUBLANES - start % SUBLANES) % SUBLANES)
            mid = start + head
            mid_len = (length - head) // SUBLANES * SUBLANES
            n_full = mid_len // zb
            rem = mid_len - n_full * zb
            part = mid + n_full * zb
            tail = mid + mid_len
            n_tail = length - head - mid_len

            def pieces(fn):
                def single(base):
                    def body(r, _):
                        fn(pltpu.make_async_copy(zeros.at[pl.ds(0, 1)],
                                                 xs_ref.at[pl.ds(base + r, 1)], sem))
                        return 0
                    return body
                lax.fori_loop(0, head, single(start), 0)
                lax.fori_loop(0, n_tail, single(tail), 0)

                def full(c, _):
                    off = pl.multiple_of(mid + c * zb, SUBLANES)
                    fn(pltpu.make_async_copy(zeros, xs_ref.at[pl.ds(off, zb)], sem))
                    return 0
                lax.fori_loop(0, n_full, full, 0)
                bit = zb // 2
                while bit >= SUBLANES:
                    off = pl.multiple_of(part + (rem - rem % (2 * bit)), SUBLANES)

                    @pl.when((rem & bit) != 0)
                    def _(bit=bit, off=off):
                        fn(pltpu.make_async_copy(zeros.at[pl.ds(0, bit)],
                                                 xs_ref.at[pl.ds(off, bit)], sem))
                    bit //= 2

            pieces(lambda cp: cp.start())
            pieces(lambda cp: cp.wait())
            return 0

        lax.fori_loop(0, fs_ref.shape[0], region, 0)

    def issue(r, _):
        src = x_ref.at[pl.ds(r, 1)]
        pltpu.make_async_copy(src, xs_ref.at[pl.ds(idx[slot, 2 * r], 1)], sem).start()
        pltpu.make_async_copy(src, xs_ref.at[pl.ds(idx[slot, 2 * r + 1], 1)], sem).start()
        return 0

    def drain(r, _):
        pltpu.make_async_copy(x_ref.at[pl.ds(0, 1)], xs_ref.at[pl.ds(0, 1)], sem).wait()
        return 0

    lax.fori_loop(0, tm, issue, 0, unroll=8)
    lax.fori_loop(0, 2 * tm, drain, 0, unroll=16)


def _dispatch(x1, pos, fill_start, fill_len, n_rows):
    t, d = x1.shape
    tm = min(TM_ROUTE, t)
    assert t % tm == 0
    pos2 = pos.reshape(t // tm, 2 * tm)
    return pl.pallas_call(
        _dispatch_kernel,
        grid_spec=pltpu.PrefetchScalarGridSpec(
            num_scalar_prefetch=2,
            grid=(t // tm,),
            in_specs=[pl.BlockSpec(memory_space=pl.ANY),
                      pl.BlockSpec((tm, d), lambda i, fs, fl: (i, 0))],
            out_specs=pl.BlockSpec(memory_space=pl.ANY),
            scratch_shapes=[pltpu.SMEM((2, 2 * tm), jnp.int32), pltpu.VMEM((ZERO_ROWS, d), F32),
                            pltpu.SemaphoreType.DMA, pltpu.SemaphoreType.DMA((2,))]),
        out_shape=jax.ShapeDtypeStruct((n_rows, d), F32),
        compiler_params=pltpu.CompilerParams(dimension_semantics=("arbitrary",),
                                             vmem_limit_bytes=VMEM_LIMIT),
        name="dispatch",
    )(fill_start, fill_len, pos2, x1)


def _expert_kernel(be_ref, nb_ref, xs_ref, wu_ref, wd_ref, ys_ref):
    b = pl.program_id(0)

    @pl.when(b < nb_ref[0])
    def _():
        xb = xs_ref[...].astype(BF16)
        hidden = jnp.dot(xb, wu_ref[0], preferred_element_type=F32)
        a = hidden[:, :EXPERT_FF]
        g = hidden[:, EXPERT_FF:]
        act = (g * _sigmoid(g) * a).astype(BF16)
        ys_ref[...] = jnp.dot(act, wd_ref[0], preferred_element_type=F32)

    @pl.when(b >= nb_ref[0])
    def _():
        ys_ref[...] = jnp.zeros_like(ys_ref)


def _experts(xs, wu, wd, block_expert, n_used):
    n_rows, d = xs.shape
    n_blocks = n_rows // EXPERT_BLOCK

    def live(b, nb):
        return jnp.minimum(b, nb[0] - 1)

    return pl.pallas_call(
        _expert_kernel,
        grid_spec=pltpu.PrefetchScalarGridSpec(
            num_scalar_prefetch=2,
            grid=(n_blocks,),
            in_specs=[pl.BlockSpec((EXPERT_BLOCK, d), lambda b, be, nb: (live(b, nb), 0)),
                      pl.BlockSpec((1,) + wu.shape[1:], lambda b, be, nb: (be[live(b, nb)], 0, 0)),
                      pl.BlockSpec((1,) + wd.shape[1:], lambda b, be, nb: (be[live(b, nb)], 0, 0))],
            out_specs=pl.BlockSpec((EXPERT_BLOCK, d), lambda b, be, nb: (b, 0))),
        out_shape=jax.ShapeDtypeStruct((n_rows, d), F32),
        compiler_params=pltpu.CompilerParams(dimension_semantics=("arbitrary",),
                                             vmem_limit_bytes=VMEM_LIMIT),
        name="experts",
    )(block_expert, n_used, xs, wu, wd)


def _combine_kernel(pos_ref, x1_ref, route_ref, ys_ref, g2_ref, b2_ref, o_ref,
                    idx, rows, sem, isem):
    i = pl.program_id(0)
    tm = x1_ref.shape[0]
    slot = _index_prefetch(pos_ref, idx, isem, i, pl.num_programs(0))

    def issue(r, _):
        for k in range(2):
            src = ys_ref.at[pl.ds(idx[slot, 2 * r + k], 1)]
            pltpu.make_async_copy(src, rows.at[k, pl.ds(r, 1)], sem).start()
        return 0

    def drain(r, _):
        pltpu.make_async_copy(ys_ref.at[pl.ds(0, 1)], rows.at[0, pl.ds(0, 1)], sem).wait()
        return 0

    lax.fori_loop(0, tm, issue, 0, unroll=8)
    lax.fori_loop(0, 2 * tm, drain, 0, unroll=16)
    route = route_ref[...]
    m = route[:, 4:5] * rows[0] + route[:, 5:6] * rows[1]
    o_ref[...] = _layer_norm(DEEPNORM_ALPHA * x1_ref[...] + m, g2_ref[...], b2_ref[...])


def _combine(x1, route, pos, ys, g2, b2):
    t, d = x1.shape
    tm = min(TM_ROUTE, t)
    pos2 = pos.reshape(t // tm, 2 * tm)
    const = lambda shape: pl.BlockSpec(shape, lambda i: (0,) * len(shape))
    return pl.pallas_call(
        _combine_kernel,
        grid=(t // tm,),
        in_specs=[pl.BlockSpec(memory_space=pl.ANY),
                  pl.BlockSpec((tm, d), lambda i: (i, 0)),
                  pl.BlockSpec((tm, LANES), lambda i: (i, 0)),
                  pl.BlockSpec(memory_space=pl.ANY),
                  const(g2.shape), const(b2.shape)],
        out_specs=pl.BlockSpec((tm, d), lambda i: (i, 0)),
        out_shape=jax.ShapeDtypeStruct((t, d), F32),
        scratch_shapes=[pltpu.SMEM((2, 2 * tm), jnp.int32), pltpu.VMEM((2, tm, d), F32),
                        pltpu.SemaphoreType.DMA, pltpu.SemaphoreType.DMA((2,))],
        compiler_params=pltpu.CompilerParams(dimension_semantics=("arbitrary",),
                                             vmem_limit_bytes=VMEM_LIMIT),
        name="combine",
    )(pos2, x1, route, ys, g2, b2)


def _routing_tables(route, counts, n_blocks):
    t = route.shape[0]
    cnt = counts[0, :N_EXPERTS].astype(jnp.int32)
    padded = (cnt + EXPERT_BLOCK - 1) // EXPERT_BLOCK * EXPERT_BLOCK
    pad_end = jnp.cumsum(padded)
    pad_start = pad_end - padded
    experts = route[:, 0:2].astype(jnp.int32)
    ranks = route[:, 2:4].astype(jnp.int32)
    ids = jnp.arange(N_EXPERTS, dtype=jnp.int32)
    base = jnp.sum(jnp.where(experts[:, :, None] == ids, pad_start, 0), axis=-1)
    pos = (base + ranks).reshape(2 * t)
    blk = jnp.arange(n_blocks, dtype=jnp.int32) * EXPERT_BLOCK
    block_expert = jnp.minimum(jnp.sum((pad_end[None, :] <= blk[:, None]).astype(jnp.int32), axis=1),
                               N_EXPERTS - 1)
    n_used = (pad_end[-1:] // EXPERT_BLOCK).astype(jnp.int32)
    n_rows = n_blocks * EXPERT_BLOCK
    fill_start = jnp.concatenate([pad_start + cnt, pad_end[-1:]]).astype(jnp.int32)
    fill_len = jnp.concatenate([padded - cnt, n_rows - pad_end[-1:]]).astype(jnp.int32)
    return pos, block_expert, n_used, fill_start, fill_len


def kernel(x, w_in, b_forget, conv_w, conv_b, conv_ln_g, conv_ln_b, sc_w, w_branch, w_gate, b_gate,
           w_out, ln1_g, ln1_b, w_up, w_down, ln2_g, ln2_b, w_router, b_router):
    batch, seq, d = x.shape
    t = batch * seq
    depth = w_in.shape[0]
    x2 = x.reshape(t, d)

    nf = ATT_HEADS
    w_main = jnp.concatenate([w_in[:, :, :2 * WIDTH], w_in[:, :, 3 * WIDTH + nf:]], axis=2).astype(BF16)
    w_vt = jnp.swapaxes(w_in[:, :, 2 * WIDTH:3 * WIDTH], 1, 2).astype(BF16)
    w_f = jnp.pad(w_in[:, :, 3 * WIDTH:3 * WIDTH + nf], ((0, 0), (0, 0), (0, LANES - nf))).astype(BF16)
    b_f = jnp.pad(b_forget, ((0, 0), (0, LANES - nf)))[:, None, :]
    wg = w_gate.astype(BF16)
    wb = w_branch.astype(BF16)
    wo = w_out.astype(BF16)
    wu = w_up.astype(BF16)
    wd = w_down.astype(BF16)
    wr_pad = jnp.pad(w_router, ((0, 0), (0, LANES - N_EXPERTS)))
    wr_hi = wr_pad.astype(BF16)
    wr_lo = (wr_pad - wr_hi.astype(F32)).astype(BF16)
    wr = jnp.stack([wr_hi, wr_lo])
    br = jnp.pad(b_router, (0, LANES - N_EXPERTS), constant_values=NEG_BIG)[None, :]

    n_blocks = (2 * t + N_EXPERTS * (EXPERT_BLOCK - 1) + EXPERT_BLOCK - 1) // EXPERT_BLOCK
    n_rows = n_blocks * EXPERT_BLOCK

    for l in range(depth):
        qf, kf, vt, u, br_sc = _in_proj(x2, w_main[l], w_vt[l], w_f[l], b_f[l], sc_w[l], seq)
        br_conv = _conv31(u, conv_w[l], conv_b[l][None, :], conv_ln_g[l][None, :],
                          conv_ln_b[l][None, :], seq)
        br_att = _fox_attention(qf, kf, vt, batch, seq)
        x1, route, counts = _merge(x2, br_att, br_conv, br_sc, wg[l], b_gate[l][:, None, :], wb[l],
                                   wo[l], ln1_g[l][None, :], ln1_b[l][None, :], wr, br)
        pos, block_expert, n_used, fill_start, fill_len = _routing_tables(route, counts, n_blocks)
        xs = _dispatch(x1, pos, fill_start, fill_len, n_rows)
        ys = _experts(xs, wu[l], wd[l], block_expert, n_used)
        x2 = _combine(x1, route, pos, ys, ln2_g[l][None, :], ln2_b[l][None, :])
    return x2.reshape(batch, seq, d)
```

```python
import functools

import numpy as np
import jax
import jax.numpy as jnp
from jax import lax
from jax.experimental import pallas as pl
from jax.experimental.pallas import tpu as pltpu

F32 = jnp.float32
BF16 = jnp.bfloat16

ATT_HEADS = 8
ATT_HEAD_DIM = 64
WIDTH = 512
CONV_TAPS = 31
SC_TAPS = 3
N_EXPERTS = 32
EXPERTS_PER_GROUP = 4
EXPERT_FF = 512
DEPTH = 4
DEEPNORM_ALPHA = (2 * DEPTH) ** 0.25
LN_EPS = 1e-5

LANES = 128
SUBLANES = 8
VMEM_LIMIT = 56 * 1024 * 1024

TM_PROJ = 512
TM_CONV = 512
CONV_HALO = 32
CONV_CHUNK = 64
TQ = 512
TK = 512
ATT_TILES_PER_BLOCK = 4
EXPERT_BLOCK = 512
TM_ROUTE = 512
ROUTE_ROWS = 8
ZERO_ROWS = 256
NEG_BIG = -1e30
LOG2E = 1.4426950408889634


def _sigmoid(x):
    return 1.0 / (1.0 + jnp.exp(-x))


def _log_sigmoid(x):
    return jnp.minimum(x, 0.0) - jnp.log(1.0 + jnp.exp(-jnp.abs(x)))


def _split3(x):
    hi = x.astype(BF16)
    r1 = x - hi.astype(F32)
    mid = r1.astype(BF16)
    lo = (r1 - mid.astype(F32)).astype(BF16)
    return hi, mid, lo


def _layer_norm(y, g, b):
    mu = jnp.mean(y, axis=-1, keepdims=True)
    yc = y - mu
    var = jnp.mean(yc * yc, axis=-1, keepdims=True)
    return yc * lax.rsqrt(var + LN_EPS) * g + b


def _bias_placement():
    pairs = ATT_HEADS // 2
    place = np.zeros((3 * LANES, 2 * pairs * LANES), np.float32)
    ones = np.zeros((1, 2 * pairs * LANES), np.float32)
    for p in range(pairs):
        for h in range(2):
            for piece in range(3):
                src = piece * LANES + 2 * p + h
                place[src, p * LANES + 6 * h + 3 + piece] = 1.0
                place[src, (pairs + p) * LANES + 6 * h + piece] = -1.0
                ones[0, p * LANES + 6 * h + piece] = 1.0
                ones[0, (pairs + p) * LANES + 6 * h + 3 + piece] = 1.0
    return jnp.asarray(place, BF16), jnp.asarray(ones, F32)


def _in_proj_kernel(x_ref, wm_ref, wvt_ref, wf_ref, bf_ref, scw_ref, tri_ref, place_ref, ones_ref,
                    qf_ref, kf_ref, vt_ref, u_ref, brsc_ref,
                    carry_c, carry_p, *, tiles_per_seq):
    i = pl.program_id(0)

    @pl.when(i % tiles_per_seq == 0)
    def _():
        carry_c[...] = jnp.zeros_like(carry_c)
        carry_p[...] = jnp.zeros_like(carry_p)

    tm = x_ref.shape[0]
    xb = x_ref[...].astype(BF16)

    def mm(j):
        return jnp.dot(xb, wm_ref[:, j * WIDTH:(j + 1) * WIDTH], preferred_element_type=F32)

    q = (mm(0) * (ATT_HEAD_DIM ** -0.5 * LOG2E)).astype(BF16)
    k = mm(1).astype(BF16)
    vt_ref[...] = lax.dot_general(wvt_ref[...], xb, (((1,), (1,)), ((), ())),
                                  preferred_element_type=F32).astype(BF16)
    u_ref[...] = mm(2) * _sigmoid(mm(3))

    p = mm(5) * mm(6)
    row = lax.broadcasted_iota(jnp.int32, p.shape, 0)
    prev = carry_p[...]
    p1 = jnp.where(row == 0, prev[7:8, :], pltpu.roll(p, 1, axis=0))
    p2 = pltpu.roll(p, 2, axis=0)
    p2 = jnp.where(row == 0, prev[6:7, :], jnp.where(row == 1, prev[7:8, :], p2))
    y = scw_ref[0:1, :] * p2 + scw_ref[1:2, :] * p1 + scw_ref[2:3, :] * p
    brsc_ref[...] = (mm(4) * y).astype(BF16)
    carry_p[...] = p[tm - SUBLANES:tm, :]

    zf = jnp.dot(xb, wf_ref[...], preferred_element_type=F32)
    ls = _log_sigmoid(zf + bf_ref[...])
    hi, mid, lo = _split3(ls)
    tri = tri_ref[...]
    c = (jnp.dot(tri, hi, preferred_element_type=F32)
         + jnp.dot(tri, mid, preferred_element_type=F32)
         + jnp.dot(tri, lo, preferred_element_type=F32)) + carry_c[...]
    carry_c[...] = c[tm - 1:tm, :]

    bias = ones_ref[...]
    for n, piece in enumerate(_split3(c * LOG2E)):
        bias = bias + jnp.dot(piece, place_ref[n * LANES:(n + 1) * LANES, :],
                              preferred_element_type=F32)
    bias = bias.astype(BF16)
    pairs = ATT_HEADS // 2
    for p in range(pairs):
        lo_, hi_ = p * LANES, (p + 1) * LANES
        qf_ref[:, 2 * lo_:2 * lo_ + LANES] = q[:, lo_:hi_]
        qf_ref[:, 2 * lo_ + LANES:2 * hi_] = bias[:, lo_:hi_]
        kf_ref[:, 2 * lo_:2 * lo_ + LANES] = k[:, lo_:hi_]
        kf_ref[:, 2 * lo_ + LANES:2 * hi_] = bias[:, (pairs + p) * LANES:(pairs + p + 1) * LANES]


def _in_proj(x2, wm, wvt, wf, bf, scw, seq):
    t, d = x2.shape
    tm = min(TM_PROJ, seq)
    assert seq % tm == 0 and t % seq == 0
    tri = jnp.tril(jnp.ones((tm, tm), F32)).astype(BF16)
    place, ones = _bias_placement()
    const = lambda shape: pl.BlockSpec(shape, lambda i: (0,) * len(shape))
    rows = lambda w: pl.BlockSpec((tm, w), lambda i: (i, 0))
    return pl.pallas_call(
        functools.partial(_in_proj_kernel, tiles_per_seq=seq // tm),
        grid=(t // tm,),
        in_specs=[rows(d), const(wm.shape), const(wvt.shape), const(wf.shape), const(bf.shape),
                  const(scw.shape), const(tri.shape), const(place.shape), const(ones.shape)],
        out_specs=[rows(2 * WIDTH), rows(2 * WIDTH), pl.BlockSpec((WIDTH, tm), lambda i: (0, i)),
                   rows(WIDTH), rows(WIDTH)],
        out_shape=[jax.ShapeDtypeStruct((t, 2 * WIDTH), BF16),
                   jax.ShapeDtypeStruct((t, 2 * WIDTH), BF16),
                   jax.ShapeDtypeStruct((WIDTH, t), BF16),
                   jax.ShapeDtypeStruct((t, WIDTH), F32),
                   jax.ShapeDtypeStruct((t, WIDTH), BF16)],
        scratch_shapes=[pltpu.VMEM((1, LANES), F32), pltpu.VMEM((SUBLANES, WIDTH), F32)],
        compiler_params=pltpu.CompilerParams(dimension_semantics=("arbitrary",),
                                             vmem_limit_bytes=VMEM_LIMIT),
        name="in_proj",
    )(x2, wm, wvt, wf, bf, scw, tri, place, ones)


def _conv31_kernel(u_ref, w_ref, cb_ref, g_ref, b_ref, o_ref, ext, pre, *, tiles_per_seq):
    i = pl.program_id(0)
    tm = u_ref.shape[0]

    @pl.when(i % tiles_per_seq == 0)
    def _():
        ext[0:CONV_HALO, :] = jnp.zeros((CONV_HALO, WIDTH), F32)
        ext[CONV_HALO + tm:CONV_HALO + tm + SUBLANES, :] = jnp.zeros((SUBLANES, WIDTH), F32)

    ext[CONV_HALO:CONV_HALO + tm, :] = u_ref[...]
    base = CONV_HALO - (CONV_TAPS - 1)
    span = CONV_CHUNK + SUBLANES
    half = WIDTH // 2
    for c0 in range(0, tm, CONV_CHUNK):
        for l0 in (0, half):
            y = None
            for r in range(SUBLANES):
                z = None
                for a in range((base + CONV_TAPS - 1) // SUBLANES + 1):
                    k = SUBLANES * a + r - base
                    if 0 <= k < CONV_TAPS:
                        term = (w_ref[k:k + 1, l0:l0 + half]
                                * ext[c0 + SUBLANES * a:c0 + SUBLANES * a + span, l0:l0 + half])
                        z = term if z is None else z + term
                zr = z[r:r + CONV_CHUNK, :]
                y = zr if y is None else y + zr
            pre[c0:c0 + CONV_CHUNK, l0:l0 + half] = y
    for c0 in range(0, tm, CONV_CHUNK):
        y = _layer_norm(pre[c0:c0 + CONV_CHUNK, :] + cb_ref[...], g_ref[...], b_ref[...])
        o_ref[c0:c0 + CONV_CHUNK, :] = (y * _sigmoid(y)).astype(BF16)
    ext[0:CONV_HALO, :] = ext[tm:tm + CONV_HALO, :]


def _conv31(u, w, cb, g, b, seq):
    t = u.shape[0]
    tm = min(TM_CONV, seq)
    assert seq % tm == 0 and tm % CONV_CHUNK == 0
    const = lambda shape: pl.BlockSpec(shape, lambda i: (0,) * len(shape))
    return pl.pallas_call(
        functools.partial(_conv31_kernel, tiles_per_seq=seq // tm),
        grid=(t // tm,),
        in_specs=[pl.BlockSpec((tm, WIDTH), lambda i: (i, 0)), const(w.shape), const(cb.shape),
                  const(g.shape), const(b.shape)],
        out_specs=pl.BlockSpec((tm, WIDTH), lambda i: (i, 0)),
        out_shape=jax.ShapeDtypeStruct((t, WIDTH), BF16),
        scratch_shapes=[pltpu.VMEM((CONV_HALO + tm + SUBLANES, WIDTH), F32),
                        pltpu.VMEM((tm, WIDTH), F32)],
        compiler_params=pltpu.CompilerParams(dimension_semantics=("arbitrary",),
                                             vmem_limit_bytes=VMEM_LIMIT),
        name="conv31",
    )(u, w, cb, g, b)


def _fox_kernel(qf_ref, kf_ref, vt_ref, o_ref, acc0_ref, acc1_ref, *, tk):
    qi = pl.program_id(2)
    tq = qf_ref.shape[0]
    lane = lax.broadcasted_iota(jnp.int32, (tq, 2 * LANES), 1)
    bias_lane = lane - LANES
    qfull = qf_ref[...]
    zero = jnp.zeros_like(qfull)
    qs = [jnp.where(((lane >= h * ATT_HEAD_DIM) & (lane < (h + 1) * ATT_HEAD_DIM))
                    | ((bias_lane >= 6 * h) & (bias_lane < 6 * h + 6)), qfull, zero)
          for h in range(2)]
    q_lo = qi * tq
    accs = (acc0_ref, acc1_ref)
    for acc in accs:
        acc[...] = jnp.zeros(acc.shape, F32)

    def tiles(jobs, carry):
        work = []
        for j, masked in jobs:
            start = pl.multiple_of(j * tk, tk)
            kfull = kf_ref[pl.ds(start, tk), :]
            ss = [lax.dot_general(kfull, qs[h], (((1,), (1,)), ((), ())),
                                  preferred_element_type=F32) for h in range(2)]
            work.append((start, masked, ss))
        carry = list(carry)
        for start, masked, ss in work:
            vt = vt_ref[:, pl.ds(start, tk)]
            if masked:
                key = start + lax.broadcasted_iota(jnp.int32, ss[0].shape, 0)
                qry = q_lo + lax.broadcasted_iota(jnp.int32, ss[0].shape, 1)
                ss = [jnp.where(key <= qry, s, NEG_BIG) for s in ss]
            for h in range(2):
                m_old, l_old = carry[h]
                m_new = jnp.maximum(m_old, jnp.max(ss[h], axis=0, keepdims=True))
                a = jnp.exp2(m_old - m_new)
                p = jnp.exp2(ss[h] - m_new)
                l_new = a * l_old + jnp.sum(p.reshape(tk // SUBLANES, SUBLANES, tq), axis=0)
                accs[h][...] = a * accs[h][...] + jnp.dot(vt, p.astype(BF16),
                                                          preferred_element_type=F32)
                carry[h] = (m_new, l_new)
        return tuple(carry)

    init = tuple((jnp.full((1, tq), NEG_BIG, F32), jnp.zeros((SUBLANES, tq), F32)) for _ in range(2))
    n_full = q_lo // tk
    group = ATT_TILES_PER_BLOCK
    carry = lax.fori_loop(
        0, n_full // group,
        lambda j, c: tiles([(group * j + i, False) for i in range(group)], c), init)
    rest = n_full % group

    def tail(n):
        return lambda c: tiles([(n_full - n + i, False) for i in range(n)] + [(n_full, True)], c)

    (_, l0), (_, l1) = lax.switch(rest, [tail(n) for n in range(group)], carry)
    row = lax.broadcasted_iota(jnp.int32, (LANES, tq), 0)
    out_t = jnp.where(row < ATT_HEAD_DIM,
                      acc0_ref[...] / jnp.sum(l0, axis=0, keepdims=True),
                      acc1_ref[...] / jnp.sum(l1, axis=0, keepdims=True))
    o_ref[...] = out_t.T.astype(BF16)


def _fox_attention(qf, kf, vt, batch, seq):
    t = qf.shape[0]
    tq = min(TQ, seq)
    tk = min(TK, seq)
    assert seq % tq == 0 and seq % tk == 0 and tk % tq == 0
    nq = seq // tq
    pairs = ATT_HEADS // 2
    return pl.pallas_call(
        functools.partial(_fox_kernel, tk=tk),
        grid=(batch, pairs, nq),
        in_specs=[pl.BlockSpec((tq, 2 * LANES), lambda b, hp, qi: (b * nq + qi, hp)),
                  pl.BlockSpec((seq, 2 * LANES), lambda b, hp, qi: (b, hp)),
                  pl.BlockSpec((LANES, seq), lambda b, hp, qi: (hp, b))],
        out_specs=pl.BlockSpec((tq, LANES), lambda b, hp, qi: (b * nq + qi, hp)),
        out_shape=jax.ShapeDtypeStruct((t, WIDTH), BF16),
        scratch_shapes=[pltpu.VMEM((LANES, tq), F32), pltpu.VMEM((LANES, tq), F32)],
        compiler_params=pltpu.CompilerParams(
            dimension_semantics=("arbitrary", "arbitrary", "arbitrary"),
            vmem_limit_bytes=VMEM_LIMIT),
        name="fox_attn",
    )(qf, kf, vt)


def _partner(x, row, bit):
    n = x.shape[0]
    up = pltpu.roll(x, n - bit, axis=0)
    down = pltpu.roll(x, bit, axis=0)
    return jnp.where((row & bit) == 0, up, down)


def _merge_kernel(x_ref, att_ref, conv_ref, sc_ref, wg_ref, bg_ref, wb_ref, wo_ref,
                  g1_ref, b1_ref, wr_ref, br_ref, triu_ref,
                  x1_ref, route_ref, counts_ref, carry):
    i = pl.program_id(0)

    @pl.when(i == 0)
    def _():
        carry[...] = jnp.zeros_like(carry)

    x = x_ref[...]
    xb = x.astype(BF16)
    merged = None
    for n, br_ref_n in enumerate((att_ref, conv_ref, sc_ref)):
        gate = _sigmoid(jnp.dot(xb, wg_ref[n], preferred_element_type=F32) + bg_ref[n])
        term = gate * jnp.dot(br_ref_n[...], wb_ref[n], preferred_element_type=F32)
        merged = term if merged is None else merged + term
    h = jnp.dot(merged.astype(BF16), wo_ref[...], preferred_element_type=F32)
    x1 = _layer_norm(DEEPNORM_ALPHA * x + h, g1_ref[...], b1_ref[...])
    x1_ref[...] = x1

    x_hi = x1.astype(BF16)
    x_lo = (x1 - x_hi.astype(F32)).astype(BF16)
    nt = lambda w, v: lax.dot_general(w, v, (((1,), (1,)), ((), ())), preferred_element_type=F32)
    logits = nt(wr_ref[0], x_hi) + nt(wr_ref[0], x_lo) + nt(wr_ref[1], x_hi) + br_ref[...]
    mx = jnp.max(logits, axis=0, keepdims=True)
    ex = jnp.exp(logits - mx)
    p = ex / jnp.sum(ex, axis=0, keepdims=True)

    row = lax.broadcasted_iota(jnp.int32, p.shape, 0)
    sub = row & (EXPERTS_PER_GROUP - 1)
    rank = jnp.zeros(p.shape, jnp.int32)
    for d in range(1, EXPERTS_PER_GROUP):
        below = pltpu.roll(p, d, axis=0)
        above = pltpu.roll(p, N_EXPERTS - d, axis=0)
        rank = rank + jnp.where((sub >= d) & (below >= p), 1, 0)
        rank = rank + jnp.where((sub + d < EXPERTS_PER_GROUP) & (above > p), 1, 0)
    top2 = rank < 2
    score = jnp.where(top2, p, 0.0)
    score = score + _partner(score, row, 1)
    score = score + _partner(score, row, 2)
    best = jnp.max(score, axis=0, keepdims=True)
    group = row >> 2
    best_group = jnp.min(jnp.where(score == best, group, N_EXPERTS), axis=0, keepdims=True)
    sel = top2 & (group == best_group)
    gate = jnp.where(sel, p / best, 0.0)

    sel_b = jnp.where(sel, 1.0, 0.0).astype(BF16)
    before = jnp.dot(sel_b, triu_ref[...], preferred_element_type=F32) + carry[...]
    tm = x.shape[0]
    new_carry = before[:, tm - 1:tm] + sel_b[:, tm - 1:tm].astype(F32)
    carry[...] = new_carry
    counts_ref[...] = new_carry

    e_lo = jnp.min(jnp.where(sel, row, N_EXPERTS), axis=0, keepdims=True)
    e_hi = jnp.max(jnp.where(sel, row, -1), axis=0, keepdims=True)
    is_lo = row == e_lo
    is_hi = row == e_hi
    pick = lambda m, a: jnp.sum(jnp.where(m, a, 0.0), axis=0, keepdims=True)
    zero = jnp.zeros_like(mx)
    route_ref[...] = jnp.concatenate(
        [e_lo.astype(F32), e_hi.astype(F32), pick(is_lo, before), pick(is_hi, before),
         pick(is_lo, gate), pick(is_hi, gate), zero, zero], axis=0)


def _merge(x2, att, conv, sc, wg, bg, wb, wo, g1, b1, wr, br):
    t, d = x2.shape
    tm = min(TM_PROJ, t)
    assert t % tm == 0
    triu = jnp.triu(jnp.ones((tm, tm), F32), 1).astype(BF16)
    const = lambda shape: pl.BlockSpec(shape, lambda i: (0,) * len(shape),
                                       pipeline_mode=pl.Buffered(1))
    rows = lambda w: pl.BlockSpec((tm, w), lambda i: (i, 0))
    return pl.pallas_call(
        _merge_kernel,
        grid=(t // tm,),
        in_specs=[rows(d), rows(WIDTH), rows(WIDTH), rows(WIDTH),
                  const(wg.shape), const(bg.shape), const(wb.shape), const(wo.shape),
                  const(g1.shape), const(b1.shape), const(wr.shape), const(br.shape),
                  const(triu.shape)],
        out_specs=[rows(d), pl.BlockSpec((ROUTE_ROWS, tm), lambda i: (0, i)),
                   pl.BlockSpec((N_EXPERTS, 1), lambda i: (0, 0))],
        out_shape=[jax.ShapeDtypeStruct((t, d), F32),
                   jax.ShapeDtypeStruct((ROUTE_ROWS, t), F32),
                   jax.ShapeDtypeStruct((N_EXPERTS, 1), F32)],
        scratch_shapes=[pltpu.VMEM((N_EXPERTS, 1), F32)],
        compiler_params=pltpu.CompilerParams(dimension_semantics=("arbitrary",),
                                             vmem_limit_bytes=VMEM_LIMIT),
        name="merge",
    )(x2, att, conv, sc, wg, bg, wb, wo, g1, b1, wr, br, triu)


def _index_prefetch(pos_ref, idx, isem, i, n_steps):
    slot = i % 2

    @pl.when(i == 0)
    def _():
        pltpu.make_async_copy(pos_ref.at[0], idx.at[0], isem.at[0]).start()

    @pl.when(i + 1 < n_steps)
    def _():
        pltpu.make_async_copy(pos_ref.at[i + 1], idx.at[1 - slot], isem.at[1 - slot]).start()

    pltpu.make_async_copy(pos_ref.at[i], idx.at[slot], isem.at[slot]).wait()
    return slot


def _dispatch_kernel(fs_ref, fl_ref, pos_ref, x_ref, xs_ref, idx, zeros, sem, isem):
    i = pl.program_id(0)
    tm = x_ref.shape[0]
    zb = zeros.shape[0]
    slot = _index_prefetch(pos_ref, idx, isem, i, pl.num_programs(0))

    @pl.when(i == 0)
    def _():
        zeros[...] = jnp.zeros_like(zeros)

        def region(e, _):
            start = fs_ref[e]
            length = fl_ref[e]
            head = jnp.minimum(length, (SUBLANES - start % SUBLANES) % SUBLANES)
            mid = start + head
            mid_len = (length - head) // SUBLANES * SUBLANES
            n_full = mid_len // zb
            rem = mid_len - n_full * zb
            part = mid + n_full * zb
            tail = mid + mid_len
            n_tail = length - head - mid_len

            def pieces(fn):
                def single(base):
                    def body(r, _):
                        fn(pltpu.make_async_copy(zeros.at[pl.ds(0, 1)],
                                                 xs_ref.at[pl.ds(base + r, 1)], sem))
                        return 0
                    return body
                lax.fori_loop(0, head, single(start), 0)
                lax.fori_loop(0, n_tail, single(tail), 0)

                def full(c, _):
                    off = pl.multiple_of(mid + c * zb, SUBLANES)
                    fn(pltpu.make_async_copy(zeros, xs_ref.at[pl.ds(off, zb)], sem))
                    return 0
                lax.fori_loop(0, n_full, full, 0)
                bit = zb // 2
                while bit >= SUBLANES:
                    off = pl.multiple_of(part + (rem - rem % (2 * bit)), SUBLANES)

                    @pl.when((rem & bit) != 0)
                    def _(bit=bit, off=off):
                        fn(pltpu.make_async_copy(zeros.at[pl.ds(0, bit)],
                                                 xs_ref.at[pl.ds(off, bit)], sem))
                    bit //= 2

            pieces(lambda cp: cp.start())
            pieces(lambda cp: cp.wait())
            return 0

        lax.fori_loop(0, fs_ref.shape[0], region, 0)

    def issue(r, _):
        src = x_ref.at[pl.ds(r, 1)]
        pltpu.make_async_copy(src, xs_ref.at[pl.ds(idx[slot, r], 1)], sem).start()
        pltpu.make_async_copy(src, xs_ref.at[pl.ds(idx[slot, tm + r], 1)], sem).start()
        return 0

    def drain(r, _):
        pltpu.make_async_copy(x_ref.at[pl.ds(0, 1)], xs_ref.at[pl.ds(0, 1)], sem).wait()
        return 0

    lax.fori_loop(0, tm, issue, 0, unroll=8)
    lax.fori_loop(0, 2 * tm, drain, 0, unroll=16)


def _dispatch(x1, pos, fill_start, fill_len, n_rows):
    t, d = x1.shape
    tm = min(TM_ROUTE, t)
    assert t % tm == 0 and pos.shape == (t // tm, 2 * tm)
    pos2 = pos
    return pl.pallas_call(
        _dispatch_kernel,
        grid_spec=pltpu.PrefetchScalarGridSpec(
            num_scalar_prefetch=2,
            grid=(t // tm,),
            in_specs=[pl.BlockSpec(memory_space=pl.ANY),
                      pl.BlockSpec((tm, d), lambda i, fs, fl: (i, 0))],
            out_specs=pl.BlockSpec(memory_space=pl.ANY),
            scratch_shapes=[pltpu.SMEM((2, 2 * tm), jnp.int32), pltpu.VMEM((ZERO_ROWS, d), F32),
                            pltpu.SemaphoreType.DMA, pltpu.SemaphoreType.DMA((2,))]),
        out_shape=jax.ShapeDtypeStruct((n_rows, d), F32),
        compiler_params=pltpu.CompilerParams(dimension_semantics=("arbitrary",),
                                             vmem_limit_bytes=VMEM_LIMIT),
        name="dispatch",
    )(fill_start, fill_len, pos2, x1)


def _expert_kernel(be_ref, nb_ref, xs_ref, wu_ref, wd_ref, ys_ref, wu_bf, wd_bf):
    b = pl.program_id(0)
    live = b < nb_ref[0]

    @pl.when(live & ((b == 0) | (be_ref[b] != be_ref[jnp.maximum(b, 1) - 1])))
    def _():
        wu_bf[...] = wu_ref[0, 0].astype(BF16)
        wd_bf[...] = wd_ref[0, 0].astype(BF16)

    @pl.when(live)
    def _():
        xb = xs_ref[...].astype(BF16)
        hidden = jnp.dot(xb, wu_bf[...], preferred_element_type=F32)
        a = hidden[:, :EXPERT_FF]
        g = hidden[:, EXPERT_FF:]
        act = (g * _sigmoid(g) * a).astype(BF16)
        ys_ref[...] = jnp.dot(act, wd_bf[...], preferred_element_type=F32)

    @pl.when(jnp.logical_not(live))
    def _():
        ys_ref[...] = jnp.zeros_like(ys_ref)


def _experts(xs, w_up, w_down, layer, block_expert, n_used):
    n_rows, d = xs.shape
    n_blocks = n_rows // EXPERT_BLOCK

    def live(b, nb):
        return jnp.minimum(b, nb[0] - 1)

    wspec = lambda w: pl.BlockSpec((1, 1) + w.shape[2:],
                                   lambda b, be, nb: (layer, be[live(b, nb)], 0, 0))
    return pl.pallas_call(
        _expert_kernel,
        grid_spec=pltpu.PrefetchScalarGridSpec(
            num_scalar_prefetch=2,
            grid=(n_blocks,),
            in_specs=[pl.BlockSpec((EXPERT_BLOCK, d), lambda b, be, nb: (live(b, nb), 0)),
                      wspec(w_up), wspec(w_down)],
            out_specs=pl.BlockSpec((EXPERT_BLOCK, d), lambda b, be, nb: (b, 0)),
            scratch_shapes=[pltpu.VMEM(w_up.shape[2:], BF16), pltpu.VMEM(w_down.shape[2:], BF16)]),
        out_shape=jax.ShapeDtypeStruct((n_rows, d), F32),
        compiler_params=pltpu.CompilerParams(dimension_semantics=("arbitrary",),
                                             vmem_limit_bytes=VMEM_LIMIT),
        name="experts",
    )(block_expert, n_used, xs, w_up, w_down)


def _combine_kernel(pos_ref, x1_ref, route_ref, ys_ref, g2_ref, b2_ref, o_ref,
                    idx, rows, sem, isem):
    i = pl.program_id(0)
    tm = x1_ref.shape[0]
    slot = _index_prefetch(pos_ref, idx, isem, i, pl.num_programs(0))

    def issue(r, _):
        for k in range(2):
            src = ys_ref.at[pl.ds(idx[slot, k * tm + r], 1)]
            pltpu.make_async_copy(src, rows.at[k, pl.ds(r, 1)], sem).start()
        return 0

    def drain(r, _):
        pltpu.make_async_copy(ys_ref.at[pl.ds(0, 1)], rows.at[0, pl.ds(0, 1)], sem).wait()
        return 0

    lax.fori_loop(0, tm, issue, 0, unroll=8)
    lax.fori_loop(0, 2 * tm, drain, 0, unroll=16)
    route = route_ref[...].T
    m = route[:, 4:5] * rows[0] + route[:, 5:6] * rows[1]
    o_ref[...] = _layer_norm(DEEPNORM_ALPHA * x1_ref[...] + m, g2_ref[...], b2_ref[...])


def _combine(x1, route, pos, ys, g2, b2):
    t, d = x1.shape
    tm = min(TM_ROUTE, t)
    pos2 = pos
    const = lambda shape: pl.BlockSpec(shape, lambda i: (0,) * len(shape))
    return pl.pallas_call(
        _combine_kernel,
        grid=(t // tm,),
        in_specs=[pl.BlockSpec(memory_space=pl.ANY),
                  pl.BlockSpec((tm, d), lambda i: (i, 0)),
                  pl.BlockSpec((ROUTE_ROWS, tm), lambda i: (0, i)),
                  pl.BlockSpec(memory_space=pl.ANY),
                  const(g2.shape), const(b2.shape)],
        out_specs=pl.BlockSpec((tm, d), lambda i: (i, 0)),
        out_shape=jax.ShapeDtypeStruct((t, d), F32),
        scratch_shapes=[pltpu.SMEM((2, 2 * tm), jnp.int32), pltpu.VMEM((2, tm, d), F32),
                        pltpu.SemaphoreType.DMA, pltpu.SemaphoreType.DMA((2,))],
        compiler_params=pltpu.CompilerParams(dimension_semantics=("arbitrary",),
                                             vmem_limit_bytes=VMEM_LIMIT),
        name="combine",
    )(pos2, x1, route, ys, g2, b2)


def _routing_tables(route, counts, n_blocks):
    t = route.shape[1]
    tm = min(TM_ROUTE, t)
    cnt = counts[:, 0].astype(jnp.int32)
    padded = (cnt + EXPERT_BLOCK - 1) // EXPERT_BLOCK * EXPERT_BLOCK
    pad_end = jnp.cumsum(padded)
    pad_start = pad_end - padded
    experts = route[0:2].astype(jnp.int32)
    ranks = route[2:4].astype(jnp.int32)
    ids = jnp.arange(N_EXPERTS, dtype=jnp.int32)
    base = jnp.sum(jnp.where(experts[:, :, None] == ids, pad_start, 0), axis=-1)
    pos = (base + ranks).reshape(2, t // tm, tm).transpose(1, 0, 2).reshape(t // tm, 2 * tm)
    blk = jnp.arange(n_blocks, dtype=jnp.int32) * EXPERT_BLOCK
    block_expert = jnp.minimum(jnp.sum((pad_end[None, :] <= blk[:, None]).astype(jnp.int32), axis=1),
                               N_EXPERTS - 1)
    n_used = (pad_end[-1:] // EXPERT_BLOCK).astype(jnp.int32)
    n_rows = n_blocks * EXPERT_BLOCK
    fill_start = jnp.concatenate([pad_start + cnt, pad_end[-1:]]).astype(jnp.int32)
    fill_len = jnp.concatenate([padded - cnt, n_rows - pad_end[-1:]]).astype(jnp.int32)
    return pos, block_expert, n_used, fill_start, fill_len


def kernel(x, w_in, b_forget, conv_w, conv_b, conv_ln_g, conv_ln_b, sc_w, w_branch, w_gate, b_gate,
           w_out, ln1_g, ln1_b, w_up, w_down, ln2_g, ln2_b, w_router, b_router):
    batch, seq, d = x.shape
    t = batch * seq
    depth = w_in.shape[0]
    x2 = x.reshape(t, d)

    nf = ATT_HEADS
    w_main = jnp.concatenate([w_in[:, :, :2 * WIDTH], w_in[:, :, 3 * WIDTH + nf:]], axis=2).astype(BF16)
    w_vt = jnp.swapaxes(w_in[:, :, 2 * WIDTH:3 * WIDTH], 1, 2).astype(BF16)
    w_f = jnp.pad(w_in[:, :, 3 * WIDTH:3 * WIDTH + nf], ((0, 0), (0, 0), (0, LANES - nf))).astype(BF16)
    b_f = jnp.pad(b_forget, ((0, 0), (0, LANES - nf)))[:, None, :]
    wg = w_gate.astype(BF16)
    wb = w_branch.astype(BF16)
    wo = w_out.astype(BF16)
    wr_t = w_router.T
    wr_hi = wr_t.astype(BF16)
    wr_lo = (wr_t - wr_hi.astype(F32)).astype(BF16)
    wr = jnp.stack([wr_hi, wr_lo])
    br = b_router[:, None]

    n_blocks = (2 * t + N_EXPERTS * (EXPERT_BLOCK - 1) + EXPERT_BLOCK - 1) // EXPERT_BLOCK
    n_rows = n_blocks * EXPERT_BLOCK

    for l in range(depth):
        qf, kf, vt, u, br_sc = _in_proj(x2, w_main[l], w_vt[l], w_f[l], b_f[l], sc_w[l], seq)
        br_conv = _conv31(u, conv_w[l], conv_b[l][None, :], conv_ln_g[l][None, :],
                          conv_ln_b[l][None, :], seq)
        br_att = _fox_attention(qf, kf, vt, batch, seq)
        x1, route, counts = _merge(x2, br_att, br_conv, br_sc, wg[l], b_gate[l][:, None, :], wb[l],
                                   wo[l], ln1_g[l][None, :], ln1_b[l][None, :], wr, br)
        pos, block_expert, n_used, fill_start, fill_len = _routing_tables(route, counts, n_blocks)
        xs = _dispatch(x1, pos, fill_start, fill_len, n_rows)
        ys = _experts(xs, w_up, w_down, l, block_expert, n_used)
        x2 = _combine(x1, route, pos, ys, ln2_g[l][None, :], ln2_b[l][None, :])
    return x2.reshape(batch, seq, d)
```

```python
import functools

import numpy as np
import jax
import jax.numpy as jnp
from jax import lax
from jax.experimental import pallas as pl
from jax.experimental.pallas import tpu as pltpu
from jax.experimental.pallas import tpu_sc as plsc

F32 = jnp.float32
BF16 = jnp.bfloat16

ATT_HEADS = 8
ATT_HEAD_DIM = 64
WIDTH = 512
CONV_TAPS = 31
SC_TAPS = 3
N_EXPERTS = 32
EXPERTS_PER_GROUP = 4
EXPERT_FF = 512
DEPTH = 4
DEEPNORM_ALPHA = (2 * DEPTH) ** 0.25
LN_EPS = 1e-5

LANES = 128
SUBLANES = 8
VMEM_LIMIT = 56 * 1024 * 1024

TM_PROJ = 512
TM_CONV = 512
CONV_HALO = 32
CONV_CHUNK = 64
TQ = 512
TK = 512
ATT_TILES_PER_BLOCK = 4
EXPERT_BLOCK = 512
TM_ROUTE = 512
SC_WINDOW = 128
SC_ROW_SPLIT = 4
ROUTE_ROWS = 8
ZERO_ROWS = 256
NEG_BIG = -1e30
LOG2E = 1.4426950408889634


def _sigmoid(x):
    return 1.0 / (1.0 + jnp.exp(-x))


def _log_sigmoid(x):
    return jnp.minimum(x, 0.0) - jnp.log(1.0 + jnp.exp(-jnp.abs(x)))


def _split3(x):
    hi = x.astype(BF16)
    r1 = x - hi.astype(F32)
    mid = r1.astype(BF16)
    lo = (r1 - mid.astype(F32)).astype(BF16)
    return hi, mid, lo


def _layer_norm(y, g, b):
    mu = jnp.mean(y, axis=-1, keepdims=True)
    yc = y - mu
    var = jnp.mean(yc * yc, axis=-1, keepdims=True)
    return yc * lax.rsqrt(var + LN_EPS) * g + b


def _bias_placement():
    pairs = ATT_HEADS // 2
    place = np.zeros((3 * LANES, 2 * pairs * LANES), np.float32)
    ones = np.zeros((1, 2 * pairs * LANES), np.float32)
    for p in range(pairs):
        for h in range(2):
            for piece in range(3):
                src = piece * LANES + 2 * p + h
                place[src, p * LANES + 6 * h + 3 + piece] = 1.0
                place[src, (pairs + p) * LANES + 6 * h + piece] = -1.0
                ones[0, p * LANES + 6 * h + piece] = 1.0
                ones[0, (pairs + p) * LANES + 6 * h + 3 + piece] = 1.0
    return jnp.asarray(place, BF16), jnp.asarray(ones, F32)


def _in_proj_kernel(x_ref, wm_ref, wvt_ref, wf_ref, bf_ref, scw_ref, tri_ref, place_ref, ones_ref,
                    qf_ref, kf_ref, vt_ref, u_ref, brsc_ref,
                    carry_c, carry_p, *, tiles_per_seq):
    i = pl.program_id(0)

    @pl.when(i % tiles_per_seq == 0)
    def _():
        carry_c[...] = jnp.zeros_like(carry_c)
        carry_p[...] = jnp.zeros_like(carry_p)

    tm = x_ref.shape[0]
    xb = x_ref[...].astype(BF16)

    def mm(j):
        return jnp.dot(xb, wm_ref[:, j * WIDTH:(j + 1) * WIDTH], preferred_element_type=F32)

    q = (mm(0) * (ATT_HEAD_DIM ** -0.5 * LOG2E)).astype(BF16)
    k = mm(1).astype(BF16)
    vt_ref[...] = lax.dot_general(wvt_ref[...], xb, (((1,), (1,)), ((), ())),
                                  preferred_element_type=F32).astype(BF16)
    u_ref[...] = mm(2) * _sigmoid(mm(3))

    p = mm(5) * mm(6)
    row = lax.broadcasted_iota(jnp.int32, p.shape, 0)
    prev = carry_p[...]
    p1 = jnp.where(row == 0, prev[7:8, :], pltpu.roll(p, 1, axis=0))
    p2 = pltpu.roll(p, 2, axis=0)
    p2 = jnp.where(row == 0, prev[6:7, :], jnp.where(row == 1, prev[7:8, :], p2))
    y = scw_ref[0:1, :] * p2 + scw_ref[1:2, :] * p1 + scw_ref[2:3, :] * p
    brsc_ref[...] = (mm(4) * y).astype(BF16)
    carry_p[...] = p[tm - SUBLANES:tm, :]

    zf = jnp.dot(xb, wf_ref[...], preferred_element_type=F32)
    ls = _log_sigmoid(zf + bf_ref[...])
    hi, mid, lo = _split3(ls)
    tri = tri_ref[...]
    c = (jnp.dot(tri, hi, preferred_element_type=F32)
         + jnp.dot(tri, mid, preferred_element_type=F32)
         + jnp.dot(tri, lo, preferred_element_type=F32)) + carry_c[...]
    carry_c[...] = c[tm - 1:tm, :]

    bias = ones_ref[...]
    for n, piece in enumerate(_split3(c * LOG2E)):
        bias = bias + jnp.dot(piece, place_ref[n * LANES:(n + 1) * LANES, :],
                              preferred_element_type=F32)
    bias = bias.astype(BF16)
    pairs = ATT_HEADS // 2
    for p in range(pairs):
        lo_, hi_ = p * LANES, (p + 1) * LANES
        qf_ref[:, 2 * lo_:2 * lo_ + LANES] = q[:, lo_:hi_]
        qf_ref[:, 2 * lo_ + LANES:2 * hi_] = bias[:, lo_:hi_]
        kf_ref[:, 2 * lo_:2 * lo_ + LANES] = k[:, lo_:hi_]
        kf_ref[:, 2 * lo_ + LANES:2 * hi_] = bias[:, (pairs + p) * LANES:(pairs + p + 1) * LANES]


def _in_proj(x2, wm, wvt, wf, bf, scw, seq):
    t, d = x2.shape
    tm = min(TM_PROJ, seq)
    assert seq % tm == 0 and t % seq == 0
    tri = jnp.tril(jnp.ones((tm, tm), F32)).astype(BF16)
    place, ones = _bias_placement()
    const = lambda shape: pl.BlockSpec(shape, lambda i: (0,) * len(shape))
    rows = lambda w: pl.BlockSpec((tm, w), lambda i: (i, 0))
    return pl.pallas_call(
        functools.partial(_in_proj_kernel, tiles_per_seq=seq // tm),
        grid=(t // tm,),
        in_specs=[rows(d), const(wm.shape), const(wvt.shape), const(wf.shape), const(bf.shape),
                  const(scw.shape), const(tri.shape), const(place.shape), const(ones.shape)],
        out_specs=[rows(2 * WIDTH), rows(2 * WIDTH), pl.BlockSpec((WIDTH, tm), lambda i: (0, i)),
                   rows(WIDTH), rows(WIDTH)],
        out_shape=[jax.ShapeDtypeStruct((t, 2 * WIDTH), BF16),
                   jax.ShapeDtypeStruct((t, 2 * WIDTH), BF16),
                   jax.ShapeDtypeStruct((WIDTH, t), BF16),
                   jax.ShapeDtypeStruct((t, WIDTH), F32),
                   jax.ShapeDtypeStruct((t, WIDTH), BF16)],
        scratch_shapes=[pltpu.VMEM((1, LANES), F32), pltpu.VMEM((SUBLANES, WIDTH), F32)],
        compiler_params=pltpu.CompilerParams(dimension_semantics=("arbitrary",),
                                             vmem_limit_bytes=VMEM_LIMIT),
        name="in_proj",
    )(x2, wm, wvt, wf, bf, scw, tri, place, ones)


def _conv31_kernel(u_ref, w_ref, cb_ref, g_ref, b_ref, o_ref, ext, pre, *, tiles_per_seq):
    i = pl.program_id(0)
    tm = u_ref.shape[0]

    @pl.when(i % tiles_per_seq == 0)
    def _():
        ext[0:CONV_HALO, :] = jnp.zeros((CONV_HALO, WIDTH), F32)
        ext[CONV_HALO + tm:CONV_HALO + tm + SUBLANES, :] = jnp.zeros((SUBLANES, WIDTH), F32)

    ext[CONV_HALO:CONV_HALO + tm, :] = u_ref[...]
    base = CONV_HALO - (CONV_TAPS - 1)
    span = CONV_CHUNK + SUBLANES
    half = WIDTH // 2
    for c0 in range(0, tm, CONV_CHUNK):
        for l0 in (0, half):
            y = None
            for r in range(SUBLANES):
                z = None
                for a in range((base + CONV_TAPS - 1) // SUBLANES + 1):
                    k = SUBLANES * a + r - base
                    if 0 <= k < CONV_TAPS:
                        term = (w_ref[k:k + 1, l0:l0 + half]
                                * ext[c0 + SUBLANES * a:c0 + SUBLANES * a + span, l0:l0 + half])
                        z = term if z is None else z + term
                zr = z[r:r + CONV_CHUNK, :]
                y = zr if y is None else y + zr
            pre[c0:c0 + CONV_CHUNK, l0:l0 + half] = y
    for c0 in range(0, tm, CONV_CHUNK):
        y = _layer_norm(pre[c0:c0 + CONV_CHUNK, :] + cb_ref[...], g_ref[...], b_ref[...])
        o_ref[c0:c0 + CONV_CHUNK, :] = (y * _sigmoid(y)).astype(BF16)
    ext[0:CONV_HALO, :] = ext[tm:tm + CONV_HALO, :]


def _conv31(u, w, cb, g, b, seq):
    t = u.shape[0]
    tm = min(TM_CONV, seq)
    assert seq % tm == 0 and tm % CONV_CHUNK == 0
    const = lambda shape: pl.BlockSpec(shape, lambda i: (0,) * len(shape))
    return pl.pallas_call(
        functools.partial(_conv31_kernel, tiles_per_seq=seq // tm),
        grid=(t // tm,),
        in_specs=[pl.BlockSpec((tm, WIDTH), lambda i: (i, 0)), const(w.shape), const(cb.shape),
                  const(g.shape), const(b.shape)],
        out_specs=pl.BlockSpec((tm, WIDTH), lambda i: (i, 0)),
        out_shape=jax.ShapeDtypeStruct((t, WIDTH), BF16),
        scratch_shapes=[pltpu.VMEM((CONV_HALO + tm + SUBLANES, WIDTH), F32),
                        pltpu.VMEM((tm, WIDTH), F32)],
        compiler_params=pltpu.CompilerParams(dimension_semantics=("arbitrary",),
                                             vmem_limit_bytes=VMEM_LIMIT),
        name="conv31",
    )(u, w, cb, g, b)


def _fox_kernel(qf_ref, kf_ref, vt_ref, o_ref, acc0_ref, acc1_ref, *, tk):
    qi = pl.program_id(2)
    tq = qf_ref.shape[0]
    lane = lax.broadcasted_iota(jnp.int32, (tq, 2 * LANES), 1)
    bias_lane = lane - LANES
    qfull = qf_ref[...]
    zero = jnp.zeros_like(qfull)
    qs = [jnp.where(((lane >= h * ATT_HEAD_DIM) & (lane < (h + 1) * ATT_HEAD_DIM))
                    | ((bias_lane >= 6 * h) & (bias_lane < 6 * h + 6)), qfull, zero)
          for h in range(2)]
    q_lo = qi * tq
    accs = (acc0_ref, acc1_ref)
    for acc in accs:
        acc[...] = jnp.zeros(acc.shape, F32)

    def tiles(jobs, carry):
        work = []
        for j, masked in jobs:
            start = pl.multiple_of(j * tk, tk)
            kfull = kf_ref[pl.ds(start, tk), :]
            ss = [lax.dot_general(kfull, qs[h], (((1,), (1,)), ((), ())),
                                  preferred_element_type=F32) for h in range(2)]
            work.append((start, masked, ss))
        carry = list(carry)
        for start, masked, ss in work:
            vt = vt_ref[:, pl.ds(start, tk)]
            if masked:
                key = start + lax.broadcasted_iota(jnp.int32, ss[0].shape, 0)
                qry = q_lo + lax.broadcasted_iota(jnp.int32, ss[0].shape, 1)
                ss = [jnp.where(key <= qry, s, NEG_BIG) for s in ss]
            for h in range(2):
                m_old, l_old = carry[h]
                m_new = jnp.maximum(m_old, jnp.max(ss[h], axis=0, keepdims=True))
                a = jnp.exp2(m_old - m_new)
                p = jnp.exp2(ss[h] - m_new)
                l_new = a * l_old + jnp.sum(p.reshape(tk // SUBLANES, SUBLANES, tq), axis=0)
                accs[h][...] = a * accs[h][...] + jnp.dot(vt, p.astype(BF16),
                                                          preferred_element_type=F32)
                carry[h] = (m_new, l_new)
        return tuple(carry)

    init = tuple((jnp.full((1, tq), NEG_BIG, F32), jnp.zeros((SUBLANES, tq), F32)) for _ in range(2))
    n_full = q_lo // tk
    group = ATT_TILES_PER_BLOCK
    carry = lax.fori_loop(
        0, n_full // group,
        lambda j, c: tiles([(group * j + i, False) for i in range(group)], c), init)
    rest = n_full % group

    def tail(n):
        return lambda c: tiles([(n_full - n + i, False) for i in range(n)] + [(n_full, True)], c)

    (_, l0), (_, l1) = lax.switch(rest, [tail(n) for n in range(group)], carry)
    row = lax.broadcasted_iota(jnp.int32, (LANES, tq), 0)
    out_t = jnp.where(row < ATT_HEAD_DIM,
                      acc0_ref[...] / jnp.sum(l0, axis=0, keepdims=True),
                      acc1_ref[...] / jnp.sum(l1, axis=0, keepdims=True))
    o_ref[...] = out_t.T.astype(BF16)


def _fox_attention(qf, kf, vt, batch, seq):
    t = qf.shape[0]
    tq = min(TQ, seq)
    tk = min(TK, seq)
    assert seq % tq == 0 and seq % tk == 0 and tk % tq == 0
    nq = seq // tq
    pairs = ATT_HEADS // 2
    return pl.pallas_call(
        functools.partial(_fox_kernel, tk=tk),
        grid=(batch, pairs, nq),
        in_specs=[pl.BlockSpec((tq, 2 * LANES), lambda b, hp, qi: (b * nq + qi, hp)),
                  pl.BlockSpec((seq, 2 * LANES), lambda b, hp, qi: (b, hp)),
                  pl.BlockSpec((LANES, seq), lambda b, hp, qi: (hp, b))],
        out_specs=pl.BlockSpec((tq, LANES), lambda b, hp, qi: (b * nq + qi, hp)),
        out_shape=jax.ShapeDtypeStruct((t, WIDTH), BF16),
        scratch_shapes=[pltpu.VMEM((LANES, tq), F32), pltpu.VMEM((LANES, tq), F32)],
        compiler_params=pltpu.CompilerParams(
            dimension_semantics=("arbitrary", "arbitrary", "arbitrary"),
            vmem_limit_bytes=VMEM_LIMIT),
        name="fox_attn",
    )(qf, kf, vt)


def _partner(x, row, bit):
    n = x.shape[0]
    up = pltpu.roll(x, n - bit, axis=0)
    down = pltpu.roll(x, bit, axis=0)
    return jnp.where((row & bit) == 0, up, down)


def _merge_kernel(x_ref, att_ref, conv_ref, sc_ref, wg_ref, bg_ref, wb_ref, wo_ref,
                  g1_ref, b1_ref, wr_ref, br_ref, triu_ref,
                  x1_ref, route_ref, counts_ref, carry):
    i = pl.program_id(0)

    @pl.when(i == 0)
    def _():
        carry[...] = jnp.zeros_like(carry)

    x = x_ref[...]
    xb = x.astype(BF16)
    merged = None
    for n, br_ref_n in enumerate((att_ref, conv_ref, sc_ref)):
        gate = _sigmoid(jnp.dot(xb, wg_ref[n], preferred_element_type=F32) + bg_ref[n])
        term = gate * jnp.dot(br_ref_n[...], wb_ref[n], preferred_element_type=F32)
        merged = term if merged is None else merged + term
    h = jnp.dot(merged.astype(BF16), wo_ref[...], preferred_element_type=F32)
    x1 = _layer_norm(DEEPNORM_ALPHA * x + h, g1_ref[...], b1_ref[...])
    x1_ref[...] = x1

    x_hi = x1.astype(BF16)
    x_lo = (x1 - x_hi.astype(F32)).astype(BF16)
    nt = lambda w, v: lax.dot_general(w, v, (((1,), (1,)), ((), ())), preferred_element_type=F32)
    logits = nt(wr_ref[0], x_hi) + nt(wr_ref[0], x_lo) + nt(wr_ref[1], x_hi) + br_ref[...]
    mx = jnp.max(logits, axis=0, keepdims=True)
    ex = jnp.exp(logits - mx)
    p = ex / jnp.sum(ex, axis=0, keepdims=True)

    row = lax.broadcasted_iota(jnp.int32, p.shape, 0)
    sub = row & (EXPERTS_PER_GROUP - 1)
    rank = jnp.zeros(p.shape, jnp.int32)
    for d in range(1, EXPERTS_PER_GROUP):
        below = pltpu.roll(p, d, axis=0)
        above = pltpu.roll(p, N_EXPERTS - d, axis=0)
        rank = rank + jnp.where((sub >= d) & (below >= p), 1, 0)
        rank = rank + jnp.where((sub + d < EXPERTS_PER_GROUP) & (above > p), 1, 0)
    top2 = rank < 2
    score = jnp.where(top2, p, 0.0)
    score = score + _partner(score, row, 1)
    score = score + _partner(score, row, 2)
    best = jnp.max(score, axis=0, keepdims=True)
    group = row >> 2
    best_group = jnp.min(jnp.where(score == best, group, N_EXPERTS), axis=0, keepdims=True)
    sel = top2 & (group == best_group)
    gate = jnp.where(sel, p / best, 0.0)

    sel_b = jnp.where(sel, 1.0, 0.0).astype(BF16)
    before = jnp.dot(sel_b, triu_ref[...], preferred_element_type=F32) + carry[...]
    tm = x.shape[0]
    new_carry = before[:, tm - 1:tm] + sel_b[:, tm - 1:tm].astype(F32)
    carry[...] = new_carry
    counts_ref[...] = new_carry

    e_lo = jnp.min(jnp.where(sel, row, N_EXPERTS), axis=0, keepdims=True)
    e_hi = jnp.max(jnp.where(sel, row, -1), axis=0, keepdims=True)
    is_lo = row == e_lo
    is_hi = row == e_hi
    pick = lambda m, a: jnp.sum(jnp.where(m, a, 0.0), axis=0, keepdims=True)
    zero = jnp.zeros_like(mx)
    route_ref[...] = jnp.concatenate(
        [e_lo.astype(F32), e_hi.astype(F32), pick(is_lo, before), pick(is_hi, before),
         pick(is_lo, gate), pick(is_hi, gate), zero, zero], axis=0)


def _merge(x2, att, conv, sc, wg, bg, wb, wo, g1, b1, wr, br):
    t, d = x2.shape
    tm = min(TM_PROJ, t)
    assert t % tm == 0
    triu = jnp.triu(jnp.ones((tm, tm), F32), 1).astype(BF16)
    const = lambda shape: pl.BlockSpec(shape, lambda i: (0,) * len(shape),
                                       pipeline_mode=pl.Buffered(1))
    rows = lambda w: pl.BlockSpec((tm, w), lambda i: (i, 0))
    return pl.pallas_call(
        _merge_kernel,
        grid=(t // tm,),
        in_specs=[rows(d), rows(WIDTH), rows(WIDTH), rows(WIDTH),
                  const(wg.shape), const(bg.shape), const(wb.shape), const(wo.shape),
                  const(g1.shape), const(b1.shape), const(wr.shape), const(br.shape),
                  const(triu.shape)],
        out_specs=[rows(d), pl.BlockSpec((ROUTE_ROWS, tm), lambda i: (0, i)),
                   pl.BlockSpec((N_EXPERTS, 1), lambda i: (0, 0))],
        out_shape=[jax.ShapeDtypeStruct((t, d), F32),
                   jax.ShapeDtypeStruct((ROUTE_ROWS, t), F32),
                   jax.ShapeDtypeStruct((N_EXPERTS, 1), F32)],
        scratch_shapes=[pltpu.VMEM((N_EXPERTS, 1), F32)],
        compiler_params=pltpu.CompilerParams(dimension_semantics=("arbitrary",),
                                             vmem_limit_bytes=VMEM_LIMIT),
        name="merge",
    )(x2, att, conv, sc, wg, bg, wb, wo, g1, b1, wr, br, triu)


def _sc_mesh():
    return plsc.VectorSubcoreMesh(core_axis_name="core", subcore_axis_name="subcore")


def _split_rows(a, idx):
    rows, d = a.shape
    sub = (idx[:, None] * SC_ROW_SPLIT + jnp.arange(SC_ROW_SPLIT, dtype=idx.dtype)).reshape(-1)
    return a.reshape(rows * SC_ROW_SPLIT, d // SC_ROW_SPLIT), sub


def _sc_scatter_rows(x, idx, n_rows):
    d_full = x.shape[1]
    x, idx = _split_rows(x, idx)
    n_rows = n_rows * SC_ROW_SPLIT
    t, d = x.shape
    m = idx.shape[0]
    w = SC_WINDOW
    assert t % w == 0 and m % t == 0
    steps_per_pass = t // w

    @functools.partial(pl.kernel, out_type=jax.ShapeDtypeStruct((n_rows, d), x.dtype),
                       mesh=_sc_mesh(), scratch_types=[], name="sc_dispatch")
    def run(x_hbm, i_hbm, o_hbm):
        def body(x_vmem, i_vmem):
            pltpu.sync_copy(x_vmem, o_hbm.at[i_vmem.at[0]])

        pltpu.emit_pipeline(
            body,
            grid=(m // w,),
            in_specs=[pl.BlockSpec((w, d), index_map=lambda i: (i % steps_per_pass, 0)),
                      pl.BlockSpec((1, w), index_map=lambda i: (0, i))],
            out_specs=[],
            core_axis_name=("core", "subcore"),
            dimension_semantics=(pltpu.PARALLEL,),
        )(x_hbm, i_hbm)

    return run(x, idx.reshape(1, m)).reshape(n_rows // SC_ROW_SPLIT, d_full)


def _sc_gather_rows(table, idx):
    d_full = table.shape[1]
    table, idx = _split_rows(table, idx)
    d = table.shape[1]
    m = idx.shape[0]
    w = SC_WINDOW
    assert m % w == 0

    @functools.partial(pl.kernel, out_type=jax.ShapeDtypeStruct((m, d), table.dtype),
                       mesh=_sc_mesh(), scratch_types=[], name="sc_gather")
    def run(t_hbm, i_hbm, o_hbm):
        def body(i_vmem, o_vmem):
            pltpu.sync_copy(t_hbm.at[i_vmem.at[0]], o_vmem)

        pltpu.emit_pipeline(
            body,
            grid=(m // w,),
            in_specs=[pl.BlockSpec((1, w), index_map=lambda i: (0, i))],
            out_specs=[pl.BlockSpec((w, d), index_map=lambda i: (i, 0))],
            core_axis_name=("core", "subcore"),
            dimension_semantics=(pltpu.PARALLEL,),
        )(i_hbm, o_hbm)

    return run(table, idx.reshape(1, m)).reshape(m // SC_ROW_SPLIT, d_full)


def _fill_kernel(fs_ref, fl_ref, xs_in, xs_ref, zeros, sem):
    del xs_in
    zb = zeros.shape[0]
    zeros[...] = jnp.zeros_like(zeros)

    def region(e, _):
        start = fs_ref[e]
        length = fl_ref[e]
        head = jnp.minimum(length, (SUBLANES - start % SUBLANES) % SUBLANES)
        mid = start + head
        mid_len = (length - head) // SUBLANES * SUBLANES
        n_full = mid_len // zb
        rem = mid_len - n_full * zb
        part = mid + n_full * zb
        tail = mid + mid_len
        n_tail = length - head - mid_len

        def pieces(fn):
            def single(base):
                def body(r, _):
                    fn(pltpu.make_async_copy(zeros.at[pl.ds(0, 1)],
                                             xs_ref.at[pl.ds(base + r, 1)], sem))
                    return 0
                return body
            lax.fori_loop(0, head, single(start), 0)
            lax.fori_loop(0, n_tail, single(tail), 0)

            def full(c, _):
                off = pl.multiple_of(mid + c * zb, SUBLANES)
                fn(pltpu.make_async_copy(zeros, xs_ref.at[pl.ds(off, zb)], sem))
                return 0
            lax.fori_loop(0, n_full, full, 0)
            bit = zb // 2
            while bit >= SUBLANES:
                off = pl.multiple_of(part + (rem - rem % (2 * bit)), SUBLANES)

                @pl.when((rem & bit) != 0)
                def _(bit=bit, off=off):
                    fn(pltpu.make_async_copy(zeros.at[pl.ds(0, bit)],
                                             xs_ref.at[pl.ds(off, bit)], sem))
                bit //= 2

        pieces(lambda cp: cp.start())
        pieces(lambda cp: cp.wait())
        return 0

    lax.fori_loop(0, fs_ref.shape[0], region, 0)


def _fill_unrouted(xs, fill_start, fill_len):
    n_rows, d = xs.shape
    return pl.pallas_call(
        _fill_kernel,
        grid_spec=pltpu.PrefetchScalarGridSpec(
            num_scalar_prefetch=2,
            grid=(1,),
            in_specs=[pl.BlockSpec(memory_space=pl.ANY)],
            out_specs=pl.BlockSpec(memory_space=pl.ANY),
            scratch_shapes=[pltpu.VMEM((ZERO_ROWS, d), xs.dtype), pltpu.SemaphoreType.DMA]),
        out_shape=jax.ShapeDtypeStruct((n_rows, d), xs.dtype),
        input_output_aliases={2: 0},
        compiler_params=pltpu.CompilerParams(dimension_semantics=("arbitrary",),
                                             vmem_limit_bytes=VMEM_LIMIT),
        name="fill_unrouted",
    )(fill_start, fill_len, xs)


def _expert_kernel(be_ref, nb_ref, xs_ref, wu_ref, wd_ref, ys_ref, wu_bf, wd_bf):
    b = pl.program_id(0)
    live = b < nb_ref[0]

    @pl.when(live & ((b == 0) | (be_ref[b] != be_ref[jnp.maximum(b, 1) - 1])))
    def _():
        wu_bf[...] = wu_ref[0, 0].astype(BF16)
        wd_bf[...] = wd_ref[0, 0].astype(BF16)

    @pl.when(live)
    def _():
        xb = xs_ref[...].astype(BF16)
        hidden = jnp.dot(xb, wu_bf[...], preferred_element_type=F32)
        a = hidden[:, :EXPERT_FF]
        g = hidden[:, EXPERT_FF:]
        act = (g * _sigmoid(g) * a).astype(BF16)
        ys_ref[...] = jnp.dot(act, wd_bf[...], preferred_element_type=F32)

    @pl.when(jnp.logical_not(live))
    def _():
        ys_ref[...] = jnp.zeros_like(ys_ref)


def _experts(xs, w_up, w_down, layer, block_expert, n_used):
    n_rows, d = xs.shape
    n_blocks = n_rows // EXPERT_BLOCK

    def live(b, nb):
        return jnp.minimum(b, nb[0] - 1)

    wspec = lambda w: pl.BlockSpec((1, 1) + w.shape[2:],
                                   lambda b, be, nb: (layer, be[live(b, nb)], 0, 0))
    return pl.pallas_call(
        _expert_kernel,
        grid_spec=pltpu.PrefetchScalarGridSpec(
            num_scalar_prefetch=2,
            grid=(n_blocks,),
            in_specs=[pl.BlockSpec((EXPERT_BLOCK, d), lambda b, be, nb: (live(b, nb), 0)),
                      wspec(w_up), wspec(w_down)],
            out_specs=pl.BlockSpec((EXPERT_BLOCK, d), lambda b, be, nb: (b, 0)),
            scratch_shapes=[pltpu.VMEM(w_up.shape[2:], BF16), pltpu.VMEM(w_down.shape[2:], BF16)]),
        out_shape=jax.ShapeDtypeStruct((n_rows, d), F32),
        compiler_params=pltpu.CompilerParams(dimension_semantics=("arbitrary",),
                                             vmem_limit_bytes=VMEM_LIMIT),
        name="experts",
    )(block_expert, n_used, xs, w_up, w_down)


def _combine_kernel(x1_ref, route_ref, y0_ref, y1_ref, g2_ref, b2_ref, o_ref):
    route = route_ref[...].T
    m = route[:, 4:5] * y0_ref[...] + route[:, 5:6] * y1_ref[...]
    o_ref[...] = _layer_norm(DEEPNORM_ALPHA * x1_ref[...] + m, g2_ref[...], b2_ref[...])


def _combine(x1, route, yk, g2, b2):
    t, d = x1.shape
    tm = min(TM_ROUTE, t)
    steps = t // tm
    const = lambda shape: pl.BlockSpec(shape, lambda i: (0,) * len(shape))
    return pl.pallas_call(
        _combine_kernel,
        grid=(steps,),
        in_specs=[pl.BlockSpec((tm, d), lambda i: (i, 0)),
                  pl.BlockSpec((ROUTE_ROWS, tm), lambda i: (0, i)),
                  pl.BlockSpec((tm, d), lambda i: (i, 0)),
                  pl.BlockSpec((tm, d), lambda i: (steps + i, 0)),
                  const(g2.shape), const(b2.shape)],
        out_specs=pl.BlockSpec((tm, d), lambda i: (i, 0)),
        out_shape=jax.ShapeDtypeStruct((t, d), F32),
        compiler_params=pltpu.CompilerParams(dimension_semantics=("arbitrary",),
                                             vmem_limit_bytes=VMEM_LIMIT),
        name="combine",
    )(x1, route, yk, yk, g2, b2)


def _routing_tables(route, counts, n_blocks):
    t = route.shape[1]
    cnt = counts[:, 0].astype(jnp.int32)
    padded = (cnt + EXPERT_BLOCK - 1) // EXPERT_BLOCK * EXPERT_BLOCK
    pad_end = jnp.cumsum(padded)
    pad_start = pad_end - padded
    experts = route[0:2].astype(jnp.int32)
    ranks = route[2:4].astype(jnp.int32)
    ids = jnp.arange(N_EXPERTS, dtype=jnp.int32)
    base = jnp.sum(jnp.where(experts[:, :, None] == ids, pad_start, 0), axis=-1)
    pos = (base + ranks).reshape(2 * t)
    blk = jnp.arange(n_blocks, dtype=jnp.int32) * EXPERT_BLOCK
    block_expert = jnp.minimum(jnp.sum((pad_end[None, :] <= blk[:, None]).astype(jnp.int32), axis=1),
                               N_EXPERTS - 1)
    n_used = (pad_end[-1:] // EXPERT_BLOCK).astype(jnp.int32)
    n_rows = n_blocks * EXPERT_BLOCK
    fill_start = jnp.concatenate([pad_start + cnt, pad_end[-1:]]).astype(jnp.int32)
    fill_len = jnp.concatenate([padded - cnt, n_rows - pad_end[-1:]]).astype(jnp.int32)
    return pos, block_expert, n_used, fill_start, fill_len


def kernel(x, w_in, b_forget, conv_w, conv_b, conv_ln_g, conv_ln_b, sc_w, w_branch, w_gate, b_gate,
           w_out, ln1_g, ln1_b, w_up, w_down, ln2_g, ln2_b, w_router, b_router):
    batch, seq, d = x.shape
    t = batch * seq
    depth = w_in.shape[0]
    x2 = x.reshape(t, d)

    nf = ATT_HEADS
    w_main = jnp.concatenate([w_in[:, :, :2 * WIDTH], w_in[:, :, 3 * WIDTH + nf:]], axis=2).astype(BF16)
    w_vt = jnp.swapaxes(w_in[:, :, 2 * WIDTH:3 * WIDTH], 1, 2).astype(BF16)
    w_f = jnp.pad(w_in[:, :, 3 * WIDTH:3 * WIDTH + nf], ((0, 0), (0, 0), (0, LANES - nf))).astype(BF16)
    b_f = jnp.pad(b_forget, ((0, 0), (0, LANES - nf)))[:, None, :]
    wg = w_gate.astype(BF16)
    wb = w_branch.astype(BF16)
    wo = w_out.astype(BF16)
    wr_t = w_router.T
    wr_hi = wr_t.astype(BF16)
    wr_lo = (wr_t - wr_hi.astype(F32)).astype(BF16)
    wr = jnp.stack([wr_hi, wr_lo])
    br = b_router[:, None]

    n_blocks = (2 * t + N_EXPERTS * (EXPERT_BLOCK - 1) + EXPERT_BLOCK - 1) // EXPERT_BLOCK
    n_rows = n_blocks * EXPERT_BLOCK

    for l in range(depth):
        qf, kf, vt, u, br_sc = _in_proj(x2, w_main[l], w_vt[l], w_f[l], b_f[l], sc_w[l], seq)
        br_conv = _conv31(u, conv_w[l], conv_b[l][None, :], conv_ln_g[l][None, :],
                          conv_ln_b[l][None, :], seq)
        br_att = _fox_attention(qf, kf, vt, batch, seq)
        x1, route, counts = _merge(x2, br_att, br_conv, br_sc, wg[l], b_gate[l][:, None, :], wb[l],
                                   wo[l], ln1_g[l][None, :], ln1_b[l][None, :], wr, br)
        pos, block_expert, n_used, fill_start, fill_len = _routing_tables(route, counts, n_blocks)
        xs = _fill_unrouted(_sc_scatter_rows(x1, pos, n_rows), fill_start, fill_len)
        ys = _experts(xs, w_up, w_down, l, block_expert, n_used)
        x2 = _combine(x1, route, _sc_gather_rows(ys, pos), ln2_g[l][None, :], ln2_b[l][None, :])
    return x2.reshape(batch, seq, d)
```

```python
import functools

import numpy as np
import jax
import jax.numpy as jnp
from jax import lax
from jax.experimental import pallas as pl
from jax.experimental.pallas import tpu as pltpu
from jax.experimental.pallas import tpu_sc as plsc

F32 = jnp.float32
BF16 = jnp.bfloat16

ATT_HEADS = 8
ATT_HEAD_DIM = 64
WIDTH = 512
CONV_TAPS = 31
SC_TAPS = 3
N_EXPERTS = 32
EXPERTS_PER_GROUP = 4
EXPERT_FF = 512
DEPTH = 4
DEEPNORM_ALPHA = (2 * DEPTH) ** 0.25
LN_EPS = 1e-5

LANES = 128
SUBLANES = 8
VMEM_LIMIT = 56 * 1024 * 1024

TM_PROJ = 512
TM_CONV = 512
CONV_HALO = 32
CONV_CHUNK = 64
TQ = 512
TK = 512
ATT_TILES_PER_BLOCK = 4
EXPERT_BLOCK = 512
TM_ROUTE = 512
SC_WINDOW = 128
ROW_CHUNKS = 4
CHUNK = 256
ROUTE_ROWS = 8
ZERO_ROWS = 256
NEG_BIG = -1e30
LOG2E = 1.4426950408889634


def _sigmoid(x):
    return 1.0 / (1.0 + jnp.exp(-x))


def _log_sigmoid(x):
    return jnp.minimum(x, 0.0) - jnp.log(1.0 + jnp.exp(-jnp.abs(x)))


def _split3(x):
    hi = x.astype(BF16)
    r1 = x - hi.astype(F32)
    mid = r1.astype(BF16)
    lo = (r1 - mid.astype(F32)).astype(BF16)
    return hi, mid, lo


def _layer_norm(y, g, b):
    mu = jnp.mean(y, axis=-1, keepdims=True)
    yc = y - mu
    var = jnp.mean(yc * yc, axis=-1, keepdims=True)
    return yc * lax.rsqrt(var + LN_EPS) * g + b


def _bias_placement():
    pairs = ATT_HEADS // 2
    place = np.zeros((3 * LANES, 2 * pairs * LANES), np.float32)
    ones = np.zeros((1, 2 * pairs * LANES), np.float32)
    for p in range(pairs):
        for h in range(2):
            for piece in range(3):
                src = piece * LANES + 2 * p + h
                place[src, p * LANES + 6 * h + 3 + piece] = 1.0
                place[src, (pairs + p) * LANES + 6 * h + piece] = -1.0
                ones[0, p * LANES + 6 * h + piece] = 1.0
                ones[0, (pairs + p) * LANES + 6 * h + 3 + piece] = 1.0
    return jnp.asarray(place, BF16), jnp.asarray(ones, F32)


def _in_proj_kernel(x_ref, wm_ref, wvt_ref, wf_ref, bf_ref, scw_ref, tri_ref, place_ref, ones_ref,
                    qf_ref, kf_ref, vt_ref, u_ref, brsc_ref,
                    carry_c, carry_p, *, tiles_per_seq):
    i = pl.program_id(0)

    @pl.when(i % tiles_per_seq == 0)
    def _():
        carry_c[...] = jnp.zeros_like(carry_c)
        carry_p[...] = jnp.zeros_like(carry_p)

    tm = x_ref.shape[0]
    xb = x_ref[...].astype(BF16)

    def mm(j):
        return jnp.dot(xb, wm_ref[:, j * WIDTH:(j + 1) * WIDTH], preferred_element_type=F32)

    q = (mm(0) * (ATT_HEAD_DIM ** -0.5 * LOG2E)).astype(BF16)
    k = mm(1).astype(BF16)
    vt_ref[...] = lax.dot_general(wvt_ref[...], xb, (((1,), (1,)), ((), ())),
                                  preferred_element_type=F32).astype(BF16)
    u_ref[...] = mm(2) * _sigmoid(mm(3))

    p = mm(5) * mm(6)
    row = lax.broadcasted_iota(jnp.int32, p.shape, 0)
    prev = carry_p[...]
    p1 = jnp.where(row == 0, prev[7:8, :], pltpu.roll(p, 1, axis=0))
    p2 = pltpu.roll(p, 2, axis=0)
    p2 = jnp.where(row == 0, prev[6:7, :], jnp.where(row == 1, prev[7:8, :], p2))
    y = scw_ref[0:1, :] * p2 + scw_ref[1:2, :] * p1 + scw_ref[2:3, :] * p
    brsc_ref[...] = (mm(4) * y).astype(BF16)
    carry_p[...] = p[tm - SUBLANES:tm, :]

    zf = jnp.dot(xb, wf_ref[...], preferred_element_type=F32)
    ls = _log_sigmoid(zf + bf_ref[...])
    hi, mid, lo = _split3(ls)
    tri = tri_ref[...]
    c = (jnp.dot(tri, hi, preferred_element_type=F32)
         + jnp.dot(tri, mid, preferred_element_type=F32)
         + jnp.dot(tri, lo, preferred_element_type=F32)) + carry_c[...]
    carry_c[...] = c[tm - 1:tm, :]

    bias = ones_ref[...]
    for n, piece in enumerate(_split3(c * LOG2E)):
        bias = bias + jnp.dot(piece, place_ref[n * LANES:(n + 1) * LANES, :],
                              preferred_element_type=F32)
    bias = bias.astype(BF16)
    pairs = ATT_HEADS // 2
    for p in range(pairs):
        lo_, hi_ = p * LANES, (p + 1) * LANES
        qf_ref[:, 2 * lo_:2 * lo_ + LANES] = q[:, lo_:hi_]
        qf_ref[:, 2 * lo_ + LANES:2 * hi_] = bias[:, lo_:hi_]
        kf_ref[:, 2 * lo_:2 * lo_ + LANES] = k[:, lo_:hi_]
        kf_ref[:, 2 * lo_ + LANES:2 * hi_] = bias[:, (pairs + p) * LANES:(pairs + p + 1) * LANES]


def _in_proj(x2, wm, wvt, wf, bf, scw, seq):
    t, d = x2.shape
    tm = min(TM_PROJ, seq)
    assert seq % tm == 0 and t % seq == 0
    tri = jnp.tril(jnp.ones((tm, tm), F32)).astype(BF16)
    place, ones = _bias_placement()
    const = lambda shape: pl.BlockSpec(shape, lambda i: (0,) * len(shape))
    rows = lambda w: pl.BlockSpec((tm, w), lambda i: (i, 0))
    return pl.pallas_call(
        functools.partial(_in_proj_kernel, tiles_per_seq=seq // tm),
        grid=(t // tm,),
        in_specs=[rows(d), const(wm.shape), const(wvt.shape), const(wf.shape), const(bf.shape),
                  const(scw.shape), const(tri.shape), const(place.shape), const(ones.shape)],
        out_specs=[rows(2 * WIDTH), rows(2 * WIDTH), pl.BlockSpec((WIDTH, tm), lambda i: (0, i)),
                   rows(WIDTH), rows(WIDTH)],
        out_shape=[jax.ShapeDtypeStruct((t, 2 * WIDTH), BF16),
                   jax.ShapeDtypeStruct((t, 2 * WIDTH), BF16),
                   jax.ShapeDtypeStruct((WIDTH, t), BF16),
                   jax.ShapeDtypeStruct((t, WIDTH), F32),
                   jax.ShapeDtypeStruct((t, WIDTH), BF16)],
        scratch_shapes=[pltpu.VMEM((1, LANES), F32), pltpu.VMEM((SUBLANES, WIDTH), F32)],
        compiler_params=pltpu.CompilerParams(dimension_semantics=("arbitrary",),
                                             vmem_limit_bytes=VMEM_LIMIT),
        name="in_proj",
    )(x2, wm, wvt, wf, bf, scw, tri, place, ones)


def _conv31_kernel(u_ref, w_ref, cb_ref, g_ref, b_ref, o_ref, ext, pre, *, tiles_per_seq):
    i = pl.program_id(0)
    tm = u_ref.shape[0]

    @pl.when(i % tiles_per_seq == 0)
    def _():
        ext[0:CONV_HALO, :] = jnp.zeros((CONV_HALO, WIDTH), F32)
        ext[CONV_HALO + tm:CONV_HALO + tm + SUBLANES, :] = jnp.zeros((SUBLANES, WIDTH), F32)

    ext[CONV_HALO:CONV_HALO + tm, :] = u_ref[...]
    base = CONV_HALO - (CONV_TAPS - 1)
    span = CONV_CHUNK + SUBLANES
    half = WIDTH // 2
    for c0 in range(0, tm, CONV_CHUNK):
        for l0 in (0, half):
            y = None
            for r in range(SUBLANES):
                z = None
                for a in range((base + CONV_TAPS - 1) // SUBLANES + 1):
                    k = SUBLANES * a + r - base
                    if 0 <= k < CONV_TAPS:
                        term = (w_ref[k:k + 1, l0:l0 + half]
                                * ext[c0 + SUBLANES * a:c0 + SUBLANES * a + span, l0:l0 + half])
                        z = term if z is None else z + term
                zr = z[r:r + CONV_CHUNK, :]
                y = zr if y is None else y + zr
            pre[c0:c0 + CONV_CHUNK, l0:l0 + half] = y
    for c0 in range(0, tm, CONV_CHUNK):
        y = _layer_norm(pre[c0:c0 + CONV_CHUNK, :] + cb_ref[...], g_ref[...], b_ref[...])
        o_ref[c0:c0 + CONV_CHUNK, :] = (y * _sigmoid(y)).astype(BF16)
    ext[0:CONV_HALO, :] = ext[tm:tm + CONV_HALO, :]


def _conv31(u, w, cb, g, b, seq):
    t = u.shape[0]
    tm = min(TM_CONV, seq)
    assert seq % tm == 0 and tm % CONV_CHUNK == 0
    const = lambda shape: pl.BlockSpec(shape, lambda i: (0,) * len(shape))
    return pl.pallas_call(
        functools.partial(_conv31_kernel, tiles_per_seq=seq // tm),
        grid=(t // tm,),
        in_specs=[pl.BlockSpec((tm, WIDTH), lambda i: (i, 0)), const(w.shape), const(cb.shape),
                  const(g.shape), const(b.shape)],
        out_specs=pl.BlockSpec((tm, WIDTH), lambda i: (i, 0)),
        out_shape=jax.ShapeDtypeStruct((t, WIDTH), BF16),
        scratch_shapes=[pltpu.VMEM((CONV_HALO + tm + SUBLANES, WIDTH), F32),
                        pltpu.VMEM((tm, WIDTH), F32)],
        compiler_params=pltpu.CompilerParams(dimension_semantics=("arbitrary",),
                                             vmem_limit_bytes=VMEM_LIMIT),
        name="conv31",
    )(u, w, cb, g, b)


def _fox_kernel(qf_ref, kf_ref, vt_ref, o_ref, acc0_ref, acc1_ref, *, tk):
    qi = pl.program_id(2)
    tq = qf_ref.shape[0]
    lane = lax.broadcasted_iota(jnp.int32, (tq, 2 * LANES), 1)
    bias_lane = lane - LANES
    qfull = qf_ref[...]
    zero = jnp.zeros_like(qfull)
    qs = [jnp.where(((lane >= h * ATT_HEAD_DIM) & (lane < (h + 1) * ATT_HEAD_DIM))
                    | ((bias_lane >= 6 * h) & (bias_lane < 6 * h + 6)), qfull, zero)
          for h in range(2)]
    q_lo = qi * tq
    accs = (acc0_ref, acc1_ref)
    for acc in accs:
        acc[...] = jnp.zeros(acc.shape, F32)

    def tiles(jobs, carry):
        work = []
        for j, masked in jobs:
            start = pl.multiple_of(j * tk, tk)
            kfull = kf_ref[pl.ds(start, tk), :]
            ss = [lax.dot_general(kfull, qs[h], (((1,), (1,)), ((), ())),
                                  preferred_element_type=F32) for h in range(2)]
            work.append((start, masked, ss))
        carry = list(carry)
        for start, masked, ss in work:
            vt = vt_ref[:, pl.ds(start, tk)]
            if masked:
                key = start + lax.broadcasted_iota(jnp.int32, ss[0].shape, 0)
                qry = q_lo + lax.broadcasted_iota(jnp.int32, ss[0].shape, 1)
                ss = [jnp.where(key <= qry, s, NEG_BIG) for s in ss]
            for h in range(2):
                m_old, l_old = carry[h]
                m_new = jnp.maximum(m_old, jnp.max(ss[h], axis=0, keepdims=True))
                a = jnp.exp2(m_old - m_new)
                p = jnp.exp2(ss[h] - m_new)
                l_new = a * l_old + jnp.sum(p.reshape(tk // SUBLANES, SUBLANES, tq), axis=0)
                accs[h][...] = a * accs[h][...] + jnp.dot(vt, p.astype(BF16),
                                                          preferred_element_type=F32)
                carry[h] = (m_new, l_new)
        return tuple(carry)

    init = tuple((jnp.full((1, tq), NEG_BIG, F32), jnp.zeros((SUBLANES, tq), F32)) for _ in range(2))
    n_full = q_lo // tk
    group = ATT_TILES_PER_BLOCK
    carry = lax.fori_loop(
        0, n_full // group,
        lambda j, c: tiles([(group * j + i, False) for i in range(group)], c), init)
    rest = n_full % group

    def tail(n):
        return lambda c: tiles([(n_full - n + i, False) for i in range(n)] + [(n_full, True)], c)

    (_, l0), (_, l1) = lax.switch(rest, [tail(n) for n in range(group)], carry)
    row = lax.broadcasted_iota(jnp.int32, (LANES, tq), 0)
    out_t = jnp.where(row < ATT_HEAD_DIM,
                      acc0_ref[...] / jnp.sum(l0, axis=0, keepdims=True),
                      acc1_ref[...] / jnp.sum(l1, axis=0, keepdims=True))
    o_ref[...] = out_t.T.astype(BF16)


def _fox_attention(qf, kf, vt, batch, seq):
    t = qf.shape[0]
    tq = min(TQ, seq)
    tk = min(TK, seq)
    assert seq % tq == 0 and seq % tk == 0 and tk % tq == 0
    nq = seq // tq
    pairs = ATT_HEADS // 2
    return pl.pallas_call(
        functools.partial(_fox_kernel, tk=tk),
        grid=(batch, pairs, nq),
        in_specs=[pl.BlockSpec((tq, 2 * LANES), lambda b, hp, qi: (b * nq + qi, hp)),
                  pl.BlockSpec((seq, 2 * LANES), lambda b, hp, qi: (b, hp)),
                  pl.BlockSpec((LANES, seq), lambda b, hp, qi: (hp, b))],
        out_specs=pl.BlockSpec((tq, LANES), lambda b, hp, qi: (b * nq + qi, hp)),
        out_shape=jax.ShapeDtypeStruct((t, WIDTH), BF16),
        scratch_shapes=[pltpu.VMEM((LANES, tq), F32), pltpu.VMEM((LANES, tq), F32)],
        compiler_params=pltpu.CompilerParams(
            dimension_semantics=("arbitrary", "arbitrary", "arbitrary"),
            vmem_limit_bytes=VMEM_LIMIT),
        name="fox_attn",
    )(qf, kf, vt)


def _partner(x, row, bit):
    n = x.shape[0]
    up = pltpu.roll(x, n - bit, axis=0)
    down = pltpu.roll(x, bit, axis=0)
    return jnp.where((row & bit) == 0, up, down)


def _merge_kernel(x_ref, att_ref, conv_ref, sc_ref, wg_ref, bg_ref, wb_ref, wo_ref,
                  g1_ref, b1_ref, wr_ref, br_ref, triu_ref,
                  x1_ref, route_ref, counts_ref, carry):
    i = pl.program_id(0)

    @pl.when(i == 0)
    def _():
        carry[...] = jnp.zeros_like(carry)

    x = x_ref[...]
    xb = x.astype(BF16)
    merged = None
    for n, br_ref_n in enumerate((att_ref, conv_ref, sc_ref)):
        gate = _sigmoid(jnp.dot(xb, wg_ref[n], preferred_element_type=F32) + bg_ref[n])
        term = gate * jnp.dot(br_ref_n[...], wb_ref[n], preferred_element_type=F32)
        merged = term if merged is None else merged + term
    h = jnp.dot(merged.astype(BF16), wo_ref[...], preferred_element_type=F32)
    x1 = _layer_norm(DEEPNORM_ALPHA * x + h, g1_ref[...], b1_ref[...])
    for c in range(ROW_CHUNKS):
        x1_ref[c] = x1[:, c * CHUNK:(c + 1) * CHUNK]

    x_hi = x1.astype(BF16)
    x_lo = (x1 - x_hi.astype(F32)).astype(BF16)
    nt = lambda w, v: lax.dot_general(w, v, (((1,), (1,)), ((), ())), preferred_element_type=F32)
    logits = nt(wr_ref[0], x_hi) + nt(wr_ref[0], x_lo) + nt(wr_ref[1], x_hi) + br_ref[...]
    mx = jnp.max(logits, axis=0, keepdims=True)
    ex = jnp.exp(logits - mx)
    p = ex / jnp.sum(ex, axis=0, keepdims=True)

    row = lax.broadcasted_iota(jnp.int32, p.shape, 0)
    sub = row & (EXPERTS_PER_GROUP - 1)
    rank = jnp.zeros(p.shape, jnp.int32)
    for d in range(1, EXPERTS_PER_GROUP):
        below = pltpu.roll(p, d, axis=0)
        above = pltpu.roll(p, N_EXPERTS - d, axis=0)
        rank = rank + jnp.where((sub >= d) & (below >= p), 1, 0)
        rank = rank + jnp.where((sub + d < EXPERTS_PER_GROUP) & (above > p), 1, 0)
    top2 = rank < 2
    score = jnp.where(top2, p, 0.0)
    score = score + _partner(score, row, 1)
    score = score + _partner(score, row, 2)
    best = jnp.max(score, axis=0, keepdims=True)
    group = row >> 2
    best_group = jnp.min(jnp.where(score == best, group, N_EXPERTS), axis=0, keepdims=True)
    sel = top2 & (group == best_group)
    gate = jnp.where(sel, p / best, 0.0)

    sel_b = jnp.where(sel, 1.0, 0.0).astype(BF16)
    before = jnp.dot(sel_b, triu_ref[...], preferred_element_type=F32) + carry[...]
    tm = x.shape[0]
    new_carry = before[:, tm - 1:tm] + sel_b[:, tm - 1:tm].astype(F32)
    carry[...] = new_carry
    counts_ref[...] = new_carry

    e_lo = jnp.min(jnp.where(sel, row, N_EXPERTS), axis=0, keepdims=True)
    e_hi = jnp.max(jnp.where(sel, row, -1), axis=0, keepdims=True)
    is_lo = row == e_lo
    is_hi = row == e_hi
    pick = lambda m, a: jnp.sum(jnp.where(m, a, 0.0), axis=0, keepdims=True)
    zero = jnp.zeros_like(mx)
    route_ref[...] = jnp.concatenate(
        [e_lo.astype(F32), e_hi.astype(F32), pick(is_lo, before), pick(is_hi, before),
         pick(is_lo, gate), pick(is_hi, gate), zero, zero], axis=0)


def _merge(x2, att, conv, sc, wg, bg, wb, wo, g1, b1, wr, br):
    t, d = x2.shape
    tm = min(TM_PROJ, t)
    assert t % tm == 0
    triu = jnp.triu(jnp.ones((tm, tm), F32), 1).astype(BF16)
    const = lambda shape: pl.BlockSpec(shape, lambda i: (0,) * len(shape),
                                       pipeline_mode=pl.Buffered(1))
    rows = lambda w: pl.BlockSpec((tm, w), lambda i: (i, 0))
    return pl.pallas_call(
        _merge_kernel,
        grid=(t // tm,),
        in_specs=[rows(d), rows(WIDTH), rows(WIDTH), rows(WIDTH),
                  const(wg.shape), const(bg.shape), const(wb.shape), const(wo.shape),
                  const(g1.shape), const(b1.shape), const(wr.shape), const(br.shape),
                  const(triu.shape)],
        out_specs=[pl.BlockSpec((ROW_CHUNKS, tm, CHUNK), lambda i: (0, i, 0)),
                   pl.BlockSpec((ROUTE_ROWS, tm), lambda i: (0, i)),
                   pl.BlockSpec((N_EXPERTS, 1), lambda i: (0, 0))],
        out_shape=[jax.ShapeDtypeStruct((ROW_CHUNKS, t, CHUNK), F32),
                   jax.ShapeDtypeStruct((ROUTE_ROWS, t), F32),
                   jax.ShapeDtypeStruct((N_EXPERTS, 1), F32)],
        scratch_shapes=[pltpu.VMEM((N_EXPERTS, 1), F32)],
        compiler_params=pltpu.CompilerParams(dimension_semantics=("arbitrary",),
                                             vmem_limit_bytes=VMEM_LIMIT),
        name="merge",
    )(x2, att, conv, sc, wg, bg, wb, wo, g1, b1, wr, br, triu)


def _sc_mesh():
    return plsc.VectorSubcoreMesh(core_axis_name="core", subcore_axis_name="subcore")


def _sc_scatter_rows(x, idx, n_rows):
    t, d = x.shape
    m = idx.shape[0]
    w = SC_WINDOW
    assert t % w == 0 and m % t == 0
    steps_per_pass = t // w

    @functools.partial(pl.kernel, out_type=jax.ShapeDtypeStruct((n_rows, d), x.dtype),
                       mesh=_sc_mesh(), scratch_types=[], name="sc_dispatch")
    def run(x_hbm, i_hbm, o_hbm):
        def body(x_vmem, i_vmem):
            pltpu.sync_copy(x_vmem, o_hbm.at[i_vmem.at[0]])

        pltpu.emit_pipeline(
            body,
            grid=(m // w,),
            in_specs=[pl.BlockSpec((w, d), index_map=lambda i: (i % steps_per_pass, 0)),
                      pl.BlockSpec((1, w), index_map=lambda i: (0, i))],
            out_specs=[],
            core_axis_name=("core", "subcore"),
            dimension_semantics=(pltpu.PARALLEL,),
        )(x_hbm, i_hbm)

    return run(x, idx.reshape(1, m))


def _sc_gather_rows(table, idx):
    d = table.shape[1]
    m = idx.shape[0]
    w = SC_WINDOW
    assert m % w == 0

    @functools.partial(pl.kernel, out_type=jax.ShapeDtypeStruct((m, d), table.dtype),
                       mesh=_sc_mesh(), scratch_types=[], name="sc_gather")
    def run(t_hbm, i_hbm, o_hbm):
        def body(i_vmem, o_vmem):
            pltpu.sync_copy(t_hbm.at[i_vmem.at[0]], o_vmem)

        pltpu.emit_pipeline(
            body,
            grid=(m // w,),
            in_specs=[pl.BlockSpec((1, w), index_map=lambda i: (0, i))],
            out_specs=[pl.BlockSpec((w, d), index_map=lambda i: (i, 0))],
            core_axis_name=("core", "subcore"),
            dimension_semantics=(pltpu.PARALLEL,),
        )(i_hbm, o_hbm)

    return run(table, idx.reshape(1, m))


def _fill_kernel(fs_ref, fl_ref, xs_in, xs_ref, zeros, sem):
    del xs_in
    zb = zeros.shape[0]
    zeros[...] = jnp.zeros_like(zeros)

    n_regions = fs_ref.shape[0]
    rows_per_chunk = xs_ref.shape[0] // ROW_CHUNKS

    def region(e, _):
        start = fs_ref[e % n_regions] + (e // n_regions) * rows_per_chunk
        length = fl_ref[e % n_regions]
        head = jnp.minimum(length, (SUBLANES - start % SUBLANES) % SUBLANES)
        mid = start + head
        mid_len = (length - head) // SUBLANES * SUBLANES
        n_full = mid_len // zb
        rem = mid_len - n_full * zb
        part = mid + n_full * zb
        tail = mid + mid_len
        n_tail = length - head - mid_len

        def pieces(fn):
            def single(base):
                def body(r, _):
                    fn(pltpu.make_async_copy(zeros.at[pl.ds(0, 1)],
                                             xs_ref.at[pl.ds(base + r, 1)], sem))
                    return 0
                return body
            lax.fori_loop(0, head, single(start), 0)
            lax.fori_loop(0, n_tail, single(tail), 0)

            def full(c, _):
                off = pl.multiple_of(mid + c * zb, SUBLANES)
                fn(pltpu.make_async_copy(zeros, xs_ref.at[pl.ds(off, zb)], sem))
                return 0
            lax.fori_loop(0, n_full, full, 0)
            bit = zb // 2
            while bit >= SUBLANES:
                off = pl.multiple_of(part + (rem - rem % (2 * bit)), SUBLANES)

                @pl.when((rem & bit) != 0)
                def _(bit=bit, off=off):
                    fn(pltpu.make_async_copy(zeros.at[pl.ds(0, bit)],
                                             xs_ref.at[pl.ds(off, bit)], sem))
                bit //= 2

        pieces(lambda cp: cp.start())
        pieces(lambda cp: cp.wait())
        return 0

    lax.fori_loop(0, n_regions * ROW_CHUNKS, region, 0)


def _fill_unrouted(xs, fill_start, fill_len):
    n_rows, d = xs.shape
    return pl.pallas_call(
        _fill_kernel,
        grid_spec=pltpu.PrefetchScalarGridSpec(
            num_scalar_prefetch=2,
            grid=(1,),
            in_specs=[pl.BlockSpec(memory_space=pl.ANY)],
            out_specs=pl.BlockSpec(memory_space=pl.ANY),
            scratch_shapes=[pltpu.VMEM((ZERO_ROWS, d), xs.dtype), pltpu.SemaphoreType.DMA]),
        out_shape=jax.ShapeDtypeStruct((n_rows, d), xs.dtype),
        input_output_aliases={2: 0},
        compiler_params=pltpu.CompilerParams(dimension_semantics=("arbitrary",),
                                             vmem_limit_bytes=VMEM_LIMIT),
        name="fill_unrouted",
    )(fill_start, fill_len, xs)


def _expert_kernel(be_ref, nb_ref, xs_ref, wu_ref, wd_ref, ys_ref, wu_bf, wd_bf):
    b = pl.program_id(0)
    live = b < nb_ref[0]

    @pl.when(live & ((b == 0) | (be_ref[b] != be_ref[jnp.maximum(b, 1) - 1])))
    def _():
        wu_bf[...] = wu_ref[0, 0].astype(BF16)
        wd_bf[...] = wd_ref[0, 0].astype(BF16)

    @pl.when(live)
    def _():
        xb = jnp.concatenate([xs_ref[c] for c in range(ROW_CHUNKS)], axis=1).astype(BF16)
        hidden = jnp.dot(xb, wu_bf[...], preferred_element_type=F32)
        a = hidden[:, :EXPERT_FF]
        g = hidden[:, EXPERT_FF:]
        act = (g * _sigmoid(g) * a).astype(BF16)
        y = jnp.dot(act, wd_bf[...], preferred_element_type=F32)
        for c in range(ROW_CHUNKS):
            ys_ref[c] = y[:, c * CHUNK:(c + 1) * CHUNK]

    @pl.when(jnp.logical_not(live))
    def _():
        ys_ref[...] = jnp.zeros_like(ys_ref)


def _experts(xs, w_up, w_down, layer, block_expert, n_used):
    _, n_rows, d = xs.shape
    n_blocks = n_rows // EXPERT_BLOCK

    def live(b, nb):
        return jnp.minimum(b, nb[0] - 1)

    wspec = lambda w: pl.BlockSpec((1, 1) + w.shape[2:],
                                   lambda b, be, nb: (layer, be[live(b, nb)], 0, 0))
    return pl.pallas_call(
        _expert_kernel,
        grid_spec=pltpu.PrefetchScalarGridSpec(
            num_scalar_prefetch=2,
            grid=(n_blocks,),
            in_specs=[pl.BlockSpec((ROW_CHUNKS, EXPERT_BLOCK, d), lambda b, be, nb: (0, live(b, nb), 0)),
                      wspec(w_up), wspec(w_down)],
            out_specs=pl.BlockSpec((ROW_CHUNKS, EXPERT_BLOCK, d), lambda b, be, nb: (0, b, 0)),
            scratch_shapes=[pltpu.VMEM(w_up.shape[2:], BF16), pltpu.VMEM(w_down.shape[2:], BF16)]),
        out_shape=jax.ShapeDtypeStruct(xs.shape, F32),
        compiler_params=pltpu.CompilerParams(dimension_semantics=("arbitrary",),
                                             vmem_limit_bytes=VMEM_LIMIT),
        name="experts",
    )(block_expert, n_used, xs, w_up, w_down)


def _combine_kernel(x1_ref, route_ref, y0_ref, y1_ref, g2_ref, b2_ref, o_ref):
    route = route_ref[...].T
    full = lambda ref: jnp.concatenate([ref[c] for c in range(ROW_CHUNKS)], axis=1)
    m = route[:, 4:5] * full(y0_ref.at[0]) + route[:, 5:6] * full(y1_ref.at[0])
    o_ref[...] = _layer_norm(DEEPNORM_ALPHA * full(x1_ref) + m, g2_ref[...], b2_ref[...])


def _combine(x1, route, yk, g2, b2):
    _, t, dc = x1.shape
    d = ROW_CHUNKS * dc
    tm = min(TM_ROUTE, t)
    steps = t // tm
    const = lambda shape: pl.BlockSpec(shape, lambda i: (0,) * len(shape))
    return pl.pallas_call(
        _combine_kernel,
        grid=(steps,),
        in_specs=[pl.BlockSpec((ROW_CHUNKS, tm, dc), lambda i: (0, i, 0)),
                  pl.BlockSpec((ROUTE_ROWS, tm), lambda i: (0, i)),
                  pl.BlockSpec((1, ROW_CHUNKS, tm, dc), lambda i: (0, 0, i, 0)),
                  pl.BlockSpec((1, ROW_CHUNKS, tm, dc), lambda i: (1, 0, i, 0)),
                  const(g2.shape), const(b2.shape)],
        out_specs=pl.BlockSpec((tm, d), lambda i: (i, 0)),
        out_shape=jax.ShapeDtypeStruct((t, d), F32),
        compiler_params=pltpu.CompilerParams(dimension_semantics=("arbitrary",),
                                             vmem_limit_bytes=VMEM_LIMIT),
        name="combine",
    )(x1, route, yk, yk, g2, b2)


def _routing_tables(route, counts, n_blocks):
    t = route.shape[1]
    cnt = counts[:, 0].astype(jnp.int32)
    padded = (cnt + EXPERT_BLOCK - 1) // EXPERT_BLOCK * EXPERT_BLOCK
    pad_end = jnp.cumsum(padded)
    pad_start = pad_end - padded
    experts = route[0:2].astype(jnp.int32)
    ranks = route[2:4].astype(jnp.int32)
    ids = jnp.arange(N_EXPERTS, dtype=jnp.int32)
    base = jnp.sum(jnp.where(experts[:, :, None] == ids, pad_start, 0), axis=-1)
    n_rows = n_blocks * EXPERT_BLOCK
    chunk_base = jnp.arange(ROW_CHUNKS, dtype=jnp.int32) * n_rows
    pos = ((base + ranks)[:, None, :] + chunk_base[None, :, None]).reshape(2 * ROW_CHUNKS * t)
    blk = jnp.arange(n_blocks, dtype=jnp.int32) * EXPERT_BLOCK
    block_expert = jnp.minimum(jnp.sum((pad_end[None, :] <= blk[:, None]).astype(jnp.int32), axis=1),
                               N_EXPERTS - 1)
    n_used = (pad_end[-1:] // EXPERT_BLOCK).astype(jnp.int32)
    fill_start = jnp.concatenate([pad_start + cnt, pad_end[-1:]]).astype(jnp.int32)
    fill_len = jnp.concatenate([padded - cnt, n_rows - pad_end[-1:]]).astype(jnp.int32)
    return pos, block_expert, n_used, fill_start, fill_len


def kernel(x, w_in, b_forget, conv_w, conv_b, conv_ln_g, conv_ln_b, sc_w, w_branch, w_gate, b_gate,
           w_out, ln1_g, ln1_b, w_up, w_down, ln2_g, ln2_b, w_router, b_router):
    batch, seq, d = x.shape
    t = batch * seq
    depth = w_in.shape[0]
    x2 = x.reshape(t, d)

    nf = ATT_HEADS
    w_main = jnp.concatenate([w_in[:, :, :2 * WIDTH], w_in[:, :, 3 * WIDTH + nf:]], axis=2).astype(BF16)
    w_vt = jnp.swapaxes(w_in[:, :, 2 * WIDTH:3 * WIDTH], 1, 2).astype(BF16)
    w_f = jnp.pad(w_in[:, :, 3 * WIDTH:3 * WIDTH + nf], ((0, 0), (0, 0), (0, LANES - nf))).astype(BF16)
    b_f = jnp.pad(b_forget, ((0, 0), (0, LANES - nf)))[:, None, :]
    wg = w_gate.astype(BF16)
    wb = w_branch.astype(BF16)
    wo = w_out.astype(BF16)
    wr_t = w_router.T
    wr_hi = wr_t.astype(BF16)
    wr_lo = (wr_t - wr_hi.astype(F32)).astype(BF16)
    wr = jnp.stack([wr_hi, wr_lo])
    br = b_router[:, None]

    n_blocks = (2 * t + N_EXPERTS * (EXPERT_BLOCK - 1) + EXPERT_BLOCK - 1) // EXPERT_BLOCK
    n_rows = n_blocks * EXPERT_BLOCK

    for l in range(depth):
        qf, kf, vt, u, br_sc = _in_proj(x2, w_main[l], w_vt[l], w_f[l], b_f[l], sc_w[l], seq)
        br_conv = _conv31(u, conv_w[l], conv_b[l][None, :], conv_ln_g[l][None, :],
                          conv_ln_b[l][None, :], seq)
        br_att = _fox_attention(qf, kf, vt, batch, seq)
        x1, route, counts = _merge(x2, br_att, br_conv, br_sc, wg[l], b_gate[l][:, None, :], wb[l],
                                   wo[l], ln1_g[l][None, :], ln1_b[l][None, :], wr, br)
        pos, block_expert, n_used, fill_start, fill_len = _routing_tables(route, counts, n_blocks)
        xs = _sc_scatter_rows(x1.reshape(ROW_CHUNKS * t, CHUNK), pos, ROW_CHUNKS * n_rows)
        xs = _fill_unrouted(xs, fill_start, fill_len).reshape(ROW_CHUNKS, n_rows, CHUNK)
        ys = _experts(xs, w_up, w_down, l, block_expert, n_used)
        yk = _sc_gather_rows(ys.reshape(ROW_CHUNKS * n_rows, CHUNK), pos)
        x2 = _combine(x1, route, yk.reshape(2, ROW_CHUNKS, t, CHUNK),
                      ln2_g[l][None, :], ln2_b[l][None, :])
    return x2.reshape(batch, seq, d)
```

```python
import functools

import numpy as np
import jax
import jax.numpy as jnp
from jax import lax
from jax.experimental import pallas as pl
from jax.experimental.pallas import tpu as pltpu
from jax.experimental.pallas import tpu_sc as plsc

F32 = jnp.float32
BF16 = jnp.bfloat16

ATT_HEADS = 8
ATT_HEAD_DIM = 64
WIDTH = 512
CONV_TAPS = 31
SC_TAPS = 3
N_EXPERTS = 32
EXPERTS_PER_GROUP = 4
EXPERT_FF = 512
DEPTH = 4
DEEPNORM_ALPHA = (2 * DEPTH) ** 0.25
LN_EPS = 1e-5

LANES = 128
SUBLANES = 8
VMEM_LIMIT = 56 * 1024 * 1024

TM_PROJ = 512
TM_CONV = 512
CONV_HALO = 32
CONV_CHUNK = 64
TQ = 512
TK = 512
ATT_TILES_PER_BLOCK = 4
EXPERT_BLOCK = 512
TM_ROUTE = 512
SC_WINDOW = 128
ROW_CHUNKS = 4
CHUNK = 256
ROUTE_ROWS = 8
ZERO_ROWS = 256
NEG_BIG = -1e30
LOG2E = 1.4426950408889634


def _sigmoid(x):
    return 1.0 / (1.0 + jnp.exp(-x))


def _log_sigmoid(x):
    return jnp.minimum(x, 0.0) - jnp.log(1.0 + jnp.exp(-jnp.abs(x)))


def _split3(x):
    hi = x.astype(BF16)
    r1 = x - hi.astype(F32)
    mid = r1.astype(BF16)
    lo = (r1 - mid.astype(F32)).astype(BF16)
    return hi, mid, lo


def _layer_norm(y, g, b):
    mu = jnp.mean(y, axis=-1, keepdims=True)
    yc = y - mu
    var = jnp.mean(yc * yc, axis=-1, keepdims=True)
    return yc * lax.rsqrt(var + LN_EPS) * g + b


def _bias_placement():
    pairs = ATT_HEADS // 2
    place = np.zeros((3 * LANES, 2 * pairs * LANES), np.float32)
    ones = np.zeros((1, 2 * pairs * LANES), np.float32)
    for p in range(pairs):
        for h in range(2):
            for piece in range(3):
                src = piece * LANES + 2 * p + h
                place[src, p * LANES + 6 * h + 3 + piece] = 1.0
                place[src, (pairs + p) * LANES + 6 * h + piece] = -1.0
                ones[0, p * LANES + 6 * h + piece] = 1.0
                ones[0, (pairs + p) * LANES + 6 * h + 3 + piece] = 1.0
    return jnp.asarray(place, BF16), jnp.asarray(ones, F32)


def _in_proj_kernel(x_ref, wm_ref, wvt_ref, wf_ref, bf_ref, scw_ref, tri_ref, place_ref, ones_ref,
                    qf_ref, kf_ref, vt_ref, u_ref, brsc_ref,
                    carry_c, carry_p, *, tiles_per_seq):
    i = pl.program_id(0)

    @pl.when(i % tiles_per_seq == 0)
    def _():
        carry_c[...] = jnp.zeros_like(carry_c)
        carry_p[...] = jnp.zeros_like(carry_p)

    tm = x_ref.shape[0]
    xb = x_ref[...].astype(BF16)

    def mm(j):
        return jnp.dot(xb, wm_ref[:, j * WIDTH:(j + 1) * WIDTH], preferred_element_type=F32)

    q = (mm(0) * (ATT_HEAD_DIM ** -0.5 * LOG2E)).astype(BF16)
    k = mm(1).astype(BF16)
    vt_ref[...] = lax.dot_general(wvt_ref[...], xb, (((1,), (1,)), ((), ())),
                                  preferred_element_type=F32).astype(BF16)
    u_ref[...] = mm(2) * _sigmoid(mm(3))

    p = mm(5) * mm(6)
    row = lax.broadcasted_iota(jnp.int32, p.shape, 0)
    prev = carry_p[...]
    p1 = jnp.where(row == 0, prev[7:8, :], pltpu.roll(p, 1, axis=0))
    p2 = pltpu.roll(p, 2, axis=0)
    p2 = jnp.where(row == 0, prev[6:7, :], jnp.where(row == 1, prev[7:8, :], p2))
    y = scw_ref[0:1, :] * p2 + scw_ref[1:2, :] * p1 + scw_ref[2:3, :] * p
    brsc_ref[...] = (mm(4) * y).astype(BF16)
    carry_p[...] = p[tm - SUBLANES:tm, :]

    zf = jnp.dot(xb, wf_ref[...], preferred_element_type=F32)
    ls = _log_sigmoid(zf + bf_ref[...])
    hi, mid, lo = _split3(ls)
    tri = tri_ref[...]
    c = (jnp.dot(tri, hi, preferred_element_type=F32)
         + jnp.dot(tri, mid, preferred_element_type=F32)
         + jnp.dot(tri, lo, preferred_element_type=F32)) + carry_c[...]
    carry_c[...] = c[tm - 1:tm, :]

    bias = ones_ref[...]
    for n, piece in enumerate(_split3(c * LOG2E)):
        bias = bias + jnp.dot(piece, place_ref[n * LANES:(n + 1) * LANES, :],
                              preferred_element_type=F32)
    bias = bias.astype(BF16)
    pairs = ATT_HEADS // 2
    for p in range(pairs):
        lo_, hi_ = p * LANES, (p + 1) * LANES
        qf_ref[:, 2 * lo_:2 * lo_ + LANES] = q[:, lo_:hi_]
        qf_ref[:, 2 * lo_ + LANES:2 * hi_] = bias[:, lo_:hi_]
        kf_ref[:, 2 * lo_:2 * lo_ + LANES] = k[:, lo_:hi_]
        kf_ref[:, 2 * lo_ + LANES:2 * hi_] = bias[:, (pairs + p) * LANES:(pairs + p + 1) * LANES]


def _in_proj(x2, wm, wvt, wf, bf, scw, seq):
    t, d = x2.shape
    tm = min(TM_PROJ, seq)
    assert seq % tm == 0 and t % seq == 0
    tri = jnp.tril(jnp.ones((tm, tm), F32)).astype(BF16)
    place, ones = _bias_placement()
    const = lambda shape: pl.BlockSpec(shape, lambda i: (0,) * len(shape))
    rows = lambda w: pl.BlockSpec((tm, w), lambda i: (i, 0))
    return pl.pallas_call(
        functools.partial(_in_proj_kernel, tiles_per_seq=seq // tm),
        grid=(t // tm,),
        in_specs=[rows(d), const(wm.shape), const(wvt.shape), const(wf.shape), const(bf.shape),
                  const(scw.shape), const(tri.shape), const(place.shape), const(ones.shape)],
        out_specs=[rows(2 * WIDTH), rows(2 * WIDTH), pl.BlockSpec((WIDTH, tm), lambda i: (0, i)),
                   rows(WIDTH), rows(WIDTH)],
        out_shape=[jax.ShapeDtypeStruct((t, 2 * WIDTH), BF16),
                   jax.ShapeDtypeStruct((t, 2 * WIDTH), BF16),
                   jax.ShapeDtypeStruct((WIDTH, t), BF16),
                   jax.ShapeDtypeStruct((t, WIDTH), F32),
                   jax.ShapeDtypeStruct((t, WIDTH), BF16)],
        scratch_shapes=[pltpu.VMEM((1, LANES), F32), pltpu.VMEM((SUBLANES, WIDTH), F32)],
        compiler_params=pltpu.CompilerParams(dimension_semantics=("arbitrary",),
                                             vmem_limit_bytes=VMEM_LIMIT),
        name="in_proj",
    )(x2, wm, wvt, wf, bf, scw, tri, place, ones)


def _conv31_kernel(u_ref, w_ref, cb_ref, g_ref, b_ref, o_ref, ext, pre, *, tiles_per_seq):
    i = pl.program_id(0)
    tm = u_ref.shape[0]

    @pl.when(i % tiles_per_seq == 0)
    def _():
        ext[0:CONV_HALO, :] = jnp.zeros((CONV_HALO, WIDTH), F32)
        ext[CONV_HALO + tm:CONV_HALO + tm + SUBLANES, :] = jnp.zeros((SUBLANES, WIDTH), F32)

    ext[CONV_HALO:CONV_HALO + tm, :] = u_ref[...]
    base = CONV_HALO - (CONV_TAPS - 1)
    span = CONV_CHUNK + SUBLANES
    half = WIDTH // 2
    for c0 in range(0, tm, CONV_CHUNK):
        for l0 in (0, half):
            y = None
            for r in range(SUBLANES):
                z = None
                for a in range((base + CONV_TAPS - 1) // SUBLANES + 1):
                    k = SUBLANES * a + r - base
                    if 0 <= k < CONV_TAPS:
                        term = (w_ref[k:k + 1, l0:l0 + half]
                                * ext[c0 + SUBLANES * a:c0 + SUBLANES * a + span, l0:l0 + half])
                        z = term if z is None else z + term
                zr = z[r:r + CONV_CHUNK, :]
                y = zr if y is None else y + zr
            pre[c0:c0 + CONV_CHUNK, l0:l0 + half] = y
    for c0 in range(0, tm, CONV_CHUNK):
        y = _layer_norm(pre[c0:c0 + CONV_CHUNK, :] + cb_ref[...], g_ref[...], b_ref[...])
        o_ref[c0:c0 + CONV_CHUNK, :] = (y * _sigmoid(y)).astype(BF16)
    ext[0:CONV_HALO, :] = ext[tm:tm + CONV_HALO, :]


def _conv31(u, w, cb, g, b, seq):
    t = u.shape[0]
    tm = min(TM_CONV, seq)
    assert seq % tm == 0 and tm % CONV_CHUNK == 0
    const = lambda shape: pl.BlockSpec(shape, lambda i: (0,) * len(shape))
    return pl.pallas_call(
        functools.partial(_conv31_kernel, tiles_per_seq=seq // tm),
        grid=(t // tm,),
        in_specs=[pl.BlockSpec((tm, WIDTH), lambda i: (i, 0)), const(w.shape), const(cb.shape),
                  const(g.shape), const(b.shape)],
        out_specs=pl.BlockSpec((tm, WIDTH), lambda i: (i, 0)),
        out_shape=jax.ShapeDtypeStruct((t, WIDTH), BF16),
        scratch_shapes=[pltpu.VMEM((CONV_HALO + tm + SUBLANES, WIDTH), F32),
                        pltpu.VMEM((tm, WIDTH), F32)],
        compiler_params=pltpu.CompilerParams(dimension_semantics=("arbitrary",),
                                             vmem_limit_bytes=VMEM_LIMIT),
        name="conv31",
    )(u, w, cb, g, b)


def _fox_kernel(qf_ref, kf_ref, vt_ref, o_ref, acc0_ref, acc1_ref, *, tk):
    qi = pl.program_id(2)
    tq = qf_ref.shape[0]
    lane = lax.broadcasted_iota(jnp.int32, (tq, 2 * LANES), 1)
    bias_lane = lane - LANES
    qfull = qf_ref[...]
    zero = jnp.zeros_like(qfull)
    qs = [jnp.where(((lane >= h * ATT_HEAD_DIM) & (lane < (h + 1) * ATT_HEAD_DIM))
                    | ((bias_lane >= 6 * h) & (bias_lane < 6 * h + 6)), qfull, zero)
          for h in range(2)]
    q_lo = qi * tq
    accs = (acc0_ref, acc1_ref)
    for acc in accs:
        acc[...] = jnp.zeros(acc.shape, F32)

    def tiles(jobs, carry):
        work = []
        for j, masked in jobs:
            start = pl.multiple_of(j * tk, tk)
            kfull = kf_ref[pl.ds(start, tk), :]
            ss = [lax.dot_general(kfull, qs[h], (((1,), (1,)), ((), ())),
                                  preferred_element_type=F32) for h in range(2)]
            work.append((start, masked, ss))
        carry = list(carry)
        for start, masked, ss in work:
            vt = vt_ref[:, pl.ds(start, tk)]
            if masked:
                key = start + lax.broadcasted_iota(jnp.int32, ss[0].shape, 0)
                qry = q_lo + lax.broadcasted_iota(jnp.int32, ss[0].shape, 1)
                ss = [jnp.where(key <= qry, s, NEG_BIG) for s in ss]
            for h in range(2):
                m_old, l_old = carry[h]
                m_new = jnp.maximum(m_old, jnp.max(ss[h], axis=0, keepdims=True))
                a = jnp.exp2(m_old - m_new)
                p = jnp.exp2(ss[h] - m_new)
                l_new = a * l_old + jnp.sum(p.reshape(tk // SUBLANES, SUBLANES, tq), axis=0)
                accs[h][...] = a * accs[h][...] + jnp.dot(vt, p.astype(BF16),
                                                          preferred_element_type=F32)
                carry[h] = (m_new, l_new)
        return tuple(carry)

    init = tuple((jnp.full((1, tq), NEG_BIG, F32), jnp.zeros((SUBLANES, tq), F32)) for _ in range(2))
    n_full = q_lo // tk
    group = ATT_TILES_PER_BLOCK
    carry = lax.fori_loop(
        0, n_full // group,
        lambda j, c: tiles([(group * j + i, False) for i in range(group)], c), init)
    rest = n_full % group

    def tail(n):
        return lambda c: tiles([(n_full - n + i, False) for i in range(n)] + [(n_full, True)], c)

    (_, l0), (_, l1) = lax.switch(rest, [tail(n) for n in range(group)], carry)
    row = lax.broadcasted_iota(jnp.int32, (LANES, tq), 0)
    out_t = jnp.where(row < ATT_HEAD_DIM,
                      acc0_ref[...] / jnp.sum(l0, axis=0, keepdims=True),
                      acc1_ref[...] / jnp.sum(l1, axis=0, keepdims=True))
    o_ref[...] = out_t.T.astype(BF16)


def _fox_attention(qf, kf, vt, batch, seq):
    t = qf.shape[0]
    tq = min(TQ, seq)
    tk = min(TK, seq)
    assert seq % tq == 0 and seq % tk == 0 and tk % tq == 0
    nq = seq // tq
    pairs = ATT_HEADS // 2
    return pl.pallas_call(
        functools.partial(_fox_kernel, tk=tk),
        grid=(batch, pairs, nq),
        in_specs=[pl.BlockSpec((tq, 2 * LANES), lambda b, hp, qi: (b * nq + qi, hp)),
                  pl.BlockSpec((seq, 2 * LANES), lambda b, hp, qi: (b, hp)),
                  pl.BlockSpec((LANES, seq), lambda b, hp, qi: (hp, b))],
        out_specs=pl.BlockSpec((tq, LANES), lambda b, hp, qi: (b * nq + qi, hp)),
        out_shape=jax.ShapeDtypeStruct((t, WIDTH), BF16),
        scratch_shapes=[pltpu.VMEM((LANES, tq), F32), pltpu.VMEM((LANES, tq), F32)],
        compiler_params=pltpu.CompilerParams(
            dimension_semantics=("arbitrary", "arbitrary", "arbitrary"),
            vmem_limit_bytes=VMEM_LIMIT),
        name="fox_attn",
    )(qf, kf, vt)


def _partner(x, row, bit):
    n = x.shape[0]
    up = pltpu.roll(x, n - bit, axis=0)
    down = pltpu.roll(x, bit, axis=0)
    return jnp.where((row & bit) == 0, up, down)


def _merge_kernel(x_ref, att_ref, conv_ref, sc_ref, wg_ref, bg_ref, wb_ref, wo_ref,
                  g1_ref, b1_ref, wr_ref, br_ref, triu_ref,
                  x1_ref, route_ref, counts_ref, carry):
    i = pl.program_id(0)

    @pl.when(i == 0)
    def _():
        carry[...] = jnp.zeros_like(carry)

    x = x_ref[...]
    xb = x.astype(BF16)
    merged = None
    for n, br_ref_n in enumerate((att_ref, conv_ref, sc_ref)):
        gate = _sigmoid(jnp.dot(xb, wg_ref[n], preferred_element_type=F32) + bg_ref[n])
        term = gate * jnp.dot(br_ref_n[...], wb_ref[n], preferred_element_type=F32)
        merged = term if merged is None else merged + term
    h = jnp.dot(merged.astype(BF16), wo_ref[...], preferred_element_type=F32)
    x1 = _layer_norm(DEEPNORM_ALPHA * x + h, g1_ref[...], b1_ref[...])
    for c in range(ROW_CHUNKS):
        x1_ref[c] = x1[:, c * CHUNK:(c + 1) * CHUNK]

    x_hi = x1.astype(BF16)
    x_lo = (x1 - x_hi.astype(F32)).astype(BF16)
    nt = lambda w, v: lax.dot_general(w, v, (((1,), (1,)), ((), ())), preferred_element_type=F32)
    logits = nt(wr_ref[0], x_hi) + nt(wr_ref[0], x_lo) + nt(wr_ref[1], x_hi) + br_ref[...]
    mx = jnp.max(logits, axis=0, keepdims=True)
    ex = jnp.exp(logits - mx)
    p = ex / jnp.sum(ex, axis=0, keepdims=True)

    row = lax.broadcasted_iota(jnp.int32, p.shape, 0)
    sub = row & (EXPERTS_PER_GROUP - 1)
    rank = jnp.zeros(p.shape, jnp.int32)
    for d in range(1, EXPERTS_PER_GROUP):
        below = pltpu.roll(p, d, axis=0)
        above = pltpu.roll(p, N_EXPERTS - d, axis=0)
        rank = rank + jnp.where((sub >= d) & (below >= p), 1, 0)
        rank = rank + jnp.where((sub + d < EXPERTS_PER_GROUP) & (above > p), 1, 0)
    top2 = rank < 2
    score = jnp.where(top2, p, 0.0)
    score = score + _partner(score, row, 1)
    score = score + _partner(score, row, 2)
    best = jnp.max(score, axis=0, keepdims=True)
    group = row >> 2
    best_group = jnp.min(jnp.where(score == best, group, N_EXPERTS), axis=0, keepdims=True)
    sel = top2 & (group == best_group)
    gate = jnp.where(sel, p / best, 0.0)

    sel_b = jnp.where(sel, 1.0, 0.0).astype(BF16)
    before = jnp.dot(sel_b, triu_ref[...], preferred_element_type=F32) + carry[...]
    tm = x.shape[0]
    new_carry = before[:, tm - 1:tm] + sel_b[:, tm - 1:tm].astype(F32)
    carry[...] = new_carry
    counts_ref[...] = new_carry

    e_lo = jnp.min(jnp.where(sel, row, N_EXPERTS), axis=0, keepdims=True)
    e_hi = jnp.max(jnp.where(sel, row, -1), axis=0, keepdims=True)
    is_lo = row == e_lo
    is_hi = row == e_hi
    pick = lambda m, a: jnp.sum(jnp.where(m, a, 0.0), axis=0, keepdims=True)
    zero = jnp.zeros_like(mx)
    route_ref[...] = jnp.concatenate(
        [e_lo.astype(F32), e_hi.astype(F32), pick(is_lo, before), pick(is_hi, before),
         pick(is_lo, gate), pick(is_hi, gate), zero, zero], axis=0)


def _merge(x2, att, conv, sc, wg, bg, wb, wo, g1, b1, wr, br):
    t, d = x2.shape
    tm = min(TM_PROJ, t)
    assert t % tm == 0
    triu = jnp.triu(jnp.ones((tm, tm), F32), 1).astype(BF16)
    const = lambda shape: pl.BlockSpec(shape, lambda i: (0,) * len(shape),
                                       pipeline_mode=pl.Buffered(1))
    rows = lambda w: pl.BlockSpec((tm, w), lambda i: (i, 0))
    return pl.pallas_call(
        _merge_kernel,
        grid=(t // tm,),
        in_specs=[rows(d), rows(WIDTH), rows(WIDTH), rows(WIDTH),
                  const(wg.shape), const(bg.shape), const(wb.shape), const(wo.shape),
                  const(g1.shape), const(b1.shape), const(wr.shape), const(br.shape),
                  const(triu.shape)],
        out_specs=[pl.BlockSpec((ROW_CHUNKS, tm, CHUNK), lambda i: (0, i, 0)),
                   pl.BlockSpec((ROUTE_ROWS, tm), lambda i: (0, i)),
                   pl.BlockSpec((N_EXPERTS, 1), lambda i: (0, 0))],
        out_shape=[jax.ShapeDtypeStruct((ROW_CHUNKS, t, CHUNK), F32),
                   jax.ShapeDtypeStruct((ROUTE_ROWS, t), F32),
                   jax.ShapeDtypeStruct((N_EXPERTS, 1), F32)],
        scratch_shapes=[pltpu.VMEM((N_EXPERTS, 1), F32)],
        compiler_params=pltpu.CompilerParams(dimension_semantics=("arbitrary",),
                                             vmem_limit_bytes=VMEM_LIMIT),
        name="merge",
    )(x2, att, conv, sc, wg, bg, wb, wo, g1, b1, wr, br, triu)


def _sc_mesh():
    return plsc.VectorSubcoreMesh(core_axis_name="core", subcore_axis_name="subcore")


def _sc_scatter_rows(x, idx, n_rows):
    t, d = x.shape
    m = idx.shape[0]
    w = SC_WINDOW
    assert t % w == 0 and m % t == 0
    steps_per_pass = t // w

    @functools.partial(pl.kernel, out_type=jax.ShapeDtypeStruct((n_rows, d), x.dtype),
                       mesh=_sc_mesh(), scratch_types=[], name="sc_dispatch")
    def run(x_hbm, i_hbm, o_hbm):
        def body(x_vmem, i_vmem):
            pltpu.sync_copy(x_vmem, o_hbm.at[i_vmem.at[0]])

        pltpu.emit_pipeline(
            body,
            grid=(m // w,),
            in_specs=[pl.BlockSpec((w, d), index_map=lambda i: (i % steps_per_pass, 0)),
                      pl.BlockSpec((1, w), index_map=lambda i: (0, i))],
            out_specs=[],
            core_axis_name=("core", "subcore"),
            dimension_semantics=(pltpu.PARALLEL,),
            trace_scopes=False,
        )(x_hbm, i_hbm)

    return run(x, idx.reshape(1, m))


def _sc_gather_rows(table, idx):
    d = table.shape[1]
    m = idx.shape[0]
    w = SC_WINDOW
    assert m % w == 0

    @functools.partial(pl.kernel, out_type=jax.ShapeDtypeStruct((m, d), table.dtype),
                       mesh=_sc_mesh(), scratch_types=[], name="sc_gather")
    def run(t_hbm, i_hbm, o_hbm):
        def body(i_vmem, o_vmem):
            pltpu.sync_copy(t_hbm.at[i_vmem.at[0]], o_vmem)

        pltpu.emit_pipeline(
            body,
            grid=(m // w,),
            in_specs=[pl.BlockSpec((1, w), index_map=lambda i: (0, i))],
            out_specs=[pl.BlockSpec((w, d), index_map=lambda i: (i, 0))],
            core_axis_name=("core", "subcore"),
            dimension_semantics=(pltpu.PARALLEL,),
            trace_scopes=False,
        )(i_hbm, o_hbm)

    return run(table, idx.reshape(1, m))


def _fill_kernel(fs_ref, fl_ref, xs_in, xs_ref, zeros, sem):
    del xs_in
    zb = zeros.shape[0]
    zeros[...] = jnp.zeros_like(zeros)

    n_regions = fs_ref.shape[0]
    rows_per_chunk = xs_ref.shape[0] // ROW_CHUNKS

    def region(e, _):
        start = fs_ref[e % n_regions] + (e // n_regions) * rows_per_chunk
        length = fl_ref[e % n_regions]
        head = jnp.minimum(length, (SUBLANES - start % SUBLANES) % SUBLANES)
        mid = start + head
        mid_len = (length - head) // SUBLANES * SUBLANES
        n_full = mid_len // zb
        rem = mid_len - n_full * zb
        part = mid + n_full * zb
        tail = mid + mid_len
        n_tail = length - head - mid_len

        def pieces(fn):
            def single(base):
                def body(r, _):
                    fn(pltpu.make_async_copy(zeros.at[pl.ds(0, 1)],
                                             xs_ref.at[pl.ds(base + r, 1)], sem))
                    return 0
                return body
            lax.fori_loop(0, head, single(start), 0)
            lax.fori_loop(0, n_tail, single(tail), 0)

            def full(c, _):
                off = pl.multiple_of(mid + c * zb, SUBLANES)
                fn(pltpu.make_async_copy(zeros, xs_ref.at[pl.ds(off, zb)], sem))
                return 0
            lax.fori_loop(0, n_full, full, 0)
            bit = zb // 2
            while bit >= SUBLANES:
                off = pl.multiple_of(part + (rem - rem % (2 * bit)), SUBLANES)

                @pl.when((rem & bit) != 0)
                def _(bit=bit, off=off):
                    fn(pltpu.make_async_copy(zeros.at[pl.ds(0, bit)],
                                             xs_ref.at[pl.ds(off, bit)], sem))
                bit //= 2

        pieces(lambda cp: cp.start())
        pieces(lambda cp: cp.wait())
        return 0

    lax.fori_loop(0, n_regions * ROW_CHUNKS, region, 0)


def _fill_unrouted(xs, fill_start, fill_len):
    n_rows, d = xs.shape
    return pl.pallas_call(
        _fill_kernel,
        grid_spec=pltpu.PrefetchScalarGridSpec(
            num_scalar_prefetch=2,
            grid=(1,),
            in_specs=[pl.BlockSpec(memory_space=pl.ANY)],
            out_specs=pl.BlockSpec(memory_space=pl.ANY),
            scratch_shapes=[pltpu.VMEM((ZERO_ROWS, d), xs.dtype), pltpu.SemaphoreType.DMA]),
        out_shape=jax.ShapeDtypeStruct((n_rows, d), xs.dtype),
        input_output_aliases={2: 0},
        compiler_params=pltpu.CompilerParams(dimension_semantics=("arbitrary",),
                                             vmem_limit_bytes=VMEM_LIMIT),
        name="fill_unrouted",
    )(fill_start, fill_len, xs)


def _expert_kernel(be_ref, nb_ref, xs_ref, wu_ref, wd_ref, ys_ref, wu_bf, wd_bf):
    b = pl.program_id(0)
    live = b < nb_ref[0]

    @pl.when(live & ((b == 0) | (be_ref[b] != be_ref[jnp.maximum(b, 1) - 1])))
    def _():
        wu_bf[...] = wu_ref[0, 0].astype(BF16)
        wd_bf[...] = wd_ref[0, 0].astype(BF16)

    @pl.when(live)
    def _():
        xb = jnp.concatenate([xs_ref[c] for c in range(ROW_CHUNKS)], axis=1).astype(BF16)
        hidden = jnp.dot(xb, wu_bf[...], preferred_element_type=F32)
        a = hidden[:, :EXPERT_FF]
        g = hidden[:, EXPERT_FF:]
        act = (g * _sigmoid(g) * a).astype(BF16)
        y = jnp.dot(act, wd_bf[...], preferred_element_type=F32)
        for c in range(ROW_CHUNKS):
            ys_ref[c] = y[:, c * CHUNK:(c + 1) * CHUNK]

    @pl.when(jnp.logical_not(live))
    def _():
        ys_ref[...] = jnp.zeros_like(ys_ref)


def _experts(xs, w_up, w_down, layer, block_expert, n_used):
    _, n_rows, d = xs.shape
    n_blocks = n_rows // EXPERT_BLOCK

    def live(b, nb):
        return jnp.minimum(b, nb[0] - 1)

    wspec = lambda w: pl.BlockSpec((1, 1) + w.shape[2:],
                                   lambda b, be, nb: (layer, be[live(b, nb)], 0, 0))
    return pl.pallas_call(
        _expert_kernel,
        grid_spec=pltpu.PrefetchScalarGridSpec(
            num_scalar_prefetch=2,
            grid=(n_blocks,),
            in_specs=[pl.BlockSpec((ROW_CHUNKS, EXPERT_BLOCK, d), lambda b, be, nb: (0, live(b, nb), 0)),
                      wspec(w_up), wspec(w_down)],
            out_specs=pl.BlockSpec((ROW_CHUNKS, EXPERT_BLOCK, d), lambda b, be, nb: (0, b, 0)),
            scratch_shapes=[pltpu.VMEM(w_up.shape[2:], BF16), pltpu.VMEM(w_down.shape[2:], BF16)]),
        out_shape=jax.ShapeDtypeStruct(xs.shape, F32),
        compiler_params=pltpu.CompilerParams(dimension_semantics=("arbitrary",),
                                             vmem_limit_bytes=VMEM_LIMIT),
        name="experts",
    )(block_expert, n_used, xs, w_up, w_down)


def _combine_kernel(x1_ref, route_ref, y0_ref, y1_ref, g2_ref, b2_ref, o_ref):
    route = route_ref[...].T
    full = lambda ref: jnp.concatenate([ref[c] for c in range(ROW_CHUNKS)], axis=1)
    m = route[:, 4:5] * full(y0_ref.at[0]) + route[:, 5:6] * full(y1_ref.at[0])
    o_ref[...] = _layer_norm(DEEPNORM_ALPHA * full(x1_ref) + m, g2_ref[...], b2_ref[...])


def _combine(x1, route, yk, g2, b2):
    _, t, dc = x1.shape
    d = ROW_CHUNKS * dc
    tm = min(TM_ROUTE, t)
    steps = t // tm
    const = lambda shape: pl.BlockSpec(shape, lambda i: (0,) * len(shape))
    return pl.pallas_call(
        _combine_kernel,
        grid=(steps,),
        in_specs=[pl.BlockSpec((ROW_CHUNKS, tm, dc), lambda i: (0, i, 0)),
                  pl.BlockSpec((ROUTE_ROWS, tm), lambda i: (0, i)),
                  pl.BlockSpec((1, ROW_CHUNKS, tm, dc), lambda i: (0, 0, i, 0)),
                  pl.BlockSpec((1, ROW_CHUNKS, tm, dc), lambda i: (1, 0, i, 0)),
                  const(g2.shape), const(b2.shape)],
        out_specs=pl.BlockSpec((tm, d), lambda i: (i, 0)),
        out_shape=jax.ShapeDtypeStruct((t, d), F32),
        compiler_params=pltpu.CompilerParams(dimension_semantics=("arbitrary",),
                                             vmem_limit_bytes=VMEM_LIMIT),
        name="combine",
    )(x1, route, yk, yk, g2, b2)


def _routing_tables(route, counts, n_blocks):
    t = route.shape[1]
    cnt = counts[:, 0].astype(jnp.int32)
    padded = (cnt + EXPERT_BLOCK - 1) // EXPERT_BLOCK * EXPERT_BLOCK
    pad_end = jnp.cumsum(padded)
    pad_start = pad_end - padded
    experts = route[0:2].astype(jnp.int32)
    ranks = route[2:4].astype(jnp.int32)
    ids = jnp.arange(N_EXPERTS, dtype=jnp.int32)
    base = jnp.sum(jnp.where(experts[:, :, None] == ids, pad_start, 0), axis=-1)
    n_rows = n_blocks * EXPERT_BLOCK
    chunk_base = jnp.arange(ROW_CHUNKS, dtype=jnp.int32) * n_rows
    pos = ((base + ranks)[:, None, :] + chunk_base[None, :, None]).reshape(2 * ROW_CHUNKS * t)
    blk = jnp.arange(n_blocks, dtype=jnp.int32) * EXPERT_BLOCK
    block_expert = jnp.minimum(jnp.sum((pad_end[None, :] <= blk[:, None]).astype(jnp.int32), axis=1),
                               N_EXPERTS - 1)
    n_used = (pad_end[-1:] // EXPERT_BLOCK).astype(jnp.int32)
    fill_start = jnp.concatenate([pad_start + cnt, pad_end[-1:]]).astype(jnp.int32)
    fill_len = jnp.concatenate([padded - cnt, n_rows - pad_end[-1:]]).astype(jnp.int32)
    return pos, block_expert, n_used, fill_start, fill_len


def kernel(x, w_in, b_forget, conv_w, conv_b, conv_ln_g, conv_ln_b, sc_w, w_branch, w_gate, b_gate,
           w_out, ln1_g, ln1_b, w_up, w_down, ln2_g, ln2_b, w_router, b_router):
    batch, seq, d = x.shape
    t = batch * seq
    depth = w_in.shape[0]
    x2 = x.reshape(t, d)

    nf = ATT_HEADS
    w_main = jnp.concatenate([w_in[:, :, :2 * WIDTH], w_in[:, :, 3 * WIDTH + nf:]], axis=2).astype(BF16)
    w_vt = jnp.swapaxes(w_in[:, :, 2 * WIDTH:3 * WIDTH], 1, 2).astype(BF16)
    w_f = jnp.pad(w_in[:, :, 3 * WIDTH:3 * WIDTH + nf], ((0, 0), (0, 0), (0, LANES - nf))).astype(BF16)
    b_f = jnp.pad(b_forget, ((0, 0), (0, LANES - nf)))[:, None, :]
    wg = w_gate.astype(BF16)
    wb = w_branch.astype(BF16)
    wo = w_out.astype(BF16)
    wr_t = w_router.T
    wr_hi = wr_t.astype(BF16)
    wr_lo = (wr_t - wr_hi.astype(F32)).astype(BF16)
    wr = jnp.stack([wr_hi, wr_lo])
    br = b_router[:, None]

    def layer(xh, l):
        th = xh.shape[0]
        n_blocks = (2 * th + N_EXPERTS * (EXPERT_BLOCK - 1) + EXPERT_BLOCK - 1) // EXPERT_BLOCK
        n_rows = n_blocks * EXPERT_BLOCK
        qf, kf, vt, u, br_sc = _in_proj(xh, w_main[l], w_vt[l], w_f[l], b_f[l], sc_w[l], seq)
        br_conv = _conv31(u, conv_w[l], conv_b[l][None, :], conv_ln_g[l][None, :],
                          conv_ln_b[l][None, :], seq)
        br_att = _fox_attention(qf, kf, vt, th // seq, seq)
        x1, route, counts = _merge(xh, br_att, br_conv, br_sc, wg[l], b_gate[l][:, None, :], wb[l],
                                   wo[l], ln1_g[l][None, :], ln1_b[l][None, :], wr, br)
        pos, block_expert, n_used, fill_start, fill_len = _routing_tables(route, counts, n_blocks)
        xs = _sc_scatter_rows(x1.reshape(ROW_CHUNKS * th, CHUNK), pos, ROW_CHUNKS * n_rows)
        xs = _fill_unrouted(xs, fill_start, fill_len).reshape(ROW_CHUNKS, n_rows, CHUNK)
        ys = _experts(xs, w_up, w_down, l, block_expert, n_used)
        yk = _sc_gather_rows(ys.reshape(ROW_CHUNKS * n_rows, CHUNK), pos)
        return _combine(x1, route, yk.reshape(2, ROW_CHUNKS, th, CHUNK),
                        ln2_g[l][None, :], ln2_b[l][None, :])

    n_streams = 2 if batch % 2 == 0 else 1
    streams = jnp.split(x2, n_streams, axis=0)
    for l in range(depth):
        streams = [layer(xh, l) for xh in streams]
    return jnp.concatenate(streams, axis=0).reshape(batch, seq, d)
```

```python
import functools

import numpy as np
import jax
import jax.numpy as jnp
from jax import lax
from jax.experimental import pallas as pl
from jax.experimental.pallas import tpu as pltpu
from jax.experimental.pallas import tpu_sc as plsc

F32 = jnp.float32
BF16 = jnp.bfloat16

ATT_HEADS = 8
ATT_HEAD_DIM = 64
WIDTH = 512
CONV_TAPS = 31
SC_TAPS = 3
N_EXPERTS = 32
EXPERTS_PER_GROUP = 4
EXPERT_FF = 512
DEPTH = 4
DEEPNORM_ALPHA = (2 * DEPTH) ** 0.25
LN_EPS = 1e-5

LANES = 128
SUBLANES = 8
VMEM_LIMIT = 56 * 1024 * 1024

TM_PROJ = 512
TM_CONV = 512
CONV_HALO = 32
CONV_CHUNK = 64
TQ = 512
TK = 512
ATT_TILES_PER_BLOCK = 4
EXPERT_BLOCK = 512
TM_ROUTE = 512
SC_WINDOW = 128
ROW_CHUNKS = 2
CHUNK = 256
ROUTE_ROWS = 8
ZERO_ROWS = 256
NEG_BIG = -1e30
LOG2E = 1.4426950408889634


def _sigmoid(x):
    return 1.0 / (1.0 + jnp.exp(-x))


def _log_sigmoid(x):
    return jnp.minimum(x, 0.0) - jnp.log(1.0 + jnp.exp(-jnp.abs(x)))


def _split3(x):
    hi = x.astype(BF16)
    r1 = x - hi.astype(F32)
    mid = r1.astype(BF16)
    lo = (r1 - mid.astype(F32)).astype(BF16)
    return hi, mid, lo


def _pack_rows(x, ref):
    half = x.shape[1] // 2
    bits = pltpu.bitcast(x, jnp.uint32)
    bits = bits + (jnp.uint32(0x7FFF) + ((bits >> 16) & jnp.uint32(1)))
    words = (bits[:, :half] & jnp.uint32(0xFFFF0000)) | (bits[:, half:] >> 16)
    for c in range(ROW_CHUNKS):
        ref[c] = words[:, c * CHUNK:(c + 1) * CHUNK]


def _unpack_rows(ref):
    words = jnp.concatenate([ref[c] for c in range(ROW_CHUNKS)], axis=1)
    hi = pltpu.bitcast(words & jnp.uint32(0xFFFF0000), F32)
    lo = pltpu.bitcast(words << 16, F32)
    return jnp.concatenate([hi, lo], axis=1)


def _layer_norm(y, g, b):
    mu = jnp.mean(y, axis=-1, keepdims=True)
    yc = y - mu
    var = jnp.mean(yc * yc, axis=-1, keepdims=True)
    return yc * lax.rsqrt(var + LN_EPS) * g + b


def _bias_placement():
    pairs = ATT_HEADS // 2
    place = np.zeros((3 * LANES, 2 * pairs * LANES), np.float32)
    ones = np.zeros((1, 2 * pairs * LANES), np.float32)
    for p in range(pairs):
        for h in range(2):
            for piece in range(3):
                src = piece * LANES + 2 * p + h
                place[src, p * LANES + 6 * h + 3 + piece] = 1.0
                place[src, (pairs + p) * LANES + 6 * h + piece] = -1.0
                ones[0, p * LANES + 6 * h + piece] = 1.0
                ones[0, (pairs + p) * LANES + 6 * h + 3 + piece] = 1.0
    return jnp.asarray(place, BF16), jnp.asarray(ones, F32)


def _in_proj_kernel(x_ref, wm_ref, wvt_ref, wf_ref, bf_ref, scw_ref, tri_ref, place_ref, ones_ref,
                    qf_ref, kf_ref, vt_ref, u_ref, brsc_ref,
                    carry_c, carry_p, *, tiles_per_seq):
    i = pl.program_id(0)

    @pl.when(i % tiles_per_seq == 0)
    def _():
        carry_c[...] = jnp.zeros_like(carry_c)
        carry_p[...] = jnp.zeros_like(carry_p)

    tm = x_ref.shape[0]
    xb = x_ref[...].astype(BF16)

    def mm(j):
        return jnp.dot(xb, wm_ref[:, j * WIDTH:(j + 1) * WIDTH], preferred_element_type=F32)

    q = (mm(0) * (ATT_HEAD_DIM ** -0.5 * LOG2E)).astype(BF16)
    k = mm(1).astype(BF16)
    vt_ref[...] = lax.dot_general(wvt_ref[...], xb, (((1,), (1,)), ((), ())),
                                  preferred_element_type=F32).astype(BF16)
    u_ref[...] = mm(2) * _sigmoid(mm(3))

    p = mm(5) * mm(6)
    row = lax.broadcasted_iota(jnp.int32, p.shape, 0)
    prev = carry_p[...]
    p1 = jnp.where(row == 0, prev[7:8, :], pltpu.roll(p, 1, axis=0))
    p2 = pltpu.roll(p, 2, axis=0)
    p2 = jnp.where(row == 0, prev[6:7, :], jnp.where(row == 1, prev[7:8, :], p2))
    y = scw_ref[0:1, :] * p2 + scw_ref[1:2, :] * p1 + scw_ref[2:3, :] * p
    brsc_ref[...] = (mm(4) * y).astype(BF16)
    carry_p[...] = p[tm - SUBLANES:tm, :]

    zf = jnp.dot(xb, wf_ref[...], preferred_element_type=F32)
    ls = _log_sigmoid(zf + bf_ref[...])
    hi, mid, lo = _split3(ls)
    tri = tri_ref[...]
    c = (jnp.dot(tri, hi, preferred_element_type=F32)
         + jnp.dot(tri, mid, preferred_element_type=F32)
         + jnp.dot(tri, lo, preferred_element_type=F32)) + carry_c[...]
    carry_c[...] = c[tm - 1:tm, :]

    bias = ones_ref[...]
    for n, piece in enumerate(_split3(c * LOG2E)):
        bias = bias + jnp.dot(piece, place_ref[n * LANES:(n + 1) * LANES, :],
                              preferred_element_type=F32)
    bias = bias.astype(BF16)
    pairs = ATT_HEADS // 2
    for p in range(pairs):
        lo_, hi_ = p * LANES, (p + 1) * LANES
        qf_ref[:, 2 * lo_:2 * lo_ + LANES] = q[:, lo_:hi_]
        qf_ref[:, 2 * lo_ + LANES:2 * hi_] = bias[:, lo_:hi_]
        kf_ref[:, 2 * lo_:2 * lo_ + LANES] = k[:, lo_:hi_]
        kf_ref[:, 2 * lo_ + LANES:2 * hi_] = bias[:, (pairs + p) * LANES:(pairs + p + 1) * LANES]


def _in_proj(x2, wm, wvt, wf, bf, scw, seq):
    t, d = x2.shape
    tm = min(TM_PROJ, seq)
    assert seq % tm == 0 and t % seq == 0
    tri = jnp.tril(jnp.ones((tm, tm), F32)).astype(BF16)
    place, ones = _bias_placement()
    const = lambda shape: pl.BlockSpec(shape, lambda i: (0,) * len(shape))
    rows = lambda w: pl.BlockSpec((tm, w), lambda i: (i, 0))
    return pl.pallas_call(
        functools.partial(_in_proj_kernel, tiles_per_seq=seq // tm),
        grid=(t // tm,),
        in_specs=[rows(d), const(wm.shape), const(wvt.shape), const(wf.shape), const(bf.shape),
                  const(scw.shape), const(tri.shape), const(place.shape), const(ones.shape)],
        out_specs=[rows(2 * WIDTH), rows(2 * WIDTH), pl.BlockSpec((WIDTH, tm), lambda i: (0, i)),
                   rows(WIDTH), rows(WIDTH)],
        out_shape=[jax.ShapeDtypeStruct((t, 2 * WIDTH), BF16),
                   jax.ShapeDtypeStruct((t, 2 * WIDTH), BF16),
                   jax.ShapeDtypeStruct((WIDTH, t), BF16),
                   jax.ShapeDtypeStruct((t, WIDTH), F32),
                   jax.ShapeDtypeStruct((t, WIDTH), BF16)],
        scratch_shapes=[pltpu.VMEM((1, LANES), F32), pltpu.VMEM((SUBLANES, WIDTH), F32)],
        compiler_params=pltpu.CompilerParams(dimension_semantics=("arbitrary",),
                                             vmem_limit_bytes=VMEM_LIMIT),
        name="in_proj",
    )(x2, wm, wvt, wf, bf, scw, tri, place, ones)


def _conv31_kernel(u_ref, w_ref, cb_ref, g_ref, b_ref, o_ref, ext, pre, *, tiles_per_seq):
    i = pl.program_id(0)
    tm = u_ref.shape[0]

    @pl.when(i % tiles_per_seq == 0)
    def _():
        ext[0:CONV_HALO, :] = jnp.zeros((CONV_HALO, WIDTH), F32)
        ext[CONV_HALO + tm:CONV_HALO + tm + SUBLANES, :] = jnp.zeros((SUBLANES, WIDTH), F32)

    ext[CONV_HALO:CONV_HALO + tm, :] = u_ref[...]
    base = CONV_HALO - (CONV_TAPS - 1)
    span = CONV_CHUNK + SUBLANES
    half = WIDTH // 2
    for c0 in range(0, tm, CONV_CHUNK):
        for l0 in (0, half):
            y = None
            for r in range(SUBLANES):
                z = None
                for a in range((base + CONV_TAPS - 1) // SUBLANES + 1):
                    k = SUBLANES * a + r - base
                    if 0 <= k < CONV_TAPS:
                        term = (w_ref[k:k + 1, l0:l0 + half]
                                * ext[c0 + SUBLANES * a:c0 + SUBLANES * a + span, l0:l0 + half])
                        z = term if z is None else z + term
                zr = z[r:r + CONV_CHUNK, :]
                y = zr if y is None else y + zr
            pre[c0:c0 + CONV_CHUNK, l0:l0 + half] = y
    for c0 in range(0, tm, CONV_CHUNK):
        y = _layer_norm(pre[c0:c0 + CONV_CHUNK, :] + cb_ref[...], g_ref[...], b_ref[...])
        o_ref[c0:c0 + CONV_CHUNK, :] = (y * _sigmoid(y)).astype(BF16)
    ext[0:CONV_HALO, :] = ext[tm:tm + CONV_HALO, :]


def _conv31(u, w, cb, g, b, seq):
    t = u.shape[0]
    tm = min(TM_CONV, seq)
    assert seq % tm == 0 and tm % CONV_CHUNK == 0
    const = lambda shape: pl.BlockSpec(shape, lambda i: (0,) * len(shape))
    return pl.pallas_call(
        functools.partial(_conv31_kernel, tiles_per_seq=seq // tm),
        grid=(t // tm,),
        in_specs=[pl.BlockSpec((tm, WIDTH), lambda i: (i, 0)), const(w.shape), const(cb.shape),
                  const(g.shape), const(b.shape)],
        out_specs=pl.BlockSpec((tm, WIDTH), lambda i: (i, 0)),
        out_shape=jax.ShapeDtypeStruct((t, WIDTH), BF16),
        scratch_shapes=[pltpu.VMEM((CONV_HALO + tm + SUBLANES, WIDTH), F32),
                        pltpu.VMEM((tm, WIDTH), F32)],
        compiler_params=pltpu.CompilerParams(dimension_semantics=("arbitrary",),
                                             vmem_limit_bytes=VMEM_LIMIT),
        name="conv31",
    )(u, w, cb, g, b)


def _fox_kernel(qf_ref, kf_ref, vt_ref, o_ref, acc0_ref, acc1_ref, *, tk):
    qi = pl.program_id(2)
    tq = qf_ref.shape[0]
    lane = lax.broadcasted_iota(jnp.int32, (tq, 2 * LANES), 1)
    bias_lane = lane - LANES
    qfull = qf_ref[...]
    zero = jnp.zeros_like(qfull)
    qs = [jnp.where(((lane >= h * ATT_HEAD_DIM) & (lane < (h + 1) * ATT_HEAD_DIM))
                    | ((bias_lane >= 6 * h) & (bias_lane < 6 * h + 6)), qfull, zero)
          for h in range(2)]
    q_lo = qi * tq
    accs = (acc0_ref, acc1_ref)
    for acc in accs:
        acc[...] = jnp.zeros(acc.shape, F32)

    def tiles(jobs, carry):
        work = []
        for j, masked in jobs:
            start = pl.multiple_of(j * tk, tk)
            kfull = kf_ref[pl.ds(start, tk), :]
            ss = [lax.dot_general(kfull, qs[h], (((1,), (1,)), ((), ())),
                                  preferred_element_type=F32) for h in range(2)]
            work.append((start, masked, ss))
        carry = list(carry)
        for start, masked, ss in work:
            vt = vt_ref[:, pl.ds(start, tk)]
            if masked:
                key = start + lax.broadcasted_iota(jnp.int32, ss[0].shape, 0)
                qry = q_lo + lax.broadcasted_iota(jnp.int32, ss[0].shape, 1)
                ss = [jnp.where(key <= qry, s, NEG_BIG) for s in ss]
            for h in range(2):
                m_old, l_old = carry[h]
                m_new = jnp.maximum(m_old, jnp.max(ss[h], axis=0, keepdims=True))
                a = jnp.exp2(m_old - m_new)
                p = jnp.exp2(ss[h] - m_new)
                l_new = a * l_old + jnp.sum(p.reshape(tk // SUBLANES, SUBLANES, tq), axis=0)
                accs[h][...] = a * accs[h][...] + jnp.dot(vt, p.astype(BF16),
                                                          preferred_element_type=F32)
                carry[h] = (m_new, l_new)
        return tuple(carry)

    init = tuple((jnp.full((1, tq), NEG_BIG, F32), jnp.zeros((SUBLANES, tq), F32)) for _ in range(2))
    n_full = q_lo // tk
    group = ATT_TILES_PER_BLOCK
    carry = lax.fori_loop(
        0, n_full // group,
        lambda j, c: tiles([(group * j + i, False) for i in range(group)], c), init)
    rest = n_full % group

    def tail(n):
        return lambda c: tiles([(n_full - n + i, False) for i in range(n)] + [(n_full, True)], c)

    (_, l0), (_, l1) = lax.switch(rest, [tail(n) for n in range(group)], carry)
    row = lax.broadcasted_iota(jnp.int32, (LANES, tq), 0)
    out_t = jnp.where(row < ATT_HEAD_DIM,
                      acc0_ref[...] / jnp.sum(l0, axis=0, keepdims=True),
                      acc1_ref[...] / jnp.sum(l1, axis=0, keepdims=True))
    o_ref[...] = out_t.T.astype(BF16)


def _fox_attention(qf, kf, vt, batch, seq):
    t = qf.shape[0]
    tq = min(TQ, seq)
    tk = min(TK, seq)
    assert seq % tq == 0 and seq % tk == 0 and tk % tq == 0
    nq = seq // tq
    pairs = ATT_HEADS // 2
    return pl.pallas_call(
        functools.partial(_fox_kernel, tk=tk),
        grid=(batch, pairs, nq),
        in_specs=[pl.BlockSpec((tq, 2 * LANES), lambda b, hp, qi: (b * nq + qi, hp)),
                  pl.BlockSpec((seq, 2 * LANES), lambda b, hp, qi: (b, hp)),
                  pl.BlockSpec((LANES, seq), lambda b, hp, qi: (hp, b))],
        out_specs=pl.BlockSpec((tq, LANES), lambda b, hp, qi: (b * nq + qi, hp)),
        out_shape=jax.ShapeDtypeStruct((t, WIDTH), BF16),
        scratch_shapes=[pltpu.VMEM((LANES, tq), F32), pltpu.VMEM((LANES, tq), F32)],
        compiler_params=pltpu.CompilerParams(
            dimension_semantics=("arbitrary", "arbitrary", "arbitrary"),
            vmem_limit_bytes=VMEM_LIMIT),
        name="fox_attn",
    )(qf, kf, vt)


def _partner(x, row, bit):
    n = x.shape[0]
    up = pltpu.roll(x, n - bit, axis=0)
    down = pltpu.roll(x, bit, axis=0)
    return jnp.where((row & bit) == 0, up, down)


def _merge_kernel(x_ref, att_ref, conv_ref, sc_ref, wg_ref, bg_ref, wb_ref, wo_ref,
                  g1_ref, b1_ref, wr_ref, br_ref, triu_ref,
                  x1_ref, x1p_ref, route_ref, counts_ref, carry):
    i = pl.program_id(0)

    @pl.when(i == 0)
    def _():
        carry[...] = jnp.zeros_like(carry)

    x = x_ref[...]
    xb = x.astype(BF16)
    merged = None
    for n, br_ref_n in enumerate((att_ref, conv_ref, sc_ref)):
        gate = _sigmoid(jnp.dot(xb, wg_ref[n], preferred_element_type=F32) + bg_ref[n])
        term = gate * jnp.dot(br_ref_n[...], wb_ref[n], preferred_element_type=F32)
        merged = term if merged is None else merged + term
    h = jnp.dot(merged.astype(BF16), wo_ref[...], preferred_element_type=F32)
    x1 = _layer_norm(DEEPNORM_ALPHA * x + h, g1_ref[...], b1_ref[...])
    x1_ref[...] = x1
    _pack_rows(x1, x1p_ref)

    x_hi = x1.astype(BF16)
    x_lo = (x1 - x_hi.astype(F32)).astype(BF16)
    nt = lambda w, v: lax.dot_general(w, v, (((1,), (1,)), ((), ())), preferred_element_type=F32)
    logits = nt(wr_ref[0], x_hi) + nt(wr_ref[0], x_lo) + nt(wr_ref[1], x_hi) + br_ref[...]
    mx = jnp.max(logits, axis=0, keepdims=True)
    ex = jnp.exp(logits - mx)
    p = ex / jnp.sum(ex, axis=0, keepdims=True)

    row = lax.broadcasted_iota(jnp.int32, p.shape, 0)
    sub = row & (EXPERTS_PER_GROUP - 1)
    rank = jnp.zeros(p.shape, jnp.int32)
    for d in range(1, EXPERTS_PER_GROUP):
        below = pltpu.roll(p, d, axis=0)
        above = pltpu.roll(p, N_EXPERTS - d, axis=0)
        rank = rank + jnp.where((sub >= d) & (below >= p), 1, 0)
        rank = rank + jnp.where((sub + d < EXPERTS_PER_GROUP) & (above > p), 1, 0)
    top2 = rank < 2
    score = jnp.where(top2, p, 0.0)
    score = score + _partner(score, row, 1)
    score = score + _partner(score, row, 2)
    best = jnp.max(score, axis=0, keepdims=True)
    group = row >> 2
    best_group = jnp.min(jnp.where(score == best, group, N_EXPERTS), axis=0, keepdims=True)
    sel = top2 & (group == best_group)
    gate = jnp.where(sel, p / best, 0.0)

    sel_b = jnp.where(sel, 1.0, 0.0).astype(BF16)
    before = jnp.dot(sel_b, triu_ref[...], preferred_element_type=F32) + carry[...]
    tm = x.shape[0]
    new_carry = before[:, tm - 1:tm] + sel_b[:, tm - 1:tm].astype(F32)
    carry[...] = new_carry
    counts_ref[...] = new_carry

    e_lo = jnp.min(jnp.where(sel, row, N_EXPERTS), axis=0, keepdims=True)
    e_hi = jnp.max(jnp.where(sel, row, -1), axis=0, keepdims=True)
    is_lo = row == e_lo
    is_hi = row == e_hi
    pick = lambda m, a: jnp.sum(jnp.where(m, a, 0.0), axis=0, keepdims=True)
    zero = jnp.zeros_like(mx)
    route_ref[...] = jnp.concatenate(
        [e_lo.astype(F32), e_hi.astype(F32), pick(is_lo, before), pick(is_hi, before),
         pick(is_lo, gate), pick(is_hi, gate), zero, zero], axis=0)


def _merge(x2, att, conv, sc, wg, bg, wb, wo, g1, b1, wr, br):
    t, d = x2.shape
    tm = min(TM_PROJ, t)
    assert t % tm == 0
    triu = jnp.triu(jnp.ones((tm, tm), F32), 1).astype(BF16)
    const = lambda shape: pl.BlockSpec(shape, lambda i: (0,) * len(shape),
                                       pipeline_mode=pl.Buffered(1))
    rows = lambda w: pl.BlockSpec((tm, w), lambda i: (i, 0))
    return pl.pallas_call(
        _merge_kernel,
        grid=(t // tm,),
        in_specs=[rows(d), rows(WIDTH), rows(WIDTH), rows(WIDTH),
                  const(wg.shape), const(bg.shape), const(wb.shape), const(wo.shape),
                  const(g1.shape), const(b1.shape), const(wr.shape), const(br.shape),
                  const(triu.shape)],
        out_specs=[rows(d), pl.BlockSpec((ROW_CHUNKS, tm, CHUNK), lambda i: (0, i, 0)),
                   pl.BlockSpec((ROUTE_ROWS, tm), lambda i: (0, i)),
                   pl.BlockSpec((N_EXPERTS, 1), lambda i: (0, 0))],
        out_shape=[jax.ShapeDtypeStruct((t, d), F32),
                   jax.ShapeDtypeStruct((ROW_CHUNKS, t, CHUNK), jnp.uint32),
                   jax.ShapeDtypeStruct((ROUTE_ROWS, t), F32),
                   jax.ShapeDtypeStruct((N_EXPERTS, 1), F32)],
        scratch_shapes=[pltpu.VMEM((N_EXPERTS, 1), F32)],
        compiler_params=pltpu.CompilerParams(dimension_semantics=("arbitrary",),
                                             vmem_limit_bytes=VMEM_LIMIT),
        name="merge",
    )(x2, att, conv, sc, wg, bg, wb, wo, g1, b1, wr, br, triu)


def _sc_mesh():
    return plsc.VectorSubcoreMesh(core_axis_name="core", subcore_axis_name="subcore")


def _sc_scatter_rows(x, idx, n_rows):
    t, d = x.shape
    m = idx.shape[0]
    w = SC_WINDOW
    assert t % w == 0 and m % t == 0
    steps_per_pass = t // w

    @functools.partial(pl.kernel, out_type=jax.ShapeDtypeStruct((n_rows, d), x.dtype),
                       mesh=_sc_mesh(), scratch_types=[], name="sc_dispatch")
    def run(x_hbm, i_hbm, o_hbm):
        def body(x_vmem, i_vmem):
            pltpu.sync_copy(x_vmem, o_hbm.at[i_vmem.at[0]])

        pltpu.emit_pipeline(
            body,
            grid=(m // w,),
            in_specs=[pl.BlockSpec((w, d), index_map=lambda i: (i % steps_per_pass, 0)),
                      pl.BlockSpec((1, w), index_map=lambda i: (0, i))],
            out_specs=[],
            core_axis_name=("core", "subcore"),
            dimension_semantics=(pltpu.PARALLEL,),
            trace_scopes=False,
        )(x_hbm, i_hbm)

    return run(x, idx.reshape(1, m))


def _sc_gather_rows(table, idx):
    d = table.shape[1]
    m = idx.shape[0]
    w = SC_WINDOW
    assert m % w == 0

    @functools.partial(pl.kernel, out_type=jax.ShapeDtypeStruct((m, d), table.dtype),
                       mesh=_sc_mesh(), scratch_types=[], name="sc_gather")
    def run(t_hbm, i_hbm, o_hbm):
        def body(i_vmem, o_vmem):
            pltpu.sync_copy(t_hbm.at[i_vmem.at[0]], o_vmem)

        pltpu.emit_pipeline(
            body,
            grid=(m // w,),
            in_specs=[pl.BlockSpec((1, w), index_map=lambda i: (0, i))],
            out_specs=[pl.BlockSpec((w, d), index_map=lambda i: (i, 0))],
            core_axis_name=("core", "subcore"),
            dimension_semantics=(pltpu.PARALLEL,),
            trace_scopes=False,
        )(i_hbm, o_hbm)

    return run(table, idx.reshape(1, m))


def _fill_kernel(fs_ref, fl_ref, xs_in, xs_ref, zeros, sem):
    del xs_in
    zb = zeros.shape[0]
    zeros[...] = jnp.zeros_like(zeros)

    n_regions = fs_ref.shape[0]
    rows_per_chunk = xs_ref.shape[0] // ROW_CHUNKS

    def region(e, fn):
        start = fs_ref[e % n_regions] + (e // n_regions) * rows_per_chunk
        length = fl_ref[e % n_regions]
        head = jnp.minimum(length, (SUBLANES - start % SUBLANES) % SUBLANES)
        mid = start + head
        mid_len = (length - head) // SUBLANES * SUBLANES
        n_full = mid_len // zb
        rem = mid_len - n_full * zb
        part = mid + n_full * zb
        tail = mid + mid_len
        n_tail = length - head - mid_len

        def single(base):
            def body(r, _):
                fn(pltpu.make_async_copy(zeros.at[pl.ds(0, 1)],
                                         xs_ref.at[pl.ds(base + r, 1)], sem))
                return 0
            return body
        lax.fori_loop(0, head, single(start), 0)
        lax.fori_loop(0, n_tail, single(tail), 0)

        def full(c, _):
            off = pl.multiple_of(mid + c * zb, SUBLANES)
            fn(pltpu.make_async_copy(zeros, xs_ref.at[pl.ds(off, zb)], sem))
            return 0
        lax.fori_loop(0, n_full, full, 0)
        bit = zb // 2
        while bit >= SUBLANES:
            off = pl.multiple_of(part + (rem - rem % (2 * bit)), SUBLANES)

            @pl.when((rem & bit) != 0)
            def _(bit=bit, off=off):
                fn(pltpu.make_async_copy(zeros.at[pl.ds(0, bit)],
                                         xs_ref.at[pl.ds(off, bit)], sem))
            bit //= 2

    for fn in (lambda cp: cp.start(), lambda cp: cp.wait()):
        def body(e, _, fn=fn):
            region(e, fn)
            return 0
        lax.fori_loop(0, n_regions * ROW_CHUNKS, body, 0)


def _fill_unrouted(xs, fill_start, fill_len):
    n_rows, d = xs.shape
    return pl.pallas_call(
        _fill_kernel,
        grid_spec=pltpu.PrefetchScalarGridSpec(
            num_scalar_prefetch=2,
            grid=(1,),
            in_specs=[pl.BlockSpec(memory_space=pl.ANY)],
            out_specs=pl.BlockSpec(memory_space=pl.ANY),
            scratch_shapes=[pltpu.VMEM((ZERO_ROWS, d), xs.dtype), pltpu.SemaphoreType.DMA]),
        out_shape=jax.ShapeDtypeStruct((n_rows, d), xs.dtype),
        input_output_aliases={2: 0},
        compiler_params=pltpu.CompilerParams(dimension_semantics=("arbitrary",),
                                             vmem_limit_bytes=VMEM_LIMIT),
        name="fill_unrouted",
    )(fill_start, fill_len, xs)


def _expert_kernel(be_ref, nb_ref, xs_ref, wu_ref, wd_ref, ys_ref, wu_bf, wd_bf):
    b = pl.program_id(0)
    live = b < nb_ref[0]

    @pl.when(live & ((b == 0) | (be_ref[b] != be_ref[jnp.maximum(b, 1) - 1])))
    def _():
        wu_bf[...] = wu_ref[0, 0].astype(BF16)
        wd_bf[...] = wd_ref[0, 0].astype(BF16)

    @pl.when(live)
    def _():
        xb = _unpack_rows(xs_ref).astype(BF16)
        hidden = jnp.dot(xb, wu_bf[...], preferred_element_type=F32)
        a = hidden[:, :EXPERT_FF]
        g = hidden[:, EXPERT_FF:]
        act = (g * _sigmoid(g) * a).astype(BF16)
        _pack_rows(jnp.dot(act, wd_bf[...], preferred_element_type=F32), ys_ref)

    @pl.when(jnp.logical_not(live))
    def _():
        ys_ref[...] = jnp.zeros_like(ys_ref)


def _experts(xs, w_up, w_down, layer, block_expert, n_used):
    _, n_rows, d = xs.shape
    n_blocks = n_rows // EXPERT_BLOCK

    def live(b, nb):
        return jnp.minimum(b, nb[0] - 1)

    wspec = lambda w: pl.BlockSpec((1, 1) + w.shape[2:],
                                   lambda b, be, nb: (layer, be[live(b, nb)], 0, 0))
    return pl.pallas_call(
        _expert_kernel,
        grid_spec=pltpu.PrefetchScalarGridSpec(
            num_scalar_prefetch=2,
            grid=(n_blocks,),
            in_specs=[pl.BlockSpec((ROW_CHUNKS, EXPERT_BLOCK, d), lambda b, be, nb: (0, live(b, nb), 0)),
                      wspec(w_up), wspec(w_down)],
            out_specs=pl.BlockSpec((ROW_CHUNKS, EXPERT_BLOCK, d), lambda b, be, nb: (0, b, 0)),
            scratch_shapes=[pltpu.VMEM(w_up.shape[2:], BF16), pltpu.VMEM(w_down.shape[2:], BF16)]),
        out_shape=jax.ShapeDtypeStruct(xs.shape, xs.dtype),
        compiler_params=pltpu.CompilerParams(dimension_semantics=("arbitrary",),
                                             vmem_limit_bytes=VMEM_LIMIT),
        name="experts",
    )(block_expert, n_used, xs, w_up, w_down)


def _combine_kernel(x1_ref, route_ref, y0_ref, y1_ref, g2_ref, b2_ref, o_ref):
    route = route_ref[...].T
    m = route[:, 4:5] * _unpack_rows(y0_ref.at[0]) + route[:, 5:6] * _unpack_rows(y1_ref.at[0])
    o_ref[...] = _layer_norm(DEEPNORM_ALPHA * x1_ref[...] + m, g2_ref[...], b2_ref[...])


def _combine(x1, route, yk, g2, b2):
    t, d = x1.shape
    dc = yk.shape[-1]
    tm = min(TM_ROUTE, t)
    steps = t // tm
    const = lambda shape: pl.BlockSpec(shape, lambda i: (0,) * len(shape))
    return pl.pallas_call(
        _combine_kernel,
        grid=(steps,),
        in_specs=[pl.BlockSpec((tm, d), lambda i: (i, 0)),
                  pl.BlockSpec((ROUTE_ROWS, tm), lambda i: (0, i)),
                  pl.BlockSpec((1, ROW_CHUNKS, tm, dc), lambda i: (0, 0, i, 0)),
                  pl.BlockSpec((1, ROW_CHUNKS, tm, dc), lambda i: (1, 0, i, 0)),
                  const(g2.shape), const(b2.shape)],
        out_specs=pl.BlockSpec((tm, d), lambda i: (i, 0)),
        out_shape=jax.ShapeDtypeStruct((t, d), F32),
        compiler_params=pltpu.CompilerParams(dimension_semantics=("arbitrary",),
                                             vmem_limit_bytes=VMEM_LIMIT),
        name="combine",
    )(x1, route, yk, yk, g2, b2)


def _routing_tables(route, counts, n_blocks):
    t = route.shape[1]
    cnt = counts[:, 0].astype(jnp.int32)
    padded = (cnt + EXPERT_BLOCK - 1) // EXPERT_BLOCK * EXPERT_BLOCK
    pad_end = jnp.cumsum(padded)
    pad_start = pad_end - padded
    experts = route[0:2].astype(jnp.int32)
    ranks = route[2:4].astype(jnp.int32)
    ids = jnp.arange(N_EXPERTS, dtype=jnp.int32)
    base = jnp.sum(jnp.where(experts[:, :, None] == ids, pad_start, 0), axis=-1)
    n_rows = n_blocks * EXPERT_BLOCK
    chunk_base = jnp.arange(ROW_CHUNKS, dtype=jnp.int32) * n_rows
    pos = ((base + ranks)[:, None, :] + chunk_base[None, :, None]).reshape(2 * ROW_CHUNKS * t)
    blk = jnp.arange(n_blocks, dtype=jnp.int32) * EXPERT_BLOCK
    block_expert = jnp.minimum(jnp.sum((pad_end[None, :] <= blk[:, None]).astype(jnp.int32), axis=1),
                               N_EXPERTS - 1)
    n_used = (pad_end[-1:] // EXPERT_BLOCK).astype(jnp.int32)
    fill_start = jnp.concatenate([pad_start + cnt, pad_end[-1:]]).astype(jnp.int32)
    fill_len = jnp.concatenate([padded - cnt, n_rows - pad_end[-1:]]).astype(jnp.int32)
    return pos, block_expert, n_used, fill_start, fill_len


def kernel(x, w_in, b_forget, conv_w, conv_b, conv_ln_g, conv_ln_b, sc_w, w_branch, w_gate, b_gate,
           w_out, ln1_g, ln1_b, w_up, w_down, ln2_g, ln2_b, w_router, b_router):
    batch, seq, d = x.shape
    t = batch * seq
    depth = w_in.shape[0]
    x2 = x.reshape(t, d)

    nf = ATT_HEADS
    w_main = jnp.concatenate([w_in[:, :, :2 * WIDTH], w_in[:, :, 3 * WIDTH + nf:]], axis=2).astype(BF16)
    w_vt = jnp.swapaxes(w_in[:, :, 2 * WIDTH:3 * WIDTH], 1, 2).astype(BF16)
    w_f = jnp.pad(w_in[:, :, 3 * WIDTH:3 * WIDTH + nf], ((0, 0), (0, 0), (0, LANES - nf))).astype(BF16)
    b_f = jnp.pad(b_forget, ((0, 0), (0, LANES - nf)))[:, None, :]
    wg = w_gate.astype(BF16)
    wb = w_branch.astype(BF16)
    wo = w_out.astype(BF16)
    wr_t = w_router.T
    wr_hi = wr_t.astype(BF16)
    wr_lo = (wr_t - wr_hi.astype(F32)).astype(BF16)
    wr = jnp.stack([wr_hi, wr_lo])
    br = b_router[:, None]

    def layer(xh, l):
        th = xh.shape[0]
        n_blocks = (2 * th + N_EXPERTS * (EXPERT_BLOCK - 1) + EXPERT_BLOCK - 1) // EXPERT_BLOCK
        n_rows = n_blocks * EXPERT_BLOCK
        qf, kf, vt, u, br_sc = _in_proj(xh, w_main[l], w_vt[l], w_f[l], b_f[l], sc_w[l], seq)
        br_conv = _conv31(u, conv_w[l], conv_b[l][None, :], conv_ln_g[l][None, :],
                          conv_ln_b[l][None, :], seq)
        br_att = _fox_attention(qf, kf, vt, th // seq, seq)
        x1, x1p, route, counts = _merge(xh, br_att, br_conv, br_sc, wg[l], b_gate[l][:, None, :],
                                        wb[l], wo[l], ln1_g[l][None, :], ln1_b[l][None, :], wr, br)
        pos, block_expert, n_used, fill_start, fill_len = _routing_tables(route, counts, n_blocks)
        xs = _sc_scatter_rows(x1p.reshape(ROW_CHUNKS * th, CHUNK), pos, ROW_CHUNKS * n_rows)
        xs = _fill_unrouted(xs, fill_start, fill_len).reshape(ROW_CHUNKS, n_rows, CHUNK)
        ys = _experts(xs, w_up, w_down, l, block_expert, n_used)
        yk = _sc_gather_rows(ys.reshape(ROW_CHUNKS * n_rows, CHUNK), pos)
        return _combine(x1, route, yk.reshape(2, ROW_CHUNKS, th, CHUNK),
                        ln2_g[l][None, :], ln2_b[l][None, :])

    for l in range(depth):
        x2 = layer(x2, l)
    return x2.reshape(batch, seq, d)
```

```python
import functools

import numpy as np
import jax
import jax.numpy as jnp
from jax import lax
from jax.experimental import pallas as pl
from jax.experimental.pallas import tpu as pltpu
from jax.experimental.pallas import tpu_sc as plsc

F32 = jnp.float32
BF16 = jnp.bfloat16

ATT_HEADS = 8
ATT_HEAD_DIM = 64
WIDTH = 512
CONV_TAPS = 31
SC_TAPS = 3
N_EXPERTS = 32
EXPERTS_PER_GROUP = 4
EXPERT_FF = 512
DEPTH = 4
DEEPNORM_ALPHA = (2 * DEPTH) ** 0.25
LN_EPS = 1e-5

LANES = 128
SUBLANES = 8
VMEM_LIMIT = 56 * 1024 * 1024

TM_PROJ = 512
TM_CONV = 512
CONV_HALO = 32
CONV_CHUNK = 64
TQ = 512
TK = 512
ATT_TILES_PER_BLOCK = 4
EXPERT_BLOCK = 512
TM_ROUTE = 512
SC_WINDOW = 128
ROW_CHUNKS = 2
CHUNK = 256
ROUTE_ROWS = 8
ZERO_ROWS = 256
NEG_BIG = -1e30
LOG2E = 1.4426950408889634


def _sigmoid(x):
    return 1.0 / (1.0 + jnp.exp(-x))


def _log_sigmoid(x):
    return jnp.minimum(x, 0.0) - jnp.log(1.0 + jnp.exp(-jnp.abs(x)))


def _split3(x):
    hi = x.astype(BF16)
    r1 = x - hi.astype(F32)
    mid = r1.astype(BF16)
    lo = (r1 - mid.astype(F32)).astype(BF16)
    return hi, mid, lo


def _pack_rows(x, ref):
    half = x.shape[1] // 2
    bits = pltpu.bitcast(x, jnp.uint32)
    bits = bits + (jnp.uint32(0x7FFF) + ((bits >> 16) & jnp.uint32(1)))
    words = (bits[:, :half] & jnp.uint32(0xFFFF0000)) | (bits[:, half:] >> 16)
    for c in range(ROW_CHUNKS):
        ref[c] = words[:, c * CHUNK:(c + 1) * CHUNK]


def _unpack_rows(ref):
    words = jnp.concatenate([ref[c] for c in range(ROW_CHUNKS)], axis=1)
    hi = pltpu.bitcast(words & jnp.uint32(0xFFFF0000), F32)
    lo = pltpu.bitcast(words << 16, F32)
    return jnp.concatenate([hi, lo], axis=1)


def _layer_norm(y, g, b):
    mu = jnp.mean(y, axis=-1, keepdims=True)
    yc = y - mu
    var = jnp.mean(yc * yc, axis=-1, keepdims=True)
    return yc * lax.rsqrt(var + LN_EPS) * g + b


def _bias_placement():
    pairs = ATT_HEADS // 2
    place = np.zeros((3 * LANES, 2 * pairs * LANES), np.float32)
    ones = np.zeros((1, 2 * pairs * LANES), np.float32)
    for p in range(pairs):
        for h in range(2):
            for piece in range(3):
                src = piece * LANES + 2 * p + h
                place[src, p * LANES + 6 * h + 3 + piece] = 1.0
                place[src, (pairs + p) * LANES + 6 * h + piece] = -1.0
                ones[0, p * LANES + 6 * h + piece] = 1.0
                ones[0, (pairs + p) * LANES + 6 * h + 3 + piece] = 1.0
    return jnp.asarray(place, BF16), jnp.asarray(ones, F32)


def _in_proj_kernel(x_ref, wm_ref, wvt_ref, wf_ref, bf_ref, scw_ref, tri_ref, place_ref, ones_ref,
                    qf_ref, kf_ref, vt_ref, u_ref, brsc_ref,
                    carry_c, carry_p, *, tiles_per_seq):
    i = pl.program_id(0)

    @pl.when(i % tiles_per_seq == 0)
    def _():
        carry_c[...] = jnp.zeros_like(carry_c)
        carry_p[...] = jnp.zeros_like(carry_p)

    tm = x_ref.shape[0]
    xb = x_ref[...].astype(BF16)

    def mm(j):
        return jnp.dot(xb, wm_ref[:, j * WIDTH:(j + 1) * WIDTH], preferred_element_type=F32)

    q = (mm(0) * (ATT_HEAD_DIM ** -0.5 * LOG2E)).astype(BF16)
    k = mm(1).astype(BF16)
    vt_ref[...] = lax.dot_general(wvt_ref[...], xb, (((1,), (1,)), ((), ())),
                                  preferred_element_type=F32).astype(BF16)
    u_ref[...] = mm(2) * _sigmoid(mm(3))

    p = mm(5) * mm(6)
    row = lax.broadcasted_iota(jnp.int32, p.shape, 0)
    prev = carry_p[...]
    p1 = jnp.where(row == 0, prev[7:8, :], pltpu.roll(p, 1, axis=0))
    p2 = pltpu.roll(p, 2, axis=0)
    p2 = jnp.where(row == 0, prev[6:7, :], jnp.where(row == 1, prev[7:8, :], p2))
    y = scw_ref[0:1, :] * p2 + scw_ref[1:2, :] * p1 + scw_ref[2:3, :] * p
    brsc_ref[...] = (mm(4) * y).astype(BF16)
    carry_p[...] = p[tm - SUBLANES:tm, :]

    zf = jnp.dot(xb, wf_ref[...], preferred_element_type=F32)
    ls = _log_sigmoid(zf + bf_ref[...])
    hi, mid, lo = _split3(ls)
    tri = tri_ref[...]
    c = (jnp.dot(tri, hi, preferred_element_type=F32)
         + jnp.dot(tri, mid, preferred_element_type=F32)
         + jnp.dot(tri, lo, preferred_element_type=F32)) + carry_c[...]
    carry_c[...] = c[tm - 1:tm, :]

    bias = ones_ref[...]
    for n, piece in enumerate(_split3(c * LOG2E)):
        bias = bias + jnp.dot(piece, place_ref[n * LANES:(n + 1) * LANES, :],
                              preferred_element_type=F32)
    bias = bias.astype(BF16)
    pairs = ATT_HEADS // 2
    for p in range(pairs):
        lo_, hi_ = p * LANES, (p + 1) * LANES
        qf_ref[:, 2 * lo_:2 * lo_ + LANES] = q[:, lo_:hi_]
        qf_ref[:, 2 * lo_ + LANES:2 * hi_] = bias[:, lo_:hi_]
        kf_ref[:, 2 * lo_:2 * lo_ + LANES] = k[:, lo_:hi_]
        kf_ref[:, 2 * lo_ + LANES:2 * hi_] = bias[:, (pairs + p) * LANES:(pairs + p + 1) * LANES]


def _in_proj(x2, wm, wvt, wf, bf, scw, seq):
    t, d = x2.shape
    tm = min(TM_PROJ, seq)
    assert seq % tm == 0 and t % seq == 0
    tri = jnp.tril(jnp.ones((tm, tm), F32)).astype(BF16)
    place, ones = _bias_placement()
    const = lambda shape: pl.BlockSpec(shape, lambda i: (0,) * len(shape))
    rows = lambda w: pl.BlockSpec((tm, w), lambda i: (i, 0))
    return pl.pallas_call(
        functools.partial(_in_proj_kernel, tiles_per_seq=seq // tm),
        grid=(t // tm,),
        in_specs=[rows(d), const(wm.shape), const(wvt.shape), const(wf.shape), const(bf.shape),
                  const(scw.shape), const(tri.shape), const(place.shape), const(ones.shape)],
        out_specs=[rows(2 * WIDTH), rows(2 * WIDTH), pl.BlockSpec((WIDTH, tm), lambda i: (0, i)),
                   rows(WIDTH), rows(WIDTH)],
        out_shape=[jax.ShapeDtypeStruct((t, 2 * WIDTH), BF16),
                   jax.ShapeDtypeStruct((t, 2 * WIDTH), BF16),
                   jax.ShapeDtypeStruct((WIDTH, t), BF16),
                   jax.ShapeDtypeStruct((t, WIDTH), F32),
                   jax.ShapeDtypeStruct((t, WIDTH), BF16)],
        scratch_shapes=[pltpu.VMEM((1, LANES), F32), pltpu.VMEM((SUBLANES, WIDTH), F32)],
        compiler_params=pltpu.CompilerParams(dimension_semantics=("arbitrary",),
                                             vmem_limit_bytes=VMEM_LIMIT),
        name="in_proj",
    )(x2, wm, wvt, wf, bf, scw, tri, place, ones)


def _conv31_kernel(u_ref, w_ref, cb_ref, g_ref, b_ref, o_ref, ext, pre, *, tiles_per_seq):
    i = pl.program_id(0)
    tm = u_ref.shape[0]

    @pl.when(i % tiles_per_seq == 0)
    def _():
        ext[0:CONV_HALO, :] = jnp.zeros((CONV_HALO, WIDTH), F32)
        ext[CONV_HALO + tm:CONV_HALO + tm + SUBLANES, :] = jnp.zeros((SUBLANES, WIDTH), F32)

    ext[CONV_HALO:CONV_HALO + tm, :] = u_ref[...]
    base = CONV_HALO - (CONV_TAPS - 1)
    span = CONV_CHUNK + SUBLANES
    half = WIDTH // 2
    for c0 in range(0, tm, CONV_CHUNK):
        for l0 in (0, half):
            y = None
            for r in range(SUBLANES):
                z = None
                for a in range((base + CONV_TAPS - 1) // SUBLANES + 1):
                    k = SUBLANES * a + r - base
                    if 0 <= k < CONV_TAPS:
                        term = (w_ref[k:k + 1, l0:l0 + half]
                                * ext[c0 + SUBLANES * a:c0 + SUBLANES * a + span, l0:l0 + half])
                        z = term if z is None else z + term
                zr = z[r:r + CONV_CHUNK, :]
                y = zr if y is None else y + zr
            pre[c0:c0 + CONV_CHUNK, l0:l0 + half] = y
    for c0 in range(0, tm, CONV_CHUNK):
        y = _layer_norm(pre[c0:c0 + CONV_CHUNK, :] + cb_ref[...], g_ref[...], b_ref[...])
        o_ref[c0:c0 + CONV_CHUNK, :] = (y * _sigmoid(y)).astype(BF16)
    ext[0:CONV_HALO, :] = ext[tm:tm + CONV_HALO, :]


def _conv31(u, w, cb, g, b, seq):
    t = u.shape[0]
    tm = min(TM_CONV, seq)
    assert seq % tm == 0 and tm % CONV_CHUNK == 0
    const = lambda shape: pl.BlockSpec(shape, lambda i: (0,) * len(shape))
    return pl.pallas_call(
        functools.partial(_conv31_kernel, tiles_per_seq=seq // tm),
        grid=(t // tm,),
        in_specs=[pl.BlockSpec((tm, WIDTH), lambda i: (i, 0)), const(w.shape), const(cb.shape),
                  const(g.shape), const(b.shape)],
        out_specs=pl.BlockSpec((tm, WIDTH), lambda i: (i, 0)),
        out_shape=jax.ShapeDtypeStruct((t, WIDTH), BF16),
        scratch_shapes=[pltpu.VMEM((CONV_HALO + tm + SUBLANES, WIDTH), F32),
                        pltpu.VMEM((tm, WIDTH), F32)],
        compiler_params=pltpu.CompilerParams(dimension_semantics=("arbitrary",),
                                             vmem_limit_bytes=VMEM_LIMIT),
        name="conv31",
    )(u, w, cb, g, b)


def _fox_kernel(qf_ref, kf_ref, vt_ref, o_ref, acc0_ref, acc1_ref, *, tq, tk):
    def q_tile(qi, _):
        _fox_q_tile(qi, qf_ref, kf_ref, vt_ref, o_ref, acc0_ref, acc1_ref, tq=tq, tk=tk)
        return 0

    lax.fori_loop(0, qf_ref.shape[0] // tq, q_tile, 0)


def _fox_q_tile(qi, qf_ref, kf_ref, vt_ref, o_ref, acc0_ref, acc1_ref, *, tq, tk):
    q_rows = pl.ds(pl.multiple_of(qi * tq, tq), tq)
    lane = lax.broadcasted_iota(jnp.int32, (tq, 2 * LANES), 1)
    bias_lane = lane - LANES
    qfull = qf_ref[q_rows, :]
    zero = jnp.zeros_like(qfull)
    qs = [jnp.where(((lane >= h * ATT_HEAD_DIM) & (lane < (h + 1) * ATT_HEAD_DIM))
                    | ((bias_lane >= 6 * h) & (bias_lane < 6 * h + 6)), qfull, zero)
          for h in range(2)]
    q_lo = qi * tq
    accs = (acc0_ref, acc1_ref)
    for acc in accs:
        acc[...] = jnp.zeros(acc.shape, F32)

    def tiles(jobs, carry):
        work = []
        for j, masked in jobs:
            start = pl.multiple_of(j * tk, tk)
            kfull = kf_ref[pl.ds(start, tk), :]
            ss = [lax.dot_general(kfull, qs[h], (((1,), (1,)), ((), ())),
                                  preferred_element_type=F32) for h in range(2)]
            work.append((start, masked, ss))
        carry = list(carry)
        for start, masked, ss in work:
            if masked:
                key = start + lax.broadcasted_iota(jnp.int32, ss[0].shape, 0)
                qry = q_lo + lax.broadcasted_iota(jnp.int32, ss[0].shape, 1)
                ss = [jnp.where(key <= qry, s, NEG_BIG) for s in ss]
            for h in range(2):
                m_old, l_old = carry[h]
                m_new = jnp.maximum(m_old, jnp.max(ss[h], axis=0, keepdims=True))
                a = jnp.exp2(m_old - m_new)
                p = jnp.exp2(ss[h] - m_new)
                l_new = a * l_old + jnp.sum(p.reshape(tk // SUBLANES, SUBLANES, tq), axis=0)
                vt = vt_ref[h * ATT_HEAD_DIM:(h + 1) * ATT_HEAD_DIM, pl.ds(start, tk)]
                accs[h][...] = a * accs[h][...] + jnp.dot(vt, p.astype(BF16),
                                                          preferred_element_type=F32)
                carry[h] = (m_new, l_new)
        return tuple(carry)

    init = tuple((jnp.full((1, tq), NEG_BIG, F32), jnp.zeros((SUBLANES, tq), F32)) for _ in range(2))
    n_full = q_lo // tk
    group = ATT_TILES_PER_BLOCK
    carry = lax.fori_loop(
        0, n_full // group,
        lambda j, c: tiles([(group * j + i, False) for i in range(group)], c), init)
    rest = n_full % group

    def tail(n):
        return lambda c: tiles([(n_full - n + i, False) for i in range(n)] + [(n_full, True)], c)

    (_, l0), (_, l1) = lax.switch(rest, [tail(n) for n in range(group)], carry)
    out_t = jnp.concatenate([acc0_ref[...] / jnp.sum(l0, axis=0, keepdims=True),
                             acc1_ref[...] / jnp.sum(l1, axis=0, keepdims=True)], axis=0)
    o_ref[q_rows, :] = out_t.T.astype(BF16)


def _fox_attention(qf, kf, vt, batch, seq):
    t = qf.shape[0]
    tq = min(TQ, seq)
    tk = min(TK, seq)
    assert seq % tq == 0 and seq % tk == 0 and tk % tq == 0
    pairs = ATT_HEADS // 2
    return pl.pallas_call(
        functools.partial(_fox_kernel, tq=tq, tk=tk),
        grid=(batch, pairs),
        in_specs=[pl.BlockSpec((seq, 2 * LANES), lambda b, hp: (b, hp)),
                  pl.BlockSpec((seq, 2 * LANES), lambda b, hp: (b, hp)),
                  pl.BlockSpec((LANES, seq), lambda b, hp: (hp, b))],
        out_specs=pl.BlockSpec((seq, LANES), lambda b, hp: (b, hp)),
        out_shape=jax.ShapeDtypeStruct((t, WIDTH), BF16),
        scratch_shapes=[pltpu.VMEM((ATT_HEAD_DIM, tq), F32), pltpu.VMEM((ATT_HEAD_DIM, tq), F32)],
        compiler_params=pltpu.CompilerParams(
            dimension_semantics=("arbitrary", "arbitrary"),
            vmem_limit_bytes=VMEM_LIMIT),
        name="fox_attn",
    )(qf, kf, vt)


def _partner(x, row, bit):
    n = x.shape[0]
    up = pltpu.roll(x, n - bit, axis=0)
    down = pltpu.roll(x, bit, axis=0)
    return jnp.where((row & bit) == 0, up, down)


def _merge_kernel(x_ref, att_ref, conv_ref, sc_ref, wg_ref, bg_ref, wb_ref, wo_ref,
                  g1_ref, b1_ref, wr_ref, br_ref, triu_ref,
                  x1_ref, x1p_ref, route_ref, counts_ref, carry):
    i = pl.program_id(0)

    @pl.when(i == 0)
    def _():
        carry[...] = jnp.zeros_like(carry)

    x = x_ref[...]
    xb = x.astype(BF16)
    merged = None
    for n, br_ref_n in enumerate((att_ref, conv_ref, sc_ref)):
        gate = _sigmoid(jnp.dot(xb, wg_ref[n], preferred_element_type=F32) + bg_ref[n])
        term = gate * jnp.dot(br_ref_n[...], wb_ref[n], preferred_element_type=F32)
        merged = term if merged is None else merged + term
    h = jnp.dot(merged.astype(BF16), wo_ref[...], preferred_element_type=F32)
    x1 = _layer_norm(DEEPNORM_ALPHA * x + h, g1_ref[...], b1_ref[...])
    x1_ref[...] = x1
    _pack_rows(x1, x1p_ref)

    x_hi = x1.astype(BF16)
    x_lo = (x1 - x_hi.astype(F32)).astype(BF16)
    nt = lambda w, v: lax.dot_general(w, v, (((1,), (1,)), ((), ())), preferred_element_type=F32)
    logits = nt(wr_ref[0], x_hi) + nt(wr_ref[0], x_lo) + nt(wr_ref[1], x_hi) + br_ref[...]
    mx = jnp.max(logits, axis=0, keepdims=True)
    ex = jnp.exp(logits - mx)
    p = ex / jnp.sum(ex, axis=0, keepdims=True)

    row = lax.broadcasted_iota(jnp.int32, p.shape, 0)
    sub = row & (EXPERTS_PER_GROUP - 1)
    rank = jnp.zeros(p.shape, jnp.int32)
    for d in range(1, EXPERTS_PER_GROUP):
        below = pltpu.roll(p, d, axis=0)
        above = pltpu.roll(p, N_EXPERTS - d, axis=0)
        rank = rank + jnp.where((sub >= d) & (below >= p), 1, 0)
        rank = rank + jnp.where((sub + d < EXPERTS_PER_GROUP) & (above > p), 1, 0)
    top2 = rank < 2
    score = jnp.where(top2, p, 0.0)
    score = score + _partner(score, row, 1)
    score = score + _partner(score, row, 2)
    best = jnp.max(score, axis=0, keepdims=True)
    group = row >> 2
    best_group = jnp.min(jnp.where(score == best, group, N_EXPERTS), axis=0, keepdims=True)
    sel = top2 & (group == best_group)
    gate = jnp.where(sel, p / best, 0.0)

    sel_b = jnp.where(sel, 1.0, 0.0).astype(BF16)
    before = jnp.dot(sel_b, triu_ref[...], preferred_element_type=F32) + carry[...]
    tm = x.shape[0]
    new_carry = before[:, tm - 1:tm] + sel_b[:, tm - 1:tm].astype(F32)
    carry[...] = new_carry
    counts_ref[...] = new_carry

    e_lo = jnp.min(jnp.where(sel, row, N_EXPERTS), axis=0, keepdims=True)
    e_hi = jnp.max(jnp.where(sel, row, -1), axis=0, keepdims=True)
    is_lo = row == e_lo
    is_hi = row == e_hi
    pick = lambda m, a: jnp.sum(jnp.where(m, a, 0.0), axis=0, keepdims=True)
    zero = jnp.zeros_like(mx)
    route_ref[...] = jnp.concatenate(
        [e_lo.astype(F32), e_hi.astype(F32), pick(is_lo, before), pick(is_hi, before),
         pick(is_lo, gate), pick(is_hi, gate), zero, zero], axis=0)


def _merge(x2, att, conv, sc, wg, bg, wb, wo, g1, b1, wr, br):
    t, d = x2.shape
    tm = min(TM_PROJ, t)
    assert t % tm == 0
    triu = jnp.triu(jnp.ones((tm, tm), F32), 1).astype(BF16)
    const = lambda shape: pl.BlockSpec(shape, lambda i: (0,) * len(shape),
                                       pipeline_mode=pl.Buffered(1))
    rows = lambda w: pl.BlockSpec((tm, w), lambda i: (i, 0))
    return pl.pallas_call(
        _merge_kernel,
        grid=(t // tm,),
        in_specs=[rows(d), rows(WIDTH), rows(WIDTH), rows(WIDTH),
                  const(wg.shape), const(bg.shape), const(wb.shape), const(wo.shape),
                  const(g1.shape), const(b1.shape), const(wr.shape), const(br.shape),
                  const(triu.shape)],
        out_specs=[rows(d), pl.BlockSpec((ROW_CHUNKS, tm, CHUNK), lambda i: (0, i, 0)),
                   pl.BlockSpec((ROUTE_ROWS, tm), lambda i: (0, i)),
                   pl.BlockSpec((N_EXPERTS, 1), lambda i: (0, 0))],
        out_shape=[jax.ShapeDtypeStruct((t, d), F32),
                   jax.ShapeDtypeStruct((ROW_CHUNKS, t, CHUNK), jnp.uint32),
                   jax.ShapeDtypeStruct((ROUTE_ROWS, t), F32),
                   jax.ShapeDtypeStruct((N_EXPERTS, 1), F32)],
        scratch_shapes=[pltpu.VMEM((N_EXPERTS, 1), F32)],
        compiler_params=pltpu.CompilerParams(dimension_semantics=("arbitrary",),
                                             vmem_limit_bytes=VMEM_LIMIT),
        name="merge",
    )(x2, att, conv, sc, wg, bg, wb, wo, g1, b1, wr, br, triu)


def _sc_mesh():
    return plsc.VectorSubcoreMesh(core_axis_name="core", subcore_axis_name="subcore")


def _sc_scatter_rows(x, idx, n_rows):
    t, d = x.shape
    m = idx.shape[0]
    w = SC_WINDOW
    assert t % w == 0 and m % t == 0
    steps_per_pass = t // w

    @functools.partial(pl.kernel, out_type=jax.ShapeDtypeStruct((n_rows, d), x.dtype),
                       mesh=_sc_mesh(), scratch_types=[], name="sc_dispatch")
    def run(x_hbm, i_hbm, o_hbm):
        def body(x_vmem, i_vmem):
            pltpu.sync_copy(x_vmem, o_hbm.at[i_vmem.at[0]])

        pltpu.emit_pipeline(
            body,
            grid=(m // w,),
            in_specs=[pl.BlockSpec((w, d), index_map=lambda i: (i % steps_per_pass, 0)),
                      pl.BlockSpec((1, w), index_map=lambda i: (0, i))],
            out_specs=[],
            core_axis_name=("core", "subcore"),
            dimension_semantics=(pltpu.PARALLEL,),
            trace_scopes=False,
        )(x_hbm, i_hbm)

    return run(x, idx.reshape(1, m))


def _sc_gather_rows(table, idx):
    d = table.shape[1]
    m = idx.shape[0]
    w = SC_WINDOW
    assert m % w == 0

    @functools.partial(pl.kernel, out_type=jax.ShapeDtypeStruct((m, d), table.dtype),
                       mesh=_sc_mesh(), scratch_types=[], name="sc_gather")
    def run(t_hbm, i_hbm, o_hbm):
        def body(i_vmem, o_vmem):
            pltpu.sync_copy(t_hbm.at[i_vmem.at[0]], o_vmem)

        pltpu.emit_pipeline(
            body,
            grid=(m // w,),
            in_specs=[pl.BlockSpec((1, w), index_map=lambda i: (0, i))],
            out_specs=[pl.BlockSpec((w, d), index_map=lambda i: (i, 0))],
            core_axis_name=("core", "subcore"),
            dimension_semantics=(pltpu.PARALLEL,),
            trace_scopes=False,
        )(i_hbm, o_hbm)

    return run(table, idx.reshape(1, m))


def _fill_kernel(fs_ref, fl_ref, xs_in, xs_ref, zeros, sem):
    del xs_in
    zb = zeros.shape[0]
    zeros[...] = jnp.zeros_like(zeros)

    n_regions = fs_ref.shape[0]
    rows_per_chunk = xs_ref.shape[0] // ROW_CHUNKS

    def region(e, fn):
        start = fs_ref[e % n_regions] + (e // n_regions) * rows_per_chunk
        length = fl_ref[e % n_regions]
        head = jnp.minimum(length, (SUBLANES - start % SUBLANES) % SUBLANES)
        mid = start + head
        mid_len = (length - head) // SUBLANES * SUBLANES
        n_full = mid_len // zb
        rem = mid_len - n_full * zb
        part = mid + n_full * zb
        tail = mid + mid_len
        n_tail = length - head - mid_len

        def single(base):
            def body(r, _):
                fn(pltpu.make_async_copy(zeros.at[pl.ds(0, 1)],
                                         xs_ref.at[pl.ds(base + r, 1)], sem))
                return 0
            return body
        lax.fori_loop(0, head, single(start), 0)
        lax.fori_loop(0, n_tail, single(tail), 0)

        def full(c, _):
            off = pl.multiple_of(mid + c * zb, SUBLANES)
            fn(pltpu.make_async_copy(zeros, xs_ref.at[pl.ds(off, zb)], sem))
            return 0
        lax.fori_loop(0, n_full, full, 0)
        bit = zb // 2
        while bit >= SUBLANES:
            off = pl.multiple_of(part + (rem - rem % (2 * bit)), SUBLANES)

            @pl.when((rem & bit) != 0)
            def _(bit=bit, off=off):
                fn(pltpu.make_async_copy(zeros.at[pl.ds(0, bit)],
                                         xs_ref.at[pl.ds(off, bit)], sem))
            bit //= 2

    for fn in (lambda cp: cp.start(), lambda cp: cp.wait()):
        def body(e, _, fn=fn):
            region(e, fn)
            return 0
        lax.fori_loop(0, n_regions * ROW_CHUNKS, body, 0)


def _fill_unrouted(xs, fill_start, fill_len):
    n_rows, d = xs.shape
    return pl.pallas_call(
        _fill_kernel,
        grid_spec=pltpu.PrefetchScalarGridSpec(
            num_scalar_prefetch=2,
            grid=(1,),
            in_specs=[pl.BlockSpec(memory_space=pl.ANY)],
            out_specs=pl.BlockSpec(memory_space=pl.ANY),
            scratch_shapes=[pltpu.VMEM((ZERO_ROWS, d), xs.dtype), pltpu.SemaphoreType.DMA]),
        out_shape=jax.ShapeDtypeStruct((n_rows, d), xs.dtype),
        input_output_aliases={2: 0},
        compiler_params=pltpu.CompilerParams(dimension_semantics=("arbitrary",),
                                             vmem_limit_bytes=VMEM_LIMIT),
        name="fill_unrouted",
    )(fill_start, fill_len, xs)


def _expert_kernel(be_ref, nb_ref, xs_ref, wu_ref, wd_ref, ys_ref, wu_bf, wd_bf):
    b = pl.program_id(0)
    live = b < nb_ref[0]

    @pl.when(live & ((b == 0) | (be_ref[b] != be_ref[jnp.maximum(b, 1) - 1])))
    def _():
        wu_bf[...] = wu_ref[0, 0].astype(BF16)
        wd_bf[...] = wd_ref[0, 0].astype(BF16)

    @pl.when(live)
    def _():
        xb = _unpack_rows(xs_ref).astype(BF16)
        hidden = jnp.dot(xb, wu_bf[...], preferred_element_type=F32)
        a = hidden[:, :EXPERT_FF]
        g = hidden[:, EXPERT_FF:]
        act = (g * _sigmoid(g) * a).astype(BF16)
        _pack_rows(jnp.dot(act, wd_bf[...], preferred_element_type=F32), ys_ref)

    @pl.when(jnp.logical_not(live))
    def _():
        ys_ref[...] = jnp.zeros_like(ys_ref)


def _experts(xs, w_up, w_down, layer, block_expert, n_used):
    _, n_rows, d = xs.shape
    n_blocks = n_rows // EXPERT_BLOCK

    def live(b, nb):
        return jnp.minimum(b, nb[0] - 1)

    wspec = lambda w: pl.BlockSpec((1, 1) + w.shape[2:],
                                   lambda b, be, nb: (layer, be[live(b, nb)], 0, 0))
    return pl.pallas_call(
        _expert_kernel,
        grid_spec=pltpu.PrefetchScalarGridSpec(
            num_scalar_prefetch=2,
            grid=(n_blocks,),
            in_specs=[pl.BlockSpec((ROW_CHUNKS, EXPERT_BLOCK, d), lambda b, be, nb: (0, live(b, nb), 0)),
                      wspec(w_up), wspec(w_down)],
            out_specs=pl.BlockSpec((ROW_CHUNKS, EXPERT_BLOCK, d), lambda b, be, nb: (0, b, 0)),
            scratch_shapes=[pltpu.VMEM(w_up.shape[2:], BF16), pltpu.VMEM(w_down.shape[2:], BF16)]),
        out_shape=jax.ShapeDtypeStruct(xs.shape, xs.dtype),
        compiler_params=pltpu.CompilerParams(dimension_semantics=("arbitrary",),
                                             vmem_limit_bytes=VMEM_LIMIT),
        name="experts",
    )(block_expert, n_used, xs, w_up, w_down)


def _combine_kernel(x1_ref, route_ref, y0_ref, y1_ref, g2_ref, b2_ref, o_ref):
    route = route_ref[...].T
    m = route[:, 4:5] * _unpack_rows(y0_ref.at[0]) + route[:, 5:6] * _unpack_rows(y1_ref.at[0])
    o_ref[...] = _layer_norm(DEEPNORM_ALPHA * x1_ref[...] + m, g2_ref[...], b2_ref[...])


def _combine(x1, route, yk, g2, b2):
    t, d = x1.shape
    dc = yk.shape[-1]
    tm = min(TM_ROUTE, t)
    steps = t // tm
    const = lambda shape: pl.BlockSpec(shape, lambda i: (0,) * len(shape))
    return pl.pallas_call(
        _combine_kernel,
        grid=(steps,),
        in_specs=[pl.BlockSpec((tm, d), lambda i: (i, 0)),
                  pl.BlockSpec((ROUTE_ROWS, tm), lambda i: (0, i)),
                  pl.BlockSpec((1, ROW_CHUNKS, tm, dc), lambda i: (0, 0, i, 0)),
                  pl.BlockSpec((1, ROW_CHUNKS, tm, dc), lambda i: (1, 0, i, 0)),
                  const(g2.shape), const(b2.shape)],
        out_specs=pl.BlockSpec((tm, d), lambda i: (i, 0)),
        out_shape=jax.ShapeDtypeStruct((t, d), F32),
        compiler_params=pltpu.CompilerParams(dimension_semantics=("arbitrary",),
                                             vmem_limit_bytes=VMEM_LIMIT),
        name="combine",
    )(x1, route, yk, yk, g2, b2)


def _routing_tables(route, counts, n_blocks):
    t = route.shape[1]
    cnt = counts[:, 0].astype(jnp.int32)
    padded = (cnt + EXPERT_BLOCK - 1) // EXPERT_BLOCK * EXPERT_BLOCK
    pad_end = jnp.cumsum(padded)
    pad_start = pad_end - padded
    experts = route[0:2].astype(jnp.int32)
    ranks = route[2:4].astype(jnp.int32)
    ids = jnp.arange(N_EXPERTS, dtype=jnp.int32)
    base = jnp.sum(jnp.where(experts[:, :, None] == ids, pad_start, 0), axis=-1)
    n_rows = n_blocks * EXPERT_BLOCK
    chunk_base = jnp.arange(ROW_CHUNKS, dtype=jnp.int32) * n_rows
    pos = ((base + ranks)[:, None, :] + chunk_base[None, :, None]).reshape(2 * ROW_CHUNKS * t)
    blk = jnp.arange(n_blocks, dtype=jnp.int32) * EXPERT_BLOCK
    block_expert = jnp.minimum(jnp.sum((pad_end[None, :] <= blk[:, None]).astype(jnp.int32), axis=1),
                               N_EXPERTS - 1)
    n_used = (pad_end[-1:] // EXPERT_BLOCK).astype(jnp.int32)
    fill_start = jnp.concatenate([pad_start + cnt, pad_end[-1:]]).astype(jnp.int32)
    fill_len = jnp.concatenate([padded - cnt, n_rows - pad_end[-1:]]).astype(jnp.int32)
    return pos, block_expert, n_used, fill_start, fill_len


def kernel(x, w_in, b_forget, conv_w, conv_b, conv_ln_g, conv_ln_b, sc_w, w_branch, w_gate, b_gate,
           w_out, ln1_g, ln1_b, w_up, w_down, ln2_g, ln2_b, w_router, b_router):
    batch, seq, d = x.shape
    t = batch * seq
    depth = w_in.shape[0]
    x2 = x.reshape(t, d)

    nf = ATT_HEADS
    w_main = jnp.concatenate([w_in[:, :, :2 * WIDTH], w_in[:, :, 3 * WIDTH + nf:]], axis=2).astype(BF16)
    w_vt = jnp.swapaxes(w_in[:, :, 2 * WIDTH:3 * WIDTH], 1, 2).astype(BF16)
    w_f = jnp.pad(w_in[:, :, 3 * WIDTH:3 * WIDTH + nf], ((0, 0), (0, 0), (0, LANES - nf))).astype(BF16)
    b_f = jnp.pad(b_forget, ((0, 0), (0, LANES - nf)))[:, None, :]
    wg = w_gate.astype(BF16)
    wb = w_branch.astype(BF16)
    wo = w_out.astype(BF16)
    wr_t = w_router.T
    wr_hi = wr_t.astype(BF16)
    wr_lo = (wr_t - wr_hi.astype(F32)).astype(BF16)
    wr = jnp.stack([wr_hi, wr_lo])
    br = b_router[:, None]

    def layer(xh, l):
        th = xh.shape[0]
        n_blocks = (2 * th + N_EXPERTS * (EXPERT_BLOCK - 1) + EXPERT_BLOCK - 1) // EXPERT_BLOCK
        n_rows = n_blocks * EXPERT_BLOCK
        qf, kf, vt, u, br_sc = _in_proj(xh, w_main[l], w_vt[l], w_f[l], b_f[l], sc_w[l], seq)
        br_conv = _conv31(u, conv_w[l], conv_b[l][None, :], conv_ln_g[l][None, :],
                          conv_ln_b[l][None, :], seq)
        br_att = _fox_attention(qf, kf, vt, th // seq, seq)
        x1, x1p, route, counts = _merge(xh, br_att, br_conv, br_sc, wg[l], b_gate[l][:, None, :],
                                        wb[l], wo[l], ln1_g[l][None, :], ln1_b[l][None, :], wr, br)
        pos, block_expert, n_used, fill_start, fill_len = _routing_tables(route, counts, n_blocks)
        xs = _sc_scatter_rows(x1p.reshape(ROW_CHUNKS * th, CHUNK), pos, ROW_CHUNKS * n_rows)
        xs = _fill_unrouted(xs, fill_start, fill_len).reshape(ROW_CHUNKS, n_rows, CHUNK)
        ys = _experts(xs, w_up, w_down, l, block_expert, n_used)
        yk = _sc_gather_rows(ys.reshape(ROW_CHUNKS * n_rows, CHUNK), pos)
        return _combine(x1, route, yk.reshape(2, ROW_CHUNKS, th, CHUNK),
                        ln2_g[l][None, :], ln2_b[l][None, :])

    for l in range(depth):
        x2 = layer(x2, l)
    return x2.reshape(batch, seq, d)
```

```python
import functools

import numpy as np
import jax
import jax.numpy as jnp
from jax import lax
from jax.experimental import pallas as pl
from jax.experimental.pallas import tpu as pltpu
from jax.experimental.pallas import tpu_sc as plsc

F32 = jnp.float32
BF16 = jnp.bfloat16

ATT_HEADS = 8
ATT_HEAD_DIM = 64
WIDTH = 512
CONV_TAPS = 31
SC_TAPS = 3
N_EXPERTS = 32
EXPERTS_PER_GROUP = 4
EXPERT_FF = 512
DEPTH = 4
DEEPNORM_ALPHA = (2 * DEPTH) ** 0.25
LN_EPS = 1e-5

LANES = 128
SUBLANES = 8
VMEM_LIMIT = 56 * 1024 * 1024

TM_PROJ = 512
TM_CONV = 512
CONV_HALO = 32
CONV_CHUNK = 64
TQ = 512
TK = 512
ATT_TILES_PER_BLOCK = 4
ATT_SUM_ROWS = 16
EXPERT_BLOCK = 512
TM_ROUTE = 512
SC_WINDOW = 128
ROW_CHUNKS = 2
CHUNK = 256
ROUTE_ROWS = 8
ZERO_ROWS = 256
NEG_BIG = -1e30
LOG2E = 1.4426950408889634


def _sigmoid(x):
    return 1.0 / (1.0 + jnp.exp(-x))


def _log_sigmoid(x):
    return jnp.minimum(x, 0.0) - jnp.log(1.0 + jnp.exp(-jnp.abs(x)))


def _split3(x):
    hi = x.astype(BF16)
    r1 = x - hi.astype(F32)
    mid = r1.astype(BF16)
    lo = (r1 - mid.astype(F32)).astype(BF16)
    return hi, mid, lo


def _pack_rows(x, ref):
    half = x.shape[1] // 2
    bits = pltpu.bitcast(x, jnp.uint32)
    bits = bits + (jnp.uint32(0x7FFF) + ((bits >> 16) & jnp.uint32(1)))
    words = (bits[:, :half] & jnp.uint32(0xFFFF0000)) | (bits[:, half:] >> 16)
    for c in range(ROW_CHUNKS):
        ref[c] = words[:, c * CHUNK:(c + 1) * CHUNK]


def _unpack_rows(ref):
    words = jnp.concatenate([ref[c] for c in range(ROW_CHUNKS)], axis=1)
    hi = pltpu.bitcast(words & jnp.uint32(0xFFFF0000), F32)
    lo = pltpu.bitcast(words << 16, F32)
    return jnp.concatenate([hi, lo], axis=1)


def _layer_norm(y, g, b):
    mu = jnp.mean(y, axis=-1, keepdims=True)
    yc = y - mu
    var = jnp.mean(yc * yc, axis=-1, keepdims=True)
    return yc * lax.rsqrt(var + LN_EPS) * g + b


def _bias_placement():
    pairs = ATT_HEADS // 2
    place = np.zeros((3 * LANES, 2 * pairs * LANES), np.float32)
    ones = np.zeros((1, 2 * pairs * LANES), np.float32)
    for p in range(pairs):
        for h in range(2):
            for piece in range(3):
                src = piece * LANES + 2 * p + h
                place[src, p * LANES + 6 * h + 3 + piece] = 1.0
                place[src, (pairs + p) * LANES + 6 * h + piece] = -1.0
                ones[0, p * LANES + 6 * h + piece] = 1.0
                ones[0, (pairs + p) * LANES + 6 * h + 3 + piece] = 1.0
    return jnp.asarray(place, BF16), jnp.asarray(ones, F32)


def _in_proj_kernel(x_ref, wm_ref, wvt_ref, wf_ref, bf_ref, scw_ref, tri_ref, place_ref, ones_ref,
                    cw_ref, cb_ref, cg_ref, cbeta_ref,
                    qf_ref, kf_ref, vt_ref, brconv_ref, brsc_ref,
                    carry_c, carry_p, ext, pre, *, tiles_per_seq):
    i = pl.program_id(0)
    tm = x_ref.shape[0]

    @pl.when(i % tiles_per_seq == 0)
    def _():
        carry_c[...] = jnp.zeros_like(carry_c)
        carry_p[...] = jnp.zeros_like(carry_p)
        _conv31_reset(ext, tm)

    xb = x_ref[...].astype(BF16)

    def mm(j):
        return jnp.dot(xb, wm_ref[:, j * WIDTH:(j + 1) * WIDTH], preferred_element_type=F32)

    u = mm(2) * _sigmoid(mm(3))
    q =(mm(0) * (ATT_HEAD_DIM ** -0.5 * LOG2E)).astype(BF16)
    k = mm(1).astype(BF16)
    vt_ref[...] = lax.dot_general(wvt_ref[...], xb, (((1,), (1,)), ((), ())),
                                  preferred_element_type=F32).astype(BF16)

    p = mm(5) * mm(6)
    row = lax.broadcasted_iota(jnp.int32, p.shape, 0)
    prev = carry_p[...]
    p1 = jnp.where(row == 0, prev[7:8, :], pltpu.roll(p, 1, axis=0))
    p2 = pltpu.roll(p, 2, axis=0)
    p2 = jnp.where(row == 0, prev[6:7, :], jnp.where(row == 1, prev[7:8, :], p2))
    y = scw_ref[0:1, :] * p2 + scw_ref[1:2, :] * p1 + scw_ref[2:3, :] * p
    brsc_ref[...] = (mm(4) * y).astype(BF16)
    carry_p[...] = p[tm - SUBLANES:tm, :]

    zf = jnp.dot(xb, wf_ref[...], preferred_element_type=F32)
    ls = _log_sigmoid(zf + bf_ref[...])
    hi, mid, lo = _split3(ls)
    tri = tri_ref[...]
    c = (jnp.dot(tri, hi, preferred_element_type=F32)
         + jnp.dot(tri, mid, preferred_element_type=F32)
         + jnp.dot(tri, lo, preferred_element_type=F32)) + carry_c[...]
    carry_c[...] = c[tm - 1:tm, :]

    bias = ones_ref[...]
    for n, piece in enumerate(_split3(c * LOG2E)):
        bias = bias + jnp.dot(piece, place_ref[n * LANES:(n + 1) * LANES, :],
                              preferred_element_type=F32)
    bias = bias.astype(BF16)
    pairs = ATT_HEADS // 2
    for p in range(pairs):
        lo_, hi_ = p * LANES, (p + 1) * LANES
        qf_ref[:, 2 * lo_:2 * lo_ + LANES] = q[:, lo_:hi_]
        qf_ref[:, 2 * lo_ + LANES:2 * hi_] = bias[:, lo_:hi_]
        kf_ref[:, 2 * lo_:2 * lo_ + LANES] = k[:, lo_:hi_]
        kf_ref[:, 2 * lo_ + LANES:2 * hi_] = bias[:, (pairs + p) * LANES:(pairs + p + 1) * LANES]

    _conv31_tile(u, cw_ref, cb_ref, cg_ref, cbeta_ref, brconv_ref, ext, pre)


def _in_proj(x2, wm, wvt, wf, bf, scw, cw, cb, cg, cbeta, seq):
    t, d = x2.shape
    tm = min(TM_PROJ, seq)
    assert seq % tm == 0 and t % seq == 0 and tm % CONV_CHUNK == 0
    tri = jnp.tril(jnp.ones((tm, tm), F32)).astype(BF16)
    place, ones = _bias_placement()
    const = lambda shape: pl.BlockSpec(shape, lambda i: (0,) * len(shape))
    rows = lambda w: pl.BlockSpec((tm, w), lambda i: (i, 0))
    return pl.pallas_call(
        functools.partial(_in_proj_kernel, tiles_per_seq=seq // tm),
        grid=(t // tm,),
        in_specs=[rows(d), const(wm.shape), const(wvt.shape), const(wf.shape), const(bf.shape),
                  const(scw.shape), const(tri.shape), const(place.shape), const(ones.shape),
                  const(cw.shape), const(cb.shape), const(cg.shape), const(cbeta.shape)],
        out_specs=[rows(2 * WIDTH), rows(2 * WIDTH), pl.BlockSpec((WIDTH, tm), lambda i: (0, i)),
                   rows(WIDTH), rows(WIDTH)],
        out_shape=[jax.ShapeDtypeStruct((t, 2 * WIDTH), BF16),
                   jax.ShapeDtypeStruct((t, 2 * WIDTH), BF16),
                   jax.ShapeDtypeStruct((WIDTH, t), BF16),
                   jax.ShapeDtypeStruct((t, WIDTH), BF16),
                   jax.ShapeDtypeStruct((t, WIDTH), BF16)],
        scratch_shapes=[pltpu.VMEM((1, LANES), F32), pltpu.VMEM((SUBLANES, WIDTH), F32),
                        pltpu.VMEM((CONV_HALO + tm + SUBLANES, WIDTH), F32),
                        pltpu.VMEM((tm, WIDTH), F32)],
        compiler_params=pltpu.CompilerParams(dimension_semantics=("arbitrary",),
                                             vmem_limit_bytes=VMEM_LIMIT),
        name="in_proj",
    )(x2, wm, wvt, wf, bf, scw, tri, place, ones, cw, cb, cg, cbeta)


def _conv31_reset(ext, tm):
    ext[0:CONV_HALO, :] = jnp.zeros((CONV_HALO, WIDTH), F32)
    ext[CONV_HALO + tm:CONV_HALO + tm + SUBLANES, :] = jnp.zeros((SUBLANES, WIDTH), F32)


def _conv31_tile(u, w_ref, cb_ref, g_ref, b_ref, o_ref, ext, pre):
    tm = u.shape[0]
    ext[CONV_HALO:CONV_HALO + tm, :] = u
    base = CONV_HALO - (CONV_TAPS - 1)
    span = CONV_CHUNK + SUBLANES
    half = WIDTH // 2
    for c0 in range(0, tm, CONV_CHUNK):
        for l0 in (0, half):
            y = None
            for r in range(SUBLANES):
                z = None
                for a in range((base + CONV_TAPS - 1) // SUBLANES + 1):
                    k = SUBLANES * a + r - base
                    if 0 <= k < CONV_TAPS:
                        term = (w_ref[k:k + 1, l0:l0 + half]
                                * ext[c0 + SUBLANES * a:c0 + SUBLANES * a + span, l0:l0 + half])
                        z = term if z is None else z + term
                zr = z[r:r + CONV_CHUNK, :]
                y = zr if y is None else y + zr
            pre[c0:c0 + CONV_CHUNK, l0:l0 + half] = y
    for c0 in range(0, tm, CONV_CHUNK):
        y = _layer_norm(pre[c0:c0 + CONV_CHUNK, :] + cb_ref[...], g_ref[...], b_ref[...])
        o_ref[c0:c0 + CONV_CHUNK, :] = (y * _sigmoid(y)).astype(BF16)
    ext[0:CONV_HALO, :] = ext[tm:tm + CONV_HALO, :]


def _fox_kernel(qf_ref, kf_ref, vt_ref, o_ref, acc0_ref, acc1_ref, *, tq, tk):
    def q_tile(qi, _):
        _fox_q_tile(qi, qf_ref, kf_ref, vt_ref, o_ref, acc0_ref, acc1_ref, tq=tq, tk=tk)
        return 0

    lax.fori_loop(0, qf_ref.shape[0] // tq, q_tile, 0)


def _fox_q_tile(qi, qf_ref, kf_ref, vt_ref, o_ref, acc0_ref, acc1_ref, *, tq, tk):
    q_rows = pl.ds(pl.multiple_of(qi * tq, tq), tq)
    lane = lax.broadcasted_iota(jnp.int32, (tq, 2 * LANES), 1)
    bias_lane = lane - LANES
    qfull = qf_ref[q_rows, :]
    zero = jnp.zeros_like(qfull)
    qs = [jnp.where(((lane >= h * ATT_HEAD_DIM) & (lane < (h + 1) * ATT_HEAD_DIM))
                    | ((bias_lane >= 6 * h) & (bias_lane < 6 * h + 6)), qfull, zero)
          for h in range(2)]
    q_lo = qi * tq
    accs = (acc0_ref, acc1_ref)
    for acc in accs:
        acc[...] = jnp.zeros(acc.shape, F32)

    def tiles(jobs, carry):
        work = []
        for j, masked in jobs:
            start = pl.multiple_of(j * tk, tk)
            kfull = kf_ref[pl.ds(start, tk), :]
            ss = [lax.dot_general(kfull, qs[h], (((1,), (1,)), ((), ())),
                                  preferred_element_type=F32) for h in range(2)]
            work.append((start, masked, ss))
        carry = list(carry)
        for start, masked, ss in work:
            if masked:
                key = start + lax.broadcasted_iota(jnp.int32, ss[0].shape, 0)
                qry = q_lo + lax.broadcasted_iota(jnp.int32, ss[0].shape, 1)
                ss = [jnp.where(key <= qry, s, NEG_BIG) for s in ss]
            for h in range(2):
                m_old = carry[h]
                m_new = jnp.maximum(m_old, jnp.max(ss[h], axis=0, keepdims=True))
                a = jnp.exp2(m_old - m_new)
                p = jnp.exp2(ss[h] - m_new).astype(BF16)
                vt = jnp.concatenate(
                    [vt_ref[h * ATT_HEAD_DIM:(h + 1) * ATT_HEAD_DIM, pl.ds(start, tk)], ones_rows],
                    axis=0)
                accs[h][...] = a * accs[h][...] + jnp.dot(vt, p, preferred_element_type=F32)
                carry[h] = m_new
        return tuple(carry)

    ones_rows = jnp.where(lax.broadcasted_iota(jnp.int32, (ATT_SUM_ROWS, tk), 0) == 0,
                          1.0, 0.0).astype(BF16)
    init = tuple(jnp.full((1, tq), NEG_BIG, F32) for _ in range(2))
    n_full = q_lo // tk
    group = ATT_TILES_PER_BLOCK
    carry = lax.fori_loop(
        0, n_full // group,
        lambda j, c: tiles([(group * j + i, False) for i in range(group)], c), init)
    rest = n_full % group

    def tail(n):
        return lambda c: tiles([(n_full - n + i, False) for i in range(n)] + [(n_full, True)], c)

    lax.switch(rest, [tail(n) for n in range(group)], carry)
    d = ATT_HEAD_DIM
    out_t = jnp.concatenate([acc[0:d, :] / acc[d:d + 1, :] for acc in accs], axis=0)
    o_ref[q_rows, :] = out_t.T.astype(BF16)


def _fox_attention(qf, kf, vt, batch, seq):
    t = qf.shape[0]
    tq = min(TQ, seq)
    tk = min(TK, seq)
    assert seq % tq == 0 and seq % tk == 0 and tk % tq == 0
    pairs = ATT_HEADS // 2
    return pl.pallas_call(
        functools.partial(_fox_kernel, tq=tq, tk=tk),
        grid=(batch, pairs),
        in_specs=[pl.BlockSpec((seq, 2 * LANES), lambda b, hp: (b, hp)),
                  pl.BlockSpec((seq, 2 * LANES), lambda b, hp: (b, hp)),
                  pl.BlockSpec((LANES, seq), lambda b, hp: (hp, b))],
        out_specs=pl.BlockSpec((seq, LANES), lambda b, hp: (b, hp)),
        out_shape=jax.ShapeDtypeStruct((t, WIDTH), BF16),
        scratch_shapes=[pltpu.VMEM((ATT_HEAD_DIM + ATT_SUM_ROWS, tq), F32),
                        pltpu.VMEM((ATT_HEAD_DIM + ATT_SUM_ROWS, tq), F32)],
        compiler_params=pltpu.CompilerParams(
            dimension_semantics=("arbitrary", "arbitrary"),
            vmem_limit_bytes=VMEM_LIMIT),
        name="fox_attn",
    )(qf, kf, vt)


def _partner(x, row, bit):
    n = x.shape[0]
    up = pltpu.roll(x, n - bit, axis=0)
    down = pltpu.roll(x, bit, axis=0)
    return jnp.where((row & bit) == 0, up, down)


def _merge_kernel(x_ref, att_ref, conv_ref, sc_ref, wg_ref, bg_ref, wb_ref, wo_ref,
                  g1_ref, b1_ref, wr_ref, br_ref, triu_ref,
                  x1_ref, x1p_ref, route_ref, counts_ref, carry):
    i = pl.program_id(0)

    @pl.when(i == 0)
    def _():
        carry[...] = jnp.zeros_like(carry)

    x = x_ref[...]
    xb = x.astype(BF16)
    merged = None
    for n, br_ref_n in enumerate((att_ref, conv_ref, sc_ref)):
        gate = _sigmoid(jnp.dot(xb, wg_ref[n], preferred_element_type=F32) + bg_ref[n])
        term = gate * jnp.dot(br_ref_n[...], wb_ref[n], preferred_element_type=F32)
        merged = term if merged is None else merged + term
    h = jnp.dot(merged.astype(BF16), wo_ref[...], preferred_element_type=F32)
    x1 = _layer_norm(DEEPNORM_ALPHA * x + h, g1_ref[...], b1_ref[...])
    x1_ref[...] = x1
    _pack_rows(x1, x1p_ref)

    x_hi = x1.astype(BF16)
    x_lo = (x1 - x_hi.astype(F32)).astype(BF16)
    nt = lambda w, v: lax.dot_general(w, v, (((1,), (1,)), ((), ())), preferred_element_type=F32)
    logits = nt(wr_ref[0], x_hi) + nt(wr_ref[0], x_lo) + nt(wr_ref[1], x_hi) + br_ref[...]
    mx = jnp.max(logits, axis=0, keepdims=True)
    ex = jnp.exp(logits - mx)
    p = ex / jnp.sum(ex, axis=0, keepdims=True)

    row = lax.broadcasted_iota(jnp.int32, p.shape, 0)
    sub = row & (EXPERTS_PER_GROUP - 1)
    rank = jnp.zeros(p.shape, jnp.int32)
    for d in range(1, EXPERTS_PER_GROUP):
        below = pltpu.roll(p, d, axis=0)
        above = pltpu.roll(p, N_EXPERTS - d, axis=0)
        rank = rank + jnp.where((sub >= d) & (below >= p), 1, 0)
        rank = rank + jnp.where((sub + d < EXPERTS_PER_GROUP) & (above > p), 1, 0)
    top2 = rank < 2
    score = jnp.where(top2, p, 0.0)
    score = score + _partner(score, row, 1)
    score = score + _partner(score, row, 2)
    best = jnp.max(score, axis=0, keepdims=True)
    group = row >> 2
    best_group = jnp.min(jnp.where(score == best, group, N_EXPERTS), axis=0, keepdims=True)
    sel = top2 & (group == best_group)
    gate = jnp.where(sel, p / best, 0.0)

    sel_b = jnp.where(sel, 1.0, 0.0).astype(BF16)
    before = jnp.dot(sel_b, triu_ref[...], preferred_element_type=F32) + carry[...]
    tm = x.shape[0]
    new_carry = before[:, tm - 1:tm] + sel_b[:, tm - 1:tm].astype(F32)
    carry[...] = new_carry
    counts_ref[...] = new_carry

    e_lo = jnp.min(jnp.where(sel, row, N_EXPERTS), axis=0, keepdims=True)
    e_hi = jnp.max(jnp.where(sel, row, -1), axis=0, keepdims=True)
    is_lo = row == e_lo
    is_hi = row == e_hi
    pick = lambda m, a: jnp.sum(jnp.where(m, a, 0.0), axis=0, keepdims=True)
    zero = jnp.zeros_like(mx)
    route_ref[...] = jnp.concatenate(
        [e_lo.astype(F32), e_hi.astype(F32), pick(is_lo, before), pick(is_hi, before),
         pick(is_lo, gate), pick(is_hi, gate), zero, zero], axis=0)


def _merge(x2, att, conv, sc, wg, bg, wb, wo, g1, b1, wr, br):
    t, d = x2.shape
    tm = min(TM_PROJ, t)
    assert t % tm == 0
    triu = jnp.triu(jnp.ones((tm, tm), F32), 1).astype(BF16)
    const = lambda shape: pl.BlockSpec(shape, lambda i: (0,) * len(shape),
                                       pipeline_mode=pl.Buffered(1))
    rows = lambda w: pl.BlockSpec((tm, w), lambda i: (i, 0))
    return pl.pallas_call(
        _merge_kernel,
        grid=(t // tm,),
        in_specs=[rows(d), rows(WIDTH), rows(WIDTH), rows(WIDTH),
                  const(wg.shape), const(bg.shape), const(wb.shape), const(wo.shape),
                  const(g1.shape), const(b1.shape), const(wr.shape), const(br.shape),
                  const(triu.shape)],
        out_specs=[rows(d), pl.BlockSpec((ROW_CHUNKS, tm, CHUNK), lambda i: (0, i, 0)),
                   pl.BlockSpec((ROUTE_ROWS, tm), lambda i: (0, i)),
                   pl.BlockSpec((N_EXPERTS, 1), lambda i: (0, 0))],
        out_shape=[jax.ShapeDtypeStruct((t, d), F32),
                   jax.ShapeDtypeStruct((ROW_CHUNKS, t, CHUNK), jnp.uint32),
                   jax.ShapeDtypeStruct((ROUTE_ROWS, t), F32),
                   jax.ShapeDtypeStruct((N_EXPERTS, 1), F32)],
        scratch_shapes=[pltpu.VMEM((N_EXPERTS, 1), F32)],
        compiler_params=pltpu.CompilerParams(dimension_semantics=("arbitrary",),
                                             vmem_limit_bytes=VMEM_LIMIT),
        name="merge",
    )(x2, att, conv, sc, wg, bg, wb, wo, g1, b1, wr, br, triu)


def _sc_mesh():
    return plsc.VectorSubcoreMesh(core_axis_name="core", subcore_axis_name="subcore")


def _sc_scatter_rows(x, idx, n_rows):
    t, d = x.shape
    m = idx.shape[0]
    w = SC_WINDOW
    assert t % w == 0 and m % t == 0
    steps_per_pass = t // w

    @functools.partial(pl.kernel, out_type=jax.ShapeDtypeStruct((n_rows, d), x.dtype),
                       mesh=_sc_mesh(), scratch_types=[], name="sc_dispatch")
    def run(x_hbm, i_hbm, o_hbm):
        def body(x_vmem, i_vmem):
            pltpu.sync_copy(x_vmem, o_hbm.at[i_vmem.at[0]])

        pltpu.emit_pipeline(
            body,
            grid=(m // w,),
            in_specs=[pl.BlockSpec((w, d), index_map=lambda i: (i % steps_per_pass, 0)),
                      pl.BlockSpec((1, w), index_map=lambda i: (0, i))],
            out_specs=[],
            core_axis_name=("core", "subcore"),
            dimension_semantics=(pltpu.PARALLEL,),
            trace_scopes=False,
        )(x_hbm, i_hbm)

    return run(x, idx.reshape(1, m))


def _sc_gather_rows(table, idx):
    d = table.shape[1]
    m = idx.shape[0]
    w = SC_WINDOW
    assert m % w == 0

    @functools.partial(pl.kernel, out_type=jax.ShapeDtypeStruct((m, d), table.dtype),
                       mesh=_sc_mesh(), scratch_types=[], name="sc_gather")
    def run(t_hbm, i_hbm, o_hbm):
        def body(i_vmem, o_vmem):
            pltpu.sync_copy(t_hbm.at[i_vmem.at[0]], o_vmem)

        pltpu.emit_pipeline(
            body,
            grid=(m // w,),
            in_specs=[pl.BlockSpec((1, w), index_map=lambda i: (0, i))],
            out_specs=[pl.BlockSpec((w, d), index_map=lambda i: (i, 0))],
            core_axis_name=("core", "subcore"),
            dimension_semantics=(pltpu.PARALLEL,),
            trace_scopes=False,
        )(i_hbm, o_hbm)

    return run(table, idx.reshape(1, m))


def _fill_kernel(fs_ref, fl_ref, xs_in, xs_ref, zeros, sem):
    del xs_in
    zb = zeros.shape[0]
    zeros[...] = jnp.zeros_like(zeros)

    n_regions = fs_ref.shape[0]
    rows_per_chunk = xs_ref.shape[0] // ROW_CHUNKS

    def region(e, fn):
        start = fs_ref[e % n_regions] + (e // n_regions) * rows_per_chunk
        length = fl_ref[e % n_regions]
        head = jnp.minimum(length, (SUBLANES - start % SUBLANES) % SUBLANES)
        mid = start + head
        mid_len = (length - head) // SUBLANES * SUBLANES
        n_full = mid_len // zb
        rem = mid_len - n_full * zb
        part = mid + n_full * zb
        tail = mid + mid_len
        n_tail = length - head - mid_len

        def single(base):
            def body(r, _):
                fn(pltpu.make_async_copy(zeros.at[pl.ds(0, 1)],
                                         xs_ref.at[pl.ds(base + r, 1)], sem))
                return 0
            return body
        lax.fori_loop(0, head, single(start), 0)
        lax.fori_loop(0, n_tail, single(tail), 0)

        def full(c, _):
            off = pl.multiple_of(mid + c * zb, SUBLANES)
            fn(pltpu.make_async_copy(zeros, xs_ref.at[pl.ds(off, zb)], sem))
            return 0
        lax.fori_loop(0, n_full, full, 0)
        bit = zb // 2
        while bit >= SUBLANES:
            off = pl.multiple_of(part + (rem - rem % (2 * bit)), SUBLANES)

            @pl.when((rem & bit) != 0)
            def _(bit=bit, off=off):
                fn(pltpu.make_async_copy(zeros.at[pl.ds(0, bit)],
                                         xs_ref.at[pl.ds(off, bit)], sem))
            bit //= 2

    for fn in (lambda cp: cp.start(), lambda cp: cp.wait()):
        def body(e, _, fn=fn):
            region(e, fn)
            return 0
        lax.fori_loop(0, n_regions * ROW_CHUNKS, body, 0)


def _fill_unrouted(xs, fill_start, fill_len):
    n_rows, d = xs.shape
    return pl.pallas_call(
        _fill_kernel,
        grid_spec=pltpu.PrefetchScalarGridSpec(
            num_scalar_prefetch=2,
            grid=(1,),
            in_specs=[pl.BlockSpec(memory_space=pl.ANY)],
            out_specs=pl.BlockSpec(memory_space=pl.ANY),
            scratch_shapes=[pltpu.VMEM((ZERO_ROWS, d), xs.dtype), pltpu.SemaphoreType.DMA]),
        out_shape=jax.ShapeDtypeStruct((n_rows, d), xs.dtype),
        input_output_aliases={2: 0},
        compiler_params=pltpu.CompilerParams(dimension_semantics=("arbitrary",),
                                             vmem_limit_bytes=VMEM_LIMIT),
        name="fill_unrouted",
    )(fill_start, fill_len, xs)


def _expert_kernel(be_ref, nb_ref, xs_ref, wu_ref, wd_ref, ys_ref, wu_bf, wd_bf):
    b = pl.program_id(0)
    live = b < nb_ref[0]

    @pl.when(live & ((b == 0) | (be_ref[b] != be_ref[jnp.maximum(b, 1) - 1])))
    def _():
        wu_bf[...] = wu_ref[0, 0].astype(BF16)
        wd_bf[...] = wd_ref[0, 0].astype(BF16)

    @pl.when(live)
    def _():
        xb = _unpack_rows(xs_ref).astype(BF16)
        hidden = jnp.dot(xb, wu_bf[...], preferred_element_type=F32)
        a = hidden[:, :EXPERT_FF]
        g = hidden[:, EXPERT_FF:]
        act = (g * _sigmoid(g) * a).astype(BF16)
        _pack_rows(jnp.dot(act, wd_bf[...], preferred_element_type=F32), ys_ref)

    @pl.when(jnp.logical_not(live))
    def _():
        ys_ref[...] = jnp.zeros_like(ys_ref)


def _experts(xs, w_up, w_down, layer, block_expert, n_used):
    _, n_rows, d = xs.shape
    n_blocks = n_rows // EXPERT_BLOCK

    def live(b, nb):
        return jnp.minimum(b, nb[0] - 1)

    wspec = lambda w: pl.BlockSpec((1, 1) + w.shape[2:],
                                   lambda b, be, nb: (layer, be[live(b, nb)], 0, 0))
    return pl.pallas_call(
        _expert_kernel,
        grid_spec=pltpu.PrefetchScalarGridSpec(
            num_scalar_prefetch=2,
            grid=(n_blocks,),
            in_specs=[pl.BlockSpec((ROW_CHUNKS, EXPERT_BLOCK, d), lambda b, be, nb: (0, live(b, nb), 0)),
                      wspec(w_up), wspec(w_down)],
            out_specs=pl.BlockSpec((ROW_CHUNKS, EXPERT_BLOCK, d), lambda b, be, nb: (0, b, 0)),
            scratch_shapes=[pltpu.VMEM(w_up.shape[2:], BF16), pltpu.VMEM(w_down.shape[2:], BF16)]),
        out_shape=jax.ShapeDtypeStruct(xs.shape, xs.dtype),
        compiler_params=pltpu.CompilerParams(dimension_semantics=("arbitrary",),
                                             vmem_limit_bytes=VMEM_LIMIT),
        name="experts",
    )(block_expert, n_used, xs, w_up, w_down)


def _combine_kernel(x1_ref, route_ref, y0_ref, y1_ref, g2_ref, b2_ref, o_ref):
    route = route_ref[...].T
    m = route[:, 4:5] * _unpack_rows(y0_ref.at[0]) + route[:, 5:6] * _unpack_rows(y1_ref.at[0])
    o_ref[...] = _layer_norm(DEEPNORM_ALPHA * x1_ref[...] + m, g2_ref[...], b2_ref[...])


def _combine(x1, route, yk, g2, b2):
    t, d = x1.shape
    dc = yk.shape[-1]
    tm = min(TM_ROUTE, t)
    steps = t // tm
    const = lambda shape: pl.BlockSpec(shape, lambda i: (0,) * len(shape))
    return pl.pallas_call(
        _combine_kernel,
        grid=(steps,),
        in_specs=[pl.BlockSpec((tm, d), lambda i: (i, 0)),
                  pl.BlockSpec((ROUTE_ROWS, tm), lambda i: (0, i)),
                  pl.BlockSpec((1, ROW_CHUNKS, tm, dc), lambda i: (0, 0, i, 0)),
                  pl.BlockSpec((1, ROW_CHUNKS, tm, dc), lambda i: (1, 0, i, 0)),
                  const(g2.shape), const(b2.shape)],
        out_specs=pl.BlockSpec((tm, d), lambda i: (i, 0)),
        out_shape=jax.ShapeDtypeStruct((t, d), F32),
        compiler_params=pltpu.CompilerParams(dimension_semantics=("arbitrary",),
                                             vmem_limit_bytes=VMEM_LIMIT),
        name="combine",
    )(x1, route, yk, yk, g2, b2)


def _routing_tables(route, counts, n_blocks):
    t = route.shape[1]
    cnt = counts[:, 0].astype(jnp.int32)
    padded = (cnt + EXPERT_BLOCK - 1) // EXPERT_BLOCK * EXPERT_BLOCK
    pad_end = jnp.cumsum(padded)
    pad_start = pad_end - padded
    experts = route[0:2].astype(jnp.int32)
    ranks = route[2:4].astype(jnp.int32)
    ids = jnp.arange(N_EXPERTS, dtype=jnp.int32)
    base = jnp.sum(jnp.where(experts[:, :, None] == ids, pad_start, 0), axis=-1)
    n_rows = n_blocks * EXPERT_BLOCK
    chunk_base = jnp.arange(ROW_CHUNKS, dtype=jnp.int32) * n_rows
    pos = ((base + ranks)[:, None, :] + chunk_base[None, :, None]).reshape(2 * ROW_CHUNKS * t)
    blk = jnp.arange(n_blocks, dtype=jnp.int32) * EXPERT_BLOCK
    block_expert = jnp.minimum(jnp.sum((pad_end[None, :] <= blk[:, None]).astype(jnp.int32), axis=1),
                               N_EXPERTS - 1)
    n_used = (pad_end[-1:] // EXPERT_BLOCK).astype(jnp.int32)
    fill_start = jnp.concatenate([pad_start + cnt, pad_end[-1:]]).astype(jnp.int32)
    fill_len = jnp.concatenate([padded - cnt, n_rows - pad_end[-1:]]).astype(jnp.int32)
    return pos, block_expert, n_used, fill_start, fill_len


def kernel(x, w_in, b_forget, conv_w, conv_b, conv_ln_g, conv_ln_b, sc_w, w_branch, w_gate, b_gate,
           w_out, ln1_g, ln1_b, w_up, w_down, ln2_g, ln2_b, w_router, b_router):
    batch, seq, d = x.shape
    t = batch * seq
    depth = w_in.shape[0]
    x2 = x.reshape(t, d)

    nf = ATT_HEADS
    w_main = jnp.concatenate([w_in[:, :, :2 * WIDTH], w_in[:, :, 3 * WIDTH + nf:]], axis=2).astype(BF16)
    w_vt = jnp.swapaxes(w_in[:, :, 2 * WIDTH:3 * WIDTH], 1, 2).astype(BF16)
    w_f = jnp.pad(w_in[:, :, 3 * WIDTH:3 * WIDTH + nf], ((0, 0), (0, 0), (0, LANES - nf))).astype(BF16)
    b_f = jnp.pad(b_forget, ((0, 0), (0, LANES - nf)))[:, None, :]
    wg = w_gate.astype(BF16)
    wb = w_branch.astype(BF16)
    wo = w_out.astype(BF16)
    wr_t = w_router.T
    wr_hi = wr_t.astype(BF16)
    wr_lo = (wr_t - wr_hi.astype(F32)).astype(BF16)
    wr = jnp.stack([wr_hi, wr_lo])
    br = b_router[:, None]

    def layer(xh, l):
        th = xh.shape[0]
        n_blocks = (2 * th + N_EXPERTS * (EXPERT_BLOCK - 1) + EXPERT_BLOCK - 1) // EXPERT_BLOCK
        n_rows = n_blocks * EXPERT_BLOCK
        qf, kf, vt, br_conv, br_sc = _in_proj(
            xh, w_main[l], w_vt[l], w_f[l], b_f[l], sc_w[l], conv_w[l], conv_b[l][None, :],
            conv_ln_g[l][None, :], conv_ln_b[l][None, :], seq)
        br_att = _fox_attention(qf, kf, vt, th // seq, seq)
        x1, x1p, route, counts = _merge(xh, br_att, br_conv, br_sc, wg[l], b_gate[l][:, None, :],
                                        wb[l], wo[l], ln1_g[l][None, :], ln1_b[l][None, :], wr, br)
        pos, block_expert, n_used, fill_start, fill_len = _routing_tables(route, counts, n_blocks)
        xs = _sc_scatter_rows(x1p.reshape(ROW_CHUNKS * th, CHUNK), pos, ROW_CHUNKS * n_rows)
        xs = _fill_unrouted(xs, fill_start, fill_len).reshape(ROW_CHUNKS, n_rows, CHUNK)
        ys = _experts(xs, w_up, w_down, l, block_expert, n_used)
        yk = _sc_gather_rows(ys.reshape(ROW_CHUNKS * n_rows, CHUNK), pos)
        return _combine(x1, route, yk.reshape(2, ROW_CHUNKS, th, CHUNK),
                        ln2_g[l][None, :], ln2_b[l][None, :])

    for l in range(depth):
        x2 = layer(x2, l)
    return x2.reshape(batch, seq, d)
```

```python
import functools

import numpy as np
import jax
import jax.numpy as jnp
from jax import lax
from jax.experimental import pallas as pl
from jax.experimental.pallas import tpu as pltpu
from jax.experimental.pallas import tpu_sc as plsc

F32 = jnp.float32
BF16 = jnp.bfloat16

ATT_HEADS = 8
ATT_HEAD_DIM = 64
WIDTH = 512
CONV_TAPS = 31
SC_TAPS = 3
N_EXPERTS = 32
EXPERTS_PER_GROUP = 4
EXPERT_FF = 512
DEPTH = 4
DEEPNORM_ALPHA = (2 * DEPTH) ** 0.25
LN_EPS = 1e-5

LANES = 128
SUBLANES = 8
VMEM_LIMIT = 56 * 1024 * 1024

TM_PROJ = 512
TM_CONV = 512
CONV_HALO = 32
CONV_CHUNK = 64
TQ = 512
TK = 512
ATT_TILES_PER_BLOCK = 4
ATT_SUM_ROWS = 16
EXPERT_BLOCK = 512
TM_ROUTE = 512
SC_WINDOW = 128
ROW_CHUNKS = 2
CHUNK = 256
ROUTE_ROWS = 8
ZERO_ROWS = 256
NEG_BIG = -1e30
LOG2E = 1.4426950408889634


def _sigmoid(x):
    return 1.0 / (1.0 + jnp.exp(-x))


def _log_sigmoid(x):
    return jnp.minimum(x, 0.0) - jnp.log(1.0 + jnp.exp(-jnp.abs(x)))


def _split3(x):
    hi = x.astype(BF16)
    r1 = x - hi.astype(F32)
    mid = r1.astype(BF16)
    lo = (r1 - mid.astype(F32)).astype(BF16)
    return hi, mid, lo


def _pack_rows(x, ref):
    half = x.shape[1] // 2
    bits = pltpu.bitcast(x, jnp.uint32)
    bits = bits + (jnp.uint32(0x7FFF) + ((bits >> 16) & jnp.uint32(1)))
    words = (bits[:, :half] & jnp.uint32(0xFFFF0000)) | (bits[:, half:] >> 16)
    for c in range(ROW_CHUNKS):
        ref[c] = words[:, c * CHUNK:(c + 1) * CHUNK]


def _unpack_rows(ref):
    words = jnp.concatenate([ref[c] for c in range(ROW_CHUNKS)], axis=1)
    hi = pltpu.bitcast(words & jnp.uint32(0xFFFF0000), F32)
    lo = pltpu.bitcast(words << 16, F32)
    return jnp.concatenate([hi, lo], axis=1)


def _pack_pieces(x):
    lane = lax.broadcasted_iota(jnp.int32, x.shape, 1)
    hi, mid, lo = (p.astype(F32) for p in _split3(x))
    packed = jnp.where(lane < ATT_HEADS, hi,
                       jnp.where(lane < 2 * ATT_HEADS, pltpu.roll(mid, ATT_HEADS, axis=1),
                                 jnp.where(lane < 3 * ATT_HEADS,
                                           pltpu.roll(lo, 2 * ATT_HEADS, axis=1), 0.0)))
    return packed.astype(BF16)


def _layer_norm(y, g, b):
    mu = jnp.mean(y, axis=-1, keepdims=True)
    yc = y - mu
    var = jnp.mean(yc * yc, axis=-1, keepdims=True)
    return yc * lax.rsqrt(var + LN_EPS) * g + b


def _bias_placement():
    pairs = ATT_HEADS // 2
    place = np.zeros((LANES, 2 * pairs * LANES), np.float32)
    ones = np.zeros((1, 2 * pairs * LANES), np.float32)
    for p in range(pairs):
        for h in range(2):
            for piece in range(3):
                src = piece * ATT_HEADS + 2 * p + h
                place[src, p * LANES + 6 * h + 3 + piece] = 1.0
                place[src, (pairs + p) * LANES + 6 * h + piece] = -1.0
                ones[0, p * LANES + 6 * h + piece] = 1.0
                ones[0, (pairs + p) * LANES + 6 * h + 3 + piece] = 1.0
    return jnp.asarray(place, BF16), jnp.asarray(ones, F32)


def _in_proj_kernel(x_ref, wm_ref, wvt_ref, wf_ref, bf_ref, scw_ref, tri_ref, place_ref, ones_ref,
                    cw_ref, cb_ref, cg_ref, cbeta_ref,
                    qf_ref, kf_ref, vt_ref, brconv_ref, brsc_ref,
                    carry_c, carry_p, ext, pre, *, tiles_per_seq):
    i = pl.program_id(0)
    tm = x_ref.shape[0]

    @pl.when(i % tiles_per_seq == 0)
    def _():
        carry_c[...] = jnp.zeros_like(carry_c)
        carry_p[...] = jnp.zeros_like(carry_p)
        _conv31_reset(ext, tm)

    xb = x_ref[...].astype(BF16)

    def mm(j):
        return jnp.dot(xb, wm_ref[:, j * WIDTH:(j + 1) * WIDTH], preferred_element_type=F32)

    ext[CONV_HALO:CONV_HALO + tm, :] = mm(2) * _sigmoid(mm(3))
    for c0 in range(0, tm, CONV_CHUNK):
        _conv31_rows(c0, cw_ref, ext, pre)
    _conv31_finish(tm, cb_ref, cg_ref, cbeta_ref, brconv_ref, ext, pre)

    q = (mm(0) * (ATT_HEAD_DIM ** -0.5 * LOG2E)).astype(BF16)
    k = mm(1).astype(BF16)
    vt_ref[...] = lax.dot_general(wvt_ref[...], xb, (((1,), (1,)), ((), ())),
                                  preferred_element_type=F32).astype(BF16)

    p = mm(5) * mm(6)
    row = lax.broadcasted_iota(jnp.int32, p.shape, 0)
    prev = carry_p[...]
    p1 = jnp.where(row == 0, prev[7:8, :], pltpu.roll(p, 1, axis=0))
    p2 = pltpu.roll(p, 2, axis=0)
    p2 = jnp.where(row == 0, prev[6:7, :], jnp.where(row == 1, prev[7:8, :], p2))
    y = scw_ref[0:1, :] * p2 + scw_ref[1:2, :] * p1 + scw_ref[2:3, :] * p
    brsc_ref[...] = (mm(4) * y).astype(BF16)
    carry_p[...] = p[tm - SUBLANES:tm, :]

    zf = jnp.dot(xb, wf_ref[...], preferred_element_type=F32)
    ls = _log_sigmoid(zf + bf_ref[...])
    sums = jnp.dot(tri_ref[...], _pack_pieces(ls), preferred_element_type=F32)
    c = (sums + pltpu.roll(sums, LANES - ATT_HEADS, axis=1)
         + pltpu.roll(sums, LANES - 2 * ATT_HEADS, axis=1)) + carry_c[...]
    carry_c[...] = c[tm - 1:tm, :]

    bias = (jnp.dot(_pack_pieces(c * LOG2E), place_ref[...], preferred_element_type=F32)
            + ones_ref[...]).astype(BF16)
    pairs = ATT_HEADS // 2
    for p in range(pairs):
        lo_, hi_ = p * LANES, (p + 1) * LANES
        qf_ref[:, 2 * lo_:2 * lo_ + LANES] = q[:, lo_:hi_]
        qf_ref[:, 2 * lo_ + LANES:2 * hi_] = bias[:, lo_:hi_]
        kf_ref[:, 2 * lo_:2 * lo_ + LANES] = k[:, lo_:hi_]
        kf_ref[:, 2 * lo_ + LANES:2 * hi_] = bias[:, (pairs + p) * LANES:(pairs + p + 1) * LANES]


def _in_proj(x2, wm, wvt, wf, bf, scw, cw, cb, cg, cbeta, seq):
    t, d = x2.shape
    tm = min(TM_PROJ, seq)
    assert seq % tm == 0 and t % seq == 0 and tm % CONV_CHUNK == 0
    tri = jnp.tril(jnp.ones((tm, tm), F32)).astype(BF16)
    place, ones = _bias_placement()
    const = lambda shape: pl.BlockSpec(shape, lambda i: (0,) * len(shape))
    rows = lambda w: pl.BlockSpec((tm, w), lambda i: (i, 0))
    return pl.pallas_call(
        functools.partial(_in_proj_kernel, tiles_per_seq=seq // tm),
        grid=(t // tm,),
        in_specs=[rows(d), const(wm.shape), const(wvt.shape), const(wf.shape), const(bf.shape),
                  const(scw.shape), const(tri.shape), const(place.shape), const(ones.shape),
                  const(cw.shape), const(cb.shape), const(cg.shape), const(cbeta.shape)],
        out_specs=[rows(2 * WIDTH), rows(2 * WIDTH), pl.BlockSpec((WIDTH, tm), lambda i: (0, i)),
                   rows(WIDTH), rows(WIDTH)],
        out_shape=[jax.ShapeDtypeStruct((t, 2 * WIDTH), BF16),
                   jax.ShapeDtypeStruct((t, 2 * WIDTH), BF16),
                   jax.ShapeDtypeStruct((WIDTH, t), BF16),
                   jax.ShapeDtypeStruct((t, WIDTH), BF16),
                   jax.ShapeDtypeStruct((t, WIDTH), BF16)],
        scratch_shapes=[pltpu.VMEM((1, LANES), F32), pltpu.VMEM((SUBLANES, WIDTH), F32),
                        pltpu.VMEM((CONV_HALO + tm + SUBLANES, WIDTH), F32),
                        pltpu.VMEM((tm, WIDTH), F32)],
        compiler_params=pltpu.CompilerParams(dimension_semantics=("arbitrary",),
                                             vmem_limit_bytes=VMEM_LIMIT),
        name="in_proj",
    )(x2, wm, wvt, wf, bf, scw, tri, place, ones, cw, cb, cg, cbeta)


def _conv31_reset(ext, tm):
    ext[0:CONV_HALO, :] = jnp.zeros((CONV_HALO, WIDTH), F32)
    ext[CONV_HALO + tm:CONV_HALO + tm + SUBLANES, :] = jnp.zeros((SUBLANES, WIDTH), F32)


def _conv31_rows(c0, w_ref, ext, pre):
    base = CONV_HALO - (CONV_TAPS - 1)
    span = CONV_CHUNK + SUBLANES
    half = WIDTH // 2
    for l0 in (0, half):
        y = None
        for r in range(SUBLANES):
            z = None
            for a in range((base + CONV_TAPS - 1) // SUBLANES + 1):
                k = SUBLANES * a + r - base
                if 0 <= k < CONV_TAPS:
                    term = (w_ref[k:k + 1, l0:l0 + half]
                            * ext[c0 + SUBLANES * a:c0 + SUBLANES * a + span, l0:l0 + half])
                    z = term if z is None else z + term
            zr = z[r:r + CONV_CHUNK, :]
            y = zr if y is None else y + zr
        pre[c0:c0 + CONV_CHUNK, l0:l0 + half] = y


def _conv31_finish(tm, cb_ref, g_ref, b_ref, o_ref, ext, pre):
    for c0 in range(0, tm, CONV_CHUNK):
        y = _layer_norm(pre[c0:c0 + CONV_CHUNK, :] + cb_ref[...], g_ref[...], b_ref[...])
        o_ref[c0:c0 + CONV_CHUNK, :] = (y * _sigmoid(y)).astype(BF16)
    ext[0:CONV_HALO, :] = ext[tm:tm + CONV_HALO, :]


def _fox_kernel(qf_ref, kf_ref, vt_ref, o_ref, acc0_ref, acc1_ref, *, tq, tk):
    def q_tile(qi, _):
        _fox_q_tile(qi, qf_ref, kf_ref, vt_ref, o_ref, acc0_ref, acc1_ref, tq=tq, tk=tk)
        return 0

    lax.fori_loop(0, qf_ref.shape[0] // tq, q_tile, 0)


def _fox_q_tile(qi, qf_ref, kf_ref, vt_ref, o_ref, acc0_ref, acc1_ref, *, tq, tk):
    q_rows = pl.ds(pl.multiple_of(qi * tq, tq), tq)
    lane = lax.broadcasted_iota(jnp.int32, (tq, 2 * LANES), 1)
    bias_lane = lane - LANES
    qfull = qf_ref[q_rows, :]
    zero = jnp.zeros_like(qfull)
    qs = [jnp.where(((lane >= h * ATT_HEAD_DIM) & (lane < (h + 1) * ATT_HEAD_DIM))
                    | ((bias_lane >= 6 * h) & (bias_lane < 6 * h + 6)), qfull, zero)
          for h in range(2)]
    q_lo = qi * tq
    accs = (acc0_ref, acc1_ref)
    for acc in accs:
        acc[...] = jnp.zeros(acc.shape, F32)

    def tiles(jobs, carry):
        work = []
        for j, masked in jobs:
            start = pl.multiple_of(j * tk, tk)
            kfull = kf_ref[pl.ds(start, tk), :]
            ss = [lax.dot_general(kfull, qs[h], (((1,), (1,)), ((), ())),
                                  preferred_element_type=F32) for h in range(2)]
            work.append((start, masked, ss))
        carry = list(carry)
        for start, masked, ss in work:
            if masked:
                key = start + lax.broadcasted_iota(jnp.int32, ss[0].shape, 0)
                qry = q_lo + lax.broadcasted_iota(jnp.int32, ss[0].shape, 1)
                ss = [jnp.where(key <= qry, s, NEG_BIG) for s in ss]
            for h in range(2):
                m_old = carry[h]
                m_new = jnp.maximum(m_old, jnp.max(ss[h], axis=0, keepdims=True))
                a = jnp.exp2(m_old - m_new)
                p = jnp.exp2(ss[h] - m_new).astype(BF16)
                vt = jnp.concatenate(
                    [vt_ref[h * ATT_HEAD_DIM:(h + 1) * ATT_HEAD_DIM, pl.ds(start, tk)], ones_rows],
                    axis=0)
                accs[h][...] = a * accs[h][...] + jnp.dot(vt, p, preferred_element_type=F32)
                carry[h] = m_new
        return tuple(carry)

    ones_rows = jnp.where(lax.broadcasted_iota(jnp.int32, (ATT_SUM_ROWS, tk), 0) == 0,
                          1.0, 0.0).astype(BF16)
    init = tuple(jnp.full((1, tq), NEG_BIG, F32) for _ in range(2))
    n_full = q_lo // tk
    group = ATT_TILES_PER_BLOCK
    carry = lax.fori_loop(
        0, n_full // group,
        lambda j, c: tiles([(group * j + i, False) for i in range(group)], c), init)
    rest = n_full % group

    def tail(n):
        return lambda c: tiles([(n_full - n + i, False) for i in range(n)] + [(n_full, True)], c)

    lax.switch(rest, [tail(n) for n in range(group)], carry)
    d = ATT_HEAD_DIM
    out_t = jnp.concatenate([acc[0:d, :] / acc[d:d + 1, :] for acc in accs], axis=0)
    o_ref[q_rows, :] = out_t.T.astype(BF16)


def _fox_attention(qf, kf, vt, batch, seq):
    t = qf.shape[0]
    tq = min(TQ, seq)
    tk = min(TK, seq)
    assert seq % tq == 0 and seq % tk == 0 and tk % tq == 0
    pairs = ATT_HEADS // 2
    return pl.pallas_call(
        functools.partial(_fox_kernel, tq=tq, tk=tk),
        grid=(batch, pairs),
        in_specs=[pl.BlockSpec((seq, 2 * LANES), lambda b, hp: (b, hp)),
                  pl.BlockSpec((seq, 2 * LANES), lambda b, hp: (b, hp)),
                  pl.BlockSpec((LANES, seq), lambda b, hp: (hp, b))],
        out_specs=pl.BlockSpec((seq, LANES), lambda b, hp: (b, hp)),
        out_shape=jax.ShapeDtypeStruct((t, WIDTH), BF16),
        scratch_shapes=[pltpu.VMEM((ATT_HEAD_DIM + ATT_SUM_ROWS, tq), F32),
                        pltpu.VMEM((ATT_HEAD_DIM + ATT_SUM_ROWS, tq), F32)],
        compiler_params=pltpu.CompilerParams(
            dimension_semantics=("arbitrary", "arbitrary"),
            vmem_limit_bytes=VMEM_LIMIT),
        name="fox_attn",
    )(qf, kf, vt)


def _partner(x, row, bit):
    n = x.shape[0]
    up = pltpu.roll(x, n - bit, axis=0)
    down = pltpu.roll(x, bit, axis=0)
    return jnp.where((row & bit) == 0, up, down)


def _merge_kernel(x_ref, att_ref, conv_ref, sc_ref, wg_ref, bg_ref, wb_ref, wo_ref,
                  g1_ref, b1_ref, wr_ref, br_ref, triu_ref,
                  x1_ref, x1p_ref, route_ref, counts_ref, carry):
    i = pl.program_id(0)

    @pl.when(i == 0)
    def _():
        carry[...] = jnp.zeros_like(carry)

    x = x_ref[...]
    xb = x.astype(BF16)
    merged = None
    for n, br_ref_n in enumerate((att_ref, conv_ref, sc_ref)):
        gate = _sigmoid(jnp.dot(xb, wg_ref[n], preferred_element_type=F32) + bg_ref[n])
        term = gate * jnp.dot(br_ref_n[...], wb_ref[n], preferred_element_type=F32)
        merged = term if merged is None else merged + term
    h = jnp.dot(merged.astype(BF16), wo_ref[...], preferred_element_type=F32)
    x1 = _layer_norm(DEEPNORM_ALPHA * x + h, g1_ref[...], b1_ref[...])
    x1_ref[...] = x1
    _pack_rows(x1, x1p_ref)

    x_hi = x1.astype(BF16)
    x_lo = (x1 - x_hi.astype(F32)).astype(BF16)
    nt = lambda w, v: lax.dot_general(w, v, (((1,), (1,)), ((), ())), preferred_element_type=F32)
    logits = nt(wr_ref[0], x_hi) + nt(wr_ref[0], x_lo) + nt(wr_ref[1], x_hi) + br_ref[...]
    mx = jnp.max(logits, axis=0, keepdims=True)
    ex = jnp.exp(logits - mx)
    p = ex / jnp.sum(ex, axis=0, keepdims=True)

    row = lax.broadcasted_iota(jnp.int32, p.shape, 0)
    sub = row & (EXPERTS_PER_GROUP - 1)
    rank = jnp.zeros(p.shape, jnp.int32)
    for d in range(1, EXPERTS_PER_GROUP):
        below = pltpu.roll(p, d, axis=0)
        above = pltpu.roll(p, N_EXPERTS - d, axis=0)
        rank = rank + jnp.where((sub >= d) & (below >= p), 1, 0)
        rank = rank + jnp.where((sub + d < EXPERTS_PER_GROUP) & (above > p), 1, 0)
    top2 = rank < 2
    score = jnp.where(top2, p, 0.0)
    score = score + _partner(score, row, 1)
    score = score + _partner(score, row, 2)
    best = jnp.max(score, axis=0, keepdims=True)
    group = row >> 2
    best_group = jnp.min(jnp.where(score == best, group, N_EXPERTS), axis=0, keepdims=True)
    sel = top2 & (group == best_group)
    gate = jnp.where(sel, p / best, 0.0)

    sel_b = jnp.where(sel, 1.0, 0.0).astype(BF16)
    before = jnp.dot(sel_b, triu_ref[...], preferred_element_type=F32) + carry[...]
    tm = x.shape[0]
    new_carry = before[:, tm - 1:tm] + sel_b[:, tm - 1:tm].astype(F32)
    carry[...] = new_carry
    counts_ref[...] = new_carry

    e_lo = jnp.min(jnp.where(sel, row, N_EXPERTS), axis=0, keepdims=True)
    e_hi = jnp.max(jnp.where(sel, row, -1), axis=0, keepdims=True)
    is_lo = row == e_lo
    is_hi = row == e_hi
    pick = lambda m, a: jnp.sum(jnp.where(m, a, 0.0), axis=0, keepdims=True)
    zero = jnp.zeros_like(mx)
    route_ref[...] = jnp.concatenate(
        [e_lo.astype(F32), e_hi.astype(F32), pick(is_lo, before), pick(is_hi, before),
         pick(is_lo, gate), pick(is_hi, gate), zero, zero], axis=0)


def _merge(x2, att, conv, sc, wg, bg, wb, wo, g1, b1, wr, br):
    t, d = x2.shape
    tm = min(TM_PROJ, t)
    assert t % tm == 0
    triu = jnp.triu(jnp.ones((tm, tm), F32), 1).astype(BF16)
    const = lambda shape: pl.BlockSpec(shape, lambda i: (0,) * len(shape),
                                       pipeline_mode=pl.Buffered(1))
    rows = lambda w: pl.BlockSpec((tm, w), lambda i: (i, 0))
    return pl.pallas_call(
        _merge_kernel,
        grid=(t // tm,),
        in_specs=[rows(d), rows(WIDTH), rows(WIDTH), rows(WIDTH),
                  const(wg.shape), const(bg.shape), const(wb.shape), const(wo.shape),
                  const(g1.shape), const(b1.shape), const(wr.shape), const(br.shape),
                  const(triu.shape)],
        out_specs=[rows(d), pl.BlockSpec((ROW_CHUNKS, tm, CHUNK), lambda i: (0, i, 0)),
                   pl.BlockSpec((ROUTE_ROWS, tm), lambda i: (0, i)),
                   pl.BlockSpec((N_EXPERTS, 1), lambda i: (0, 0))],
        out_shape=[jax.ShapeDtypeStruct((t, d), F32),
                   jax.ShapeDtypeStruct((ROW_CHUNKS, t, CHUNK), jnp.uint32),
                   jax.ShapeDtypeStruct((ROUTE_ROWS, t), F32),
                   jax.ShapeDtypeStruct((N_EXPERTS, 1), F32)],
        scratch_shapes=[pltpu.VMEM((N_EXPERTS, 1), F32)],
        compiler_params=pltpu.CompilerParams(dimension_semantics=("arbitrary",),
                                             vmem_limit_bytes=VMEM_LIMIT),
        name="merge",
    )(x2, att, conv, sc, wg, bg, wb, wo, g1, b1, wr, br, triu)


def _sc_mesh():
    return plsc.VectorSubcoreMesh(core_axis_name="core", subcore_axis_name="subcore")


def _sc_scatter_rows(x, idx, n_rows):
    t, d = x.shape
    m = idx.shape[0]
    w = SC_WINDOW
    assert t % w == 0 and m % t == 0
    steps_per_pass = t // w

    @functools.partial(pl.kernel, out_type=jax.ShapeDtypeStruct((n_rows, d), x.dtype),
                       mesh=_sc_mesh(), scratch_types=[], name="sc_dispatch")
    def run(x_hbm, i_hbm, o_hbm):
        def body(x_vmem, i_vmem):
            pltpu.sync_copy(x_vmem, o_hbm.at[i_vmem.at[0]])

        pltpu.emit_pipeline(
            body,
            grid=(m // w,),
            in_specs=[pl.BlockSpec((w, d), index_map=lambda i: (i % steps_per_pass, 0)),
                      pl.BlockSpec((1, w), index_map=lambda i: (0, i))],
            out_specs=[],
            core_axis_name=("core", "subcore"),
            dimension_semantics=(pltpu.PARALLEL,),
            trace_scopes=False,
        )(x_hbm, i_hbm)

    return run(x, idx.reshape(1, m))


def _sc_gather_rows(table, idx):
    d = table.shape[1]
    m = idx.shape[0]
    w = SC_WINDOW
    assert m % w == 0

    @functools.partial(pl.kernel, out_type=jax.ShapeDtypeStruct((m, d), table.dtype),
                       mesh=_sc_mesh(), scratch_types=[], name="sc_gather")
    def run(t_hbm, i_hbm, o_hbm):
        def body(i_vmem, o_vmem):
            pltpu.sync_copy(t_hbm.at[i_vmem.at[0]], o_vmem)

        pltpu.emit_pipeline(
            body,
            grid=(m // w,),
            in_specs=[pl.BlockSpec((1, w), index_map=lambda i: (0, i))],
            out_specs=[pl.BlockSpec((w, d), index_map=lambda i: (i, 0))],
            core_axis_name=("core", "subcore"),
            dimension_semantics=(pltpu.PARALLEL,),
            trace_scopes=False,
        )(i_hbm, o_hbm)

    return run(table, idx.reshape(1, m))


def _fill_kernel(fs_ref, fl_ref, xs_in, xs_ref, zeros, sem):
    del xs_in
    zb = zeros.shape[0]
    zeros[...] = jnp.zeros_like(zeros)

    n_regions = fs_ref.shape[0]
    rows_per_chunk = xs_ref.shape[0] // ROW_CHUNKS

    def region(e, fn):
        start = fs_ref[e % n_regions] + (e // n_regions) * rows_per_chunk
        length = fl_ref[e % n_regions]
        head = jnp.minimum(length, (SUBLANES - start % SUBLANES) % SUBLANES)
        mid = start + head
        mid_len = (length - head) // SUBLANES * SUBLANES
        n_full = mid_len // zb
        rem = mid_len - n_full * zb
        part = mid + n_full * zb
        tail = mid + mid_len
        n_tail = length - head - mid_len

        def single(base):
            def body(r, _):
                fn(pltpu.make_async_copy(zeros.at[pl.ds(0, 1)],
                                         xs_ref.at[pl.ds(base + r, 1)], sem))
                return 0
            return body
        lax.fori_loop(0, head, single(start), 0)
        lax.fori_loop(0, n_tail, single(tail), 0)

        def full(c, _):
            off = pl.multiple_of(mid + c * zb, SUBLANES)
            fn(pltpu.make_async_copy(zeros, xs_ref.at[pl.ds(off, zb)], sem))
            return 0
        lax.fori_loop(0, n_full, full, 0)
        bit = zb // 2
        while bit >= SUBLANES:
            off = pl.multiple_of(part + (rem - rem % (2 * bit)), SUBLANES)

            @pl.when((rem & bit) != 0)
            def _(bit=bit, off=off):
                fn(pltpu.make_async_copy(zeros.at[pl.ds(0, bit)],
                                         xs_ref.at[pl.ds(off, bit)], sem))
            bit //= 2

    for fn in (lambda cp: cp.start(), lambda cp: cp.wait()):
        def body(e, _, fn=fn):
            region(e, fn)
            return 0
        lax.fori_loop(0, n_regions * ROW_CHUNKS, body, 0)


def _fill_unrouted(xs, fill_start, fill_len):
    n_rows, d = xs.shape
    return pl.pallas_call(
        _fill_kernel,
        grid_spec=pltpu.PrefetchScalarGridSpec(
            num_scalar_prefetch=2,
            grid=(1,),
            in_specs=[pl.BlockSpec(memory_space=pl.ANY)],
            out_specs=pl.BlockSpec(memory_space=pl.ANY),
            scratch_shapes=[pltpu.VMEM((ZERO_ROWS, d), xs.dtype), pltpu.SemaphoreType.DMA]),
        out_shape=jax.ShapeDtypeStruct((n_rows, d), xs.dtype),
        input_output_aliases={2: 0},
        compiler_params=pltpu.CompilerParams(dimension_semantics=("arbitrary",),
                                             vmem_limit_bytes=VMEM_LIMIT),
        name="fill_unrouted",
    )(fill_start, fill_len, xs)


def _expert_kernel(be_ref, nb_ref, xs_ref, wu_ref, wd_ref, ys_ref, wu_bf, wd_bf):
    b = pl.program_id(0)
    live = b < nb_ref[0]

    @pl.when(live & ((b == 0) | (be_ref[b] != be_ref[jnp.maximum(b, 1) - 1])))
    def _():
        wu_bf[...] = wu_ref[0, 0].astype(BF16)
        wd_bf[...] = wd_ref[0, 0].astype(BF16)

    @pl.when(live)
    def _():
        xb = _unpack_rows(xs_ref).astype(BF16)
        hidden = jnp.dot(xb, wu_bf[...], preferred_element_type=F32)
        a = hidden[:, :EXPERT_FF]
        g = hidden[:, EXPERT_FF:]
        act = (g * _sigmoid(g) * a).astype(BF16)
        _pack_rows(jnp.dot(act, wd_bf[...], preferred_element_type=F32), ys_ref)

    @pl.when(jnp.logical_not(live))
    def _():
        ys_ref[...] = jnp.zeros_like(ys_ref)


def _experts(xs, w_up, w_down, layer, block_expert, n_used):
    _, n_rows, d = xs.shape
    n_blocks = n_rows // EXPERT_BLOCK

    def live(b, nb):
        return jnp.minimum(b, nb[0] - 1)

    wspec = lambda w: pl.BlockSpec((1, 1) + w.shape[2:],
                                   lambda b, be, nb: (layer, be[live(b, nb)], 0, 0))
    return pl.pallas_call(
        _expert_kernel,
        grid_spec=pltpu.PrefetchScalarGridSpec(
            num_scalar_prefetch=2,
            grid=(n_blocks,),
            in_specs=[pl.BlockSpec((ROW_CHUNKS, EXPERT_BLOCK, d), lambda b, be, nb: (0, live(b, nb), 0)),
                      wspec(w_up), wspec(w_down)],
            out_specs=pl.BlockSpec((ROW_CHUNKS, EXPERT_BLOCK, d), lambda b, be, nb: (0, b, 0)),
            scratch_shapes=[pltpu.VMEM(w_up.shape[2:], BF16), pltpu.VMEM(w_down.shape[2:], BF16)]),
        out_shape=jax.ShapeDtypeStruct(xs.shape, xs.dtype),
        compiler_params=pltpu.CompilerParams(dimension_semantics=("arbitrary",),
                                             vmem_limit_bytes=VMEM_LIMIT),
        name="experts",
    )(block_expert, n_used, xs, w_up, w_down)


def _combine_kernel(x1_ref, route_ref, y0_ref, y1_ref, g2_ref, b2_ref, o_ref):
    route = route_ref[...].T
    m = route[:, 4:5] * _unpack_rows(y0_ref.at[0]) + route[:, 5:6] * _unpack_rows(y1_ref.at[0])
    o_ref[...] = _layer_norm(DEEPNORM_ALPHA * x1_ref[...] + m, g2_ref[...], b2_ref[...])


def _combine(x1, route, yk, g2, b2):
    t, d = x1.shape
    dc = yk.shape[-1]
    tm = min(TM_ROUTE, t)
    steps = t // tm
    const = lambda shape: pl.BlockSpec(shape, lambda i: (0,) * len(shape))
    return pl.pallas_call(
        _combine_kernel,
        grid=(steps,),
        in_specs=[pl.BlockSpec((tm, d), lambda i: (i, 0)),
                  pl.BlockSpec((ROUTE_ROWS, tm), lambda i: (0, i)),
                  pl.BlockSpec((1, ROW_CHUNKS, tm, dc), lambda i: (0, 0, i, 0)),
                  pl.BlockSpec((1, ROW_CHUNKS, tm, dc), lambda i: (1, 0, i, 0)),
                  const(g2.shape), const(b2.shape)],
        out_specs=pl.BlockSpec((tm, d), lambda i: (i, 0)),
        out_shape=jax.ShapeDtypeStruct((t, d), F32),
        compiler_params=pltpu.CompilerParams(dimension_semantics=("arbitrary",),
                                             vmem_limit_bytes=VMEM_LIMIT),
        name="combine",
    )(x1, route, yk, yk, g2, b2)


def _routing_tables(route, counts, n_blocks):
    t = route.shape[1]
    cnt = counts[:, 0].astype(jnp.int32)
    padded = (cnt + EXPERT_BLOCK - 1) // EXPERT_BLOCK * EXPERT_BLOCK
    pad_end = jnp.cumsum(padded)
    pad_start = pad_end - padded
    experts = route[0:2].astype(jnp.int32)
    ranks = route[2:4].astype(jnp.int32)
    ids = jnp.arange(N_EXPERTS, dtype=jnp.int32)
    base = jnp.sum(jnp.where(experts[:, :, None] == ids, pad_start, 0), axis=-1)
    n_rows = n_blocks * EXPERT_BLOCK
    chunk_base = jnp.arange(ROW_CHUNKS, dtype=jnp.int32) * n_rows
    pos = ((base + ranks)[:, None, :] + chunk_base[None, :, None]).reshape(2 * ROW_CHUNKS * t)
    blk = jnp.arange(n_blocks, dtype=jnp.int32) * EXPERT_BLOCK
    block_expert = jnp.minimum(jnp.sum((pad_end[None, :] <= blk[:, None]).astype(jnp.int32), axis=1),
                               N_EXPERTS - 1)
    n_used = (pad_end[-1:] // EXPERT_BLOCK).astype(jnp.int32)
    fill_start = jnp.concatenate([pad_start + cnt, pad_end[-1:]]).astype(jnp.int32)
    fill_len = jnp.concatenate([padded - cnt, n_rows - pad_end[-1:]]).astype(jnp.int32)
    return pos, block_expert, n_used, fill_start, fill_len


def kernel(x, w_in, b_forget, conv_w, conv_b, conv_ln_g, conv_ln_b, sc_w, w_branch, w_gate, b_gate,
           w_out, ln1_g, ln1_b, w_up, w_down, ln2_g, ln2_b, w_router, b_router):
    batch, seq, d = x.shape
    t = batch * seq
    depth = w_in.shape[0]
    x2 = x.reshape(t, d)

    nf = ATT_HEADS
    w_main = jnp.concatenate([w_in[:, :, :2 * WIDTH], w_in[:, :, 3 * WIDTH + nf:]], axis=2).astype(BF16)
    w_vt = jnp.swapaxes(w_in[:, :, 2 * WIDTH:3 * WIDTH], 1, 2).astype(BF16)
    w_f = jnp.pad(w_in[:, :, 3 * WIDTH:3 * WIDTH + nf], ((0, 0), (0, 0), (0, LANES - nf))).astype(BF16)
    b_f = jnp.pad(b_forget, ((0, 0), (0, LANES - nf)))[:, None, :]
    wg = w_gate.astype(BF16)
    wb = w_branch.astype(BF16)
    wo = w_out.astype(BF16)
    wr_t = w_router.T
    wr_hi = wr_t.astype(BF16)
    wr_lo = (wr_t - wr_hi.astype(F32)).astype(BF16)
    wr = jnp.stack([wr_hi, wr_lo])
    br = b_router[:, None]

    def layer(xh, l):
        th = xh.shape[0]
        n_blocks = (2 * th + N_EXPERTS * (EXPERT_BLOCK - 1) + EXPERT_BLOCK - 1) // EXPERT_BLOCK
        n_rows = n_blocks * EXPERT_BLOCK
        qf, kf, vt, br_conv, br_sc = _in_proj(
            xh, w_main[l], w_vt[l], w_f[l], b_f[l], sc_w[l], conv_w[l], conv_b[l][None, :],
            conv_ln_g[l][None, :], conv_ln_b[l][None, :], seq)
        br_att = _fox_attention(qf, kf, vt, th // seq, seq)
        x1, x1p, route, counts = _merge(xh, br_att, br_conv, br_sc, wg[l], b_gate[l][:, None, :],
                                        wb[l], wo[l], ln1_g[l][None, :], ln1_b[l][None, :], wr, br)
        pos, block_expert, n_used, fill_start, fill_len = _routing_tables(route, counts, n_blocks)
        xs = _sc_scatter_rows(x1p.reshape(ROW_CHUNKS * th, CHUNK), pos, ROW_CHUNKS * n_rows)
        xs = _fill_unrouted(xs, fill_start, fill_len).reshape(ROW_CHUNKS, n_rows, CHUNK)
        ys = _experts(xs, w_up, w_down, l, block_expert, n_used)
        yk = _sc_gather_rows(ys.reshape(ROW_CHUNKS * n_rows, CHUNK), pos)
        return _combine(x1, route, yk.reshape(2, ROW_CHUNKS, th, CHUNK),
                        ln2_g[l][None, :], ln2_b[l][None, :])

    for l in range(depth):
        x2 = layer(x2, l)
    return x2.reshape(batch, seq, d)
```

```python
import functools

import numpy as np
import jax
import jax.numpy as jnp
from jax import lax
from jax.experimental import pallas as pl
from jax.experimental.pallas import tpu as pltpu
from jax.experimental.pallas import tpu_sc as plsc

F32 = jnp.float32
BF16 = jnp.bfloat16

ATT_HEADS = 8
ATT_HEAD_DIM = 64
WIDTH = 512
CONV_TAPS = 31
SC_TAPS = 3
N_EXPERTS = 32
EXPERTS_PER_GROUP = 4
EXPERT_FF = 512
DEPTH = 4
DEEPNORM_ALPHA = (2 * DEPTH) ** 0.25
LN_EPS = 1e-5

LANES = 128
SUBLANES = 8
VMEM_LIMIT = 56 * 1024 * 1024

TM_PROJ = 512
TM_CONV = 512
CONV_HALO = 32
CONV_CHUNK = 64
TQ = 512
TK = 512
ATT_TILES_PER_BLOCK = 4
ATT_SUM_ROWS = 16
EXPERT_BLOCK = 512
TM_ROUTE = 512
SC_WINDOW = 128
ROW_CHUNKS = 2
CHUNK = 256
ROUTE_ROWS = 8
ZERO_ROWS = 256
NEG_BIG = -1e30
LOG2E = 1.4426950408889634


def _sigmoid(x):
    return 1.0 / (1.0 + jnp.exp(-x))


def _log_sigmoid(x):
    return jnp.minimum(x, 0.0) - jnp.log(1.0 + jnp.exp(-jnp.abs(x)))


def _split3(x):
    hi = x.astype(BF16)
    r1 = x - hi.astype(F32)
    mid = r1.astype(BF16)
    lo = (r1 - mid.astype(F32)).astype(BF16)
    return hi, mid, lo


def _pack_rows(x, ref):
    half = x.shape[1] // 2
    bits = pltpu.bitcast(x, jnp.uint32)
    bits = bits + (jnp.uint32(0x7FFF) + ((bits >> 16) & jnp.uint32(1)))
    words = (bits[:, :half] & jnp.uint32(0xFFFF0000)) | (bits[:, half:] >> 16)
    for c in range(ROW_CHUNKS):
        ref[c] = words[:, c * CHUNK:(c + 1) * CHUNK]


def _unpack_rows(ref):
    words = jnp.concatenate([ref[c] for c in range(ROW_CHUNKS)], axis=1)
    hi = pltpu.bitcast(words & jnp.uint32(0xFFFF0000), F32)
    lo = pltpu.bitcast(words << 16, F32)
    return jnp.concatenate([hi, lo], axis=1)


def _pack_pieces(x):
    lane = lax.broadcasted_iota(jnp.int32, x.shape, 1)
    hi, mid, lo = (p.astype(F32) for p in _split3(x))
    packed = jnp.where(lane < ATT_HEADS, hi,
                       jnp.where(lane < 2 * ATT_HEADS, pltpu.roll(mid, ATT_HEADS, axis=1),
                                 jnp.where(lane < 3 * ATT_HEADS,
                                           pltpu.roll(lo, 2 * ATT_HEADS, axis=1), 0.0)))
    return packed.astype(BF16)


def _layer_norm(y, g, b):
    mu = jnp.mean(y, axis=-1, keepdims=True)
    yc = y - mu
    var = jnp.mean(yc * yc, axis=-1, keepdims=True)
    return yc * lax.rsqrt(var + LN_EPS) * g + b


def _bias_placement():
    pairs = ATT_HEADS // 2
    place = np.zeros((LANES, 2 * pairs * LANES), np.float32)
    ones = np.zeros((1, 2 * pairs * LANES), np.float32)
    for p in range(pairs):
        for h in range(2):
            for piece in range(3):
                src = piece * ATT_HEADS + 2 * p + h
                place[src, p * LANES + 6 * h + 3 + piece] = 1.0
                place[src, (pairs + p) * LANES + 6 * h + piece] = -1.0
                ones[0, p * LANES + 6 * h + piece] = 1.0
                ones[0, (pairs + p) * LANES + 6 * h + 3 + piece] = 1.0
    return jnp.asarray(place, BF16), jnp.asarray(ones, F32)


def _in_proj_kernel(x_ref, wm_ref, wvt_ref, wf_ref, bf_ref, scw_ref, tri_ref, place_ref, ones_ref,
                    cw_ref, cb_ref, cg_ref, cbeta_ref,
                    qf_ref, kf_ref, vt_ref, brconv_ref, brsc_ref,
                    carry_c, carry_p, ext, pre, *, tiles_per_seq):
    i = pl.program_id(0)
    tm = x_ref.shape[0]

    @pl.when(i % tiles_per_seq == 0)
    def _():
        carry_c[...] = jnp.zeros_like(carry_c)
        carry_p[...] = jnp.zeros_like(carry_p)
        _conv31_reset(ext, tm)

    xb = x_ref[...].astype(BF16)

    def mm(j):
        return jnp.dot(xb, wm_ref[:, j * WIDTH:(j + 1) * WIDTH], preferred_element_type=F32)

    ext[CONV_HALO:CONV_HALO + tm, :] = mm(2) * _sigmoid(mm(3))
    for c0 in range(0, tm, CONV_CHUNK):
        _conv31_rows(c0, cw_ref, ext, pre)
    _conv31_finish(tm, cb_ref, cg_ref, cbeta_ref, brconv_ref, ext, pre)

    q = (mm(0) * (ATT_HEAD_DIM ** -0.5 * LOG2E)).astype(BF16)
    k = mm(1).astype(BF16)
    vt_ref[...] = lax.dot_general(wvt_ref[...], xb, (((1,), (1,)), ((), ())),
                                  preferred_element_type=F32).astype(BF16)

    p = mm(5) * mm(6)
    row = lax.broadcasted_iota(jnp.int32, p.shape, 0)
    prev = carry_p[...]
    p1 = jnp.where(row == 0, prev[7:8, :], pltpu.roll(p, 1, axis=0))
    p2 = pltpu.roll(p, 2, axis=0)
    p2 = jnp.where(row == 0, prev[6:7, :], jnp.where(row == 1, prev[7:8, :], p2))
    y = scw_ref[0:1, :] * p2 + scw_ref[1:2, :] * p1 + scw_ref[2:3, :] * p
    brsc_ref[...] = (mm(4) * y).astype(BF16)
    carry_p[...] = p[tm - SUBLANES:tm, :]

    zf = jnp.dot(xb, wf_ref[...], preferred_element_type=F32)
    ls = _log_sigmoid(zf + bf_ref[...])
    sums = jnp.dot(tri_ref[...], _pack_pieces(ls), preferred_element_type=F32)
    c = (sums + pltpu.roll(sums, LANES - ATT_HEADS, axis=1)
         + pltpu.roll(sums, LANES - 2 * ATT_HEADS, axis=1)) + carry_c[...]
    carry_c[...] = c[tm - 1:tm, :]

    bias = (jnp.dot(_pack_pieces(c * LOG2E), place_ref[...], preferred_element_type=F32)
            + ones_ref[...]).astype(BF16)
    pairs = ATT_HEADS // 2
    for p in range(pairs):
        lo_, hi_ = p * LANES, (p + 1) * LANES
        qf_ref[:, 2 * lo_:2 * lo_ + LANES] = q[:, lo_:hi_]
        qf_ref[:, 2 * lo_ + LANES:2 * hi_] = bias[:, lo_:hi_]
        kf_ref[:, 2 * lo_:2 * lo_ + LANES] = k[:, lo_:hi_]
        kf_ref[:, 2 * lo_ + LANES:2 * hi_] = bias[:, (pairs + p) * LANES:(pairs + p + 1) * LANES]


def _in_proj(x2, wm, wvt, wf, bf, scw, cw, cb, cg, cbeta, seq):
    t, d = x2.shape
    tm = min(TM_PROJ, seq)
    assert seq % tm == 0 and t % seq == 0 and tm % CONV_CHUNK == 0
    tri = jnp.tril(jnp.ones((tm, tm), F32)).astype(BF16)
    place, ones = _bias_placement()
    const = lambda shape: pl.BlockSpec(shape, lambda i: (0,) * len(shape))
    rows = lambda w: pl.BlockSpec((tm, w), lambda i: (i, 0))
    return pl.pallas_call(
        functools.partial(_in_proj_kernel, tiles_per_seq=seq // tm),
        grid=(t // tm,),
        in_specs=[rows(d), const(wm.shape), const(wvt.shape), const(wf.shape), const(bf.shape),
                  const(scw.shape), const(tri.shape), const(place.shape), const(ones.shape),
                  const(cw.shape), const(cb.shape), const(cg.shape), const(cbeta.shape)],
        out_specs=[rows(2 * WIDTH), rows(2 * WIDTH), pl.BlockSpec((WIDTH, tm), lambda i: (0, i)),
                   rows(WIDTH), rows(WIDTH)],
        out_shape=[jax.ShapeDtypeStruct((t, 2 * WIDTH), BF16),
                   jax.ShapeDtypeStruct((t, 2 * WIDTH), BF16),
                   jax.ShapeDtypeStruct((WIDTH, t), BF16),
                   jax.ShapeDtypeStruct((t, WIDTH), BF16),
                   jax.ShapeDtypeStruct((t, WIDTH), BF16)],
        scratch_shapes=[pltpu.VMEM((1, LANES), F32), pltpu.VMEM((SUBLANES, WIDTH), F32),
                        pltpu.VMEM((CONV_HALO + tm + SUBLANES, WIDTH), F32),
                        pltpu.VMEM((tm, WIDTH), F32)],
        compiler_params=pltpu.CompilerParams(dimension_semantics=("arbitrary",),
                                             vmem_limit_bytes=VMEM_LIMIT),
        name="in_proj",
    )(x2, wm, wvt, wf, bf, scw, tri, place, ones, cw, cb, cg, cbeta)


def _conv31_reset(ext, tm):
    ext[0:CONV_HALO, :] = jnp.zeros((CONV_HALO, WIDTH), F32)
    ext[CONV_HALO + tm:CONV_HALO + tm + SUBLANES, :] = jnp.zeros((SUBLANES, WIDTH), F32)


def _conv31_rows(c0, w_ref, ext, pre):
    base = CONV_HALO - (CONV_TAPS - 1)
    span = CONV_CHUNK + SUBLANES
    half = WIDTH // 2
    for l0 in (0, half):
        y = None
        for r in range(SUBLANES):
            z = None
            for a in range((base + CONV_TAPS - 1) // SUBLANES + 1):
                k = SUBLANES * a + r - base
                if 0 <= k < CONV_TAPS:
                    term = (w_ref[k:k + 1, l0:l0 + half]
                            * ext[c0 + SUBLANES * a:c0 + SUBLANES * a + span, l0:l0 + half])
                    z = term if z is None else z + term
            zr = z[r:r + CONV_CHUNK, :]
            y = zr if y is None else y + zr
        pre[c0:c0 + CONV_CHUNK, l0:l0 + half] = y


def _conv31_finish(tm, cb_ref, g_ref, b_ref, o_ref, ext, pre):
    for c0 in range(0, tm, CONV_CHUNK):
        y = _layer_norm(pre[c0:c0 + CONV_CHUNK, :] + cb_ref[...], g_ref[...], b_ref[...])
        o_ref[c0:c0 + CONV_CHUNK, :] = (y * _sigmoid(y)).astype(BF16)
    ext[0:CONV_HALO, :] = ext[tm:tm + CONV_HALO, :]


def _fox_kernel(qf_ref, kf_ref, vt_ref, o_ref, acc0_ref, acc1_ref, *, tq, tk):
    def q_tile(qi, _):
        _fox_q_tile(qi, qf_ref, kf_ref, vt_ref, o_ref, acc0_ref, acc1_ref, tq=tq, tk=tk)
        return 0

    lax.fori_loop(0, qf_ref.shape[0] // tq, q_tile, 0)


def _fox_q_tile(qi, qf_ref, kf_ref, vt_ref, o_ref, acc0_ref, acc1_ref, *, tq, tk):
    q_rows = pl.ds(pl.multiple_of(qi * tq, tq), tq)
    lane = lax.broadcasted_iota(jnp.int32, (tq, 2 * LANES), 1)
    bias_lane = lane - LANES
    qfull = qf_ref[q_rows, :]
    zero = jnp.zeros_like(qfull)
    qs = [jnp.where(((lane >= h * ATT_HEAD_DIM) & (lane < (h + 1) * ATT_HEAD_DIM))
                    | ((bias_lane >= 6 * h) & (bias_lane < 6 * h + 6)), qfull, zero)
          for h in range(2)]
    q_lo = qi * tq
    accs = (acc0_ref, acc1_ref)
    for acc in accs:
        acc[...] = jnp.zeros(acc.shape, F32)

    def tiles(jobs, carry):
        nt = lambda k, q: lax.dot_general(k, q, (((1,), (1,)), ((), ())),
                                          preferred_element_type=F32)
        work = []
        for j, masked in jobs:
            start = pl.multiple_of(j * tk, tk)
            if masked:
                ss = [(nt(kf_ref[pl.ds(start, half), :], qs[h]),
                       nt(kf_ref[pl.ds(start + half, half), :], qs[h][half:, :]))
                      for h in range(2)]
            else:
                kfull = kf_ref[pl.ds(start, tk), :]
                ss = [nt(kfull, qs[h]) for h in range(2)]
            work.append((start, masked, ss))

        def v_rows(h, first, n):
            return jnp.concatenate(
                [vt_ref[h * ATT_HEAD_DIM:(h + 1) * ATT_HEAD_DIM, pl.ds(first, n)],
                 ones_rows(n)], axis=0)

        carry = list(carry)
        for start, masked, ss in work:
            for h in range(2):
                m_old = carry[h]
                if masked:
                    s0, s1 = ss[h]
                    causal = lambda s, k0, q0: jnp.where(
                        k0 + lax.broadcasted_iota(jnp.int32, s.shape, 0)
                        <= q0 + lax.broadcasted_iota(jnp.int32, s.shape, 1), s, NEG_BIG)
                    s0 = causal(s0, start, q_lo)
                    s1 = causal(s1, start + half, q_lo + half)
                    mx = jnp.max(s0, axis=0, keepdims=True)
                    mx = jnp.concatenate(
                        [mx[:, :half],
                         jnp.maximum(mx[:, half:], jnp.max(s1, axis=0, keepdims=True))], axis=1)
                    m_new = jnp.maximum(m_old, mx)
                    a = jnp.exp2(m_old - m_new)
                    p0 = jnp.exp2(s0 - m_new).astype(BF16)
                    p1 = jnp.exp2(s1 - m_new[:, half:]).astype(BF16)
                    accs[h][...] = a * accs[h][...] + jnp.dot(v_rows(h, start, half), p0,
                                                              preferred_element_type=F32)
                    accs[h][:, half:] = accs[h][:, half:] + jnp.dot(
                        v_rows(h, start + half, half), p1, preferred_element_type=F32)
                else:
                    m_new = jnp.maximum(m_old, jnp.max(ss[h], axis=0, keepdims=True))
                    a = jnp.exp2(m_old - m_new)
                    p = jnp.exp2(ss[h] - m_new).astype(BF16)
                    accs[h][...] = a * accs[h][...] + jnp.dot(v_rows(h, start, tk), p,
                                                              preferred_element_type=F32)
                carry[h] = m_new
        return tuple(carry)

    assert tq == tk
    half = tk // 2

    ones_rows = lambda n: jnp.where(lax.broadcasted_iota(jnp.int32, (ATT_SUM_ROWS, n), 0) == 0,
                                    1.0, 0.0).astype(BF16)
    init = tuple(jnp.full((1, tq), NEG_BIG, F32) for _ in range(2))
    n_full = q_lo // tk
    group = ATT_TILES_PER_BLOCK
    carry = lax.fori_loop(
        0, n_full // group,
        lambda j, c: tiles([(group * j + i, False) for i in range(group)], c), init)
    rest = n_full % group

    def tail(n):
        return lambda c: tiles([(n_full - n + i, False) for i in range(n)] + [(n_full, True)], c)

    lax.switch(rest, [tail(n) for n in range(group)], carry)
    d = ATT_HEAD_DIM
    out_t = jnp.concatenate([acc[0:d, :] / acc[d:d + 1, :] for acc in accs], axis=0)
    o_ref[q_rows, :] = out_t.T.astype(BF16)


def _fox_attention(qf, kf, vt, batch, seq):
    t = qf.shape[0]
    tq = min(TQ, seq)
    tk = min(TK, seq)
    assert seq % tq == 0 and seq % tk == 0 and tk % tq == 0
    pairs = ATT_HEADS // 2
    return pl.pallas_call(
        functools.partial(_fox_kernel, tq=tq, tk=tk),
        grid=(batch, pairs),
        in_specs=[pl.BlockSpec((seq, 2 * LANES), lambda b, hp: (b, hp)),
                  pl.BlockSpec((seq, 2 * LANES), lambda b, hp: (b, hp)),
                  pl.BlockSpec((LANES, seq), lambda b, hp: (hp, b))],
        out_specs=pl.BlockSpec((seq, LANES), lambda b, hp: (b, hp)),
        out_shape=jax.ShapeDtypeStruct((t, WIDTH), BF16),
        scratch_shapes=[pltpu.VMEM((ATT_HEAD_DIM + ATT_SUM_ROWS, tq), F32),
                        pltpu.VMEM((ATT_HEAD_DIM + ATT_SUM_ROWS, tq), F32)],
        compiler_params=pltpu.CompilerParams(
            dimension_semantics=("arbitrary", "arbitrary"),
            vmem_limit_bytes=VMEM_LIMIT),
        name="fox_attn",
    )(qf, kf, vt)


def _partner(x, row, bit):
    n = x.shape[0]
    up = pltpu.roll(x, n - bit, axis=0)
    down = pltpu.roll(x, bit, axis=0)
    return jnp.where((row & bit) == 0, up, down)


def _merge_kernel(x_ref, att_ref, conv_ref, sc_ref, wg_ref, bg_ref, wb_ref, wo_ref,
                  g1_ref, b1_ref, wr_ref, br_ref, triu_ref,
                  x1_ref, x1p_ref, route_ref, counts_ref, carry):
    i = pl.program_id(0)

    @pl.when(i == 0)
    def _():
        carry[...] = jnp.zeros_like(carry)

    x = x_ref[...]
    xb = x.astype(BF16)
    merged = None
    for n, br_ref_n in enumerate((att_ref, conv_ref, sc_ref)):
        gate = _sigmoid(jnp.dot(xb, wg_ref[n], preferred_element_type=F32) + bg_ref[n])
        term = gate * jnp.dot(br_ref_n[...], wb_ref[n], preferred_element_type=F32)
        merged = term if merged is None else merged + term
    h = jnp.dot(merged.astype(BF16), wo_ref[...], preferred_element_type=F32)
    x1 = _layer_norm(DEEPNORM_ALPHA * x + h, g1_ref[...], b1_ref[...])
    x1_ref[...] = x1
    _pack_rows(x1, x1p_ref)

    x_hi = x1.astype(BF16)
    x_lo = (x1 - x_hi.astype(F32)).astype(BF16)
    logits = lax.dot_general(wr_ref[...], jnp.concatenate([x_hi, x_lo, x_hi], axis=1),
                             (((1,), (1,)), ((), ())), preferred_element_type=F32) + br_ref[...]
    mx = jnp.max(logits, axis=0, keepdims=True)
    ex = jnp.exp(logits - mx)
    p = ex / jnp.sum(ex, axis=0, keepdims=True)

    row = lax.broadcasted_iota(jnp.int32, p.shape, 0)
    sub = row & (EXPERTS_PER_GROUP - 1)
    rank = jnp.zeros(p.shape, jnp.int32)
    for d in range(1, EXPERTS_PER_GROUP):
        below = pltpu.roll(p, d, axis=0)
        above = pltpu.roll(p, N_EXPERTS - d, axis=0)
        rank = rank + jnp.where((sub >= d) & (below >= p), 1, 0)
        rank = rank + jnp.where((sub + d < EXPERTS_PER_GROUP) & (above > p), 1, 0)
    top2 = rank < 2
    score = jnp.where(top2, p, 0.0)
    score = score + _partner(score, row, 1)
    score = score + _partner(score, row, 2)
    best = jnp.max(score, axis=0, keepdims=True)
    group = row >> 2
    best_group = jnp.min(jnp.where(score == best, group, N_EXPERTS), axis=0, keepdims=True)
    sel = top2 & (group == best_group)
    gate = jnp.where(sel, p / best, 0.0)

    sel_b = jnp.where(sel, 1.0, 0.0).astype(BF16)
    before = jnp.dot(sel_b, triu_ref[...], preferred_element_type=F32) + carry[...]
    tm = x.shape[0]
    new_carry = before[:, tm - 1:tm] + sel_b[:, tm - 1:tm].astype(F32)
    carry[...] = new_carry
    counts_ref[...] = new_carry

    e_lo = jnp.min(jnp.where(sel, row, N_EXPERTS), axis=0, keepdims=True)
    e_hi = jnp.max(jnp.where(sel, row, -1), axis=0, keepdims=True)
    is_lo = row == e_lo
    is_hi = row == e_hi
    pick = lambda m, a: jnp.sum(jnp.where(m, a, 0.0), axis=0, keepdims=True)
    zero = jnp.zeros_like(mx)
    route_ref[...] = jnp.concatenate(
        [e_lo.astype(F32), e_hi.astype(F32), pick(is_lo, before), pick(is_hi, before),
         pick(is_lo, gate), pick(is_hi, gate), zero, zero], axis=0)


def _merge(x2, att, conv, sc, wg, bg, wb, wo, g1, b1, wr, br):
    t, d = x2.shape
    tm = min(TM_PROJ, t)
    assert t % tm == 0
    triu = jnp.triu(jnp.ones((tm, tm), F32), 1).astype(BF16)
    const = lambda shape: pl.BlockSpec(shape, lambda i: (0,) * len(shape),
                                       pipeline_mode=pl.Buffered(1))
    rows = lambda w: pl.BlockSpec((tm, w), lambda i: (i, 0))
    return pl.pallas_call(
        _merge_kernel,
        grid=(t // tm,),
        in_specs=[rows(d), rows(WIDTH), rows(WIDTH), rows(WIDTH),
                  const(wg.shape), const(bg.shape), const(wb.shape), const(wo.shape),
                  const(g1.shape), const(b1.shape), const(wr.shape), const(br.shape),
                  const(triu.shape)],
        out_specs=[rows(d), pl.BlockSpec((ROW_CHUNKS, tm, CHUNK), lambda i: (0, i, 0)),
                   pl.BlockSpec((ROUTE_ROWS, tm), lambda i: (0, i)),
                   pl.BlockSpec((N_EXPERTS, 1), lambda i: (0, 0))],
        out_shape=[jax.ShapeDtypeStruct((t, d), F32),
                   jax.ShapeDtypeStruct((ROW_CHUNKS, t, CHUNK), jnp.uint32),
                   jax.ShapeDtypeStruct((ROUTE_ROWS, t), F32),
                   jax.ShapeDtypeStruct((N_EXPERTS, 1), F32)],
        scratch_shapes=[pltpu.VMEM((N_EXPERTS, 1), F32)],
        compiler_params=pltpu.CompilerParams(dimension_semantics=("arbitrary",),
                                             vmem_limit_bytes=VMEM_LIMIT),
        name="merge",
    )(x2, att, conv, sc, wg, bg, wb, wo, g1, b1, wr, br, triu)


def _sc_mesh():
    return plsc.VectorSubcoreMesh(core_axis_name="core", subcore_axis_name="subcore")


def _sc_scatter_rows(x, idx, n_rows):
    t, d = x.shape
    m = idx.shape[0]
    w = SC_WINDOW
    assert t % w == 0 and m % t == 0
    steps_per_pass = t // w

    @functools.partial(pl.kernel, out_type=jax.ShapeDtypeStruct((n_rows, d), x.dtype),
                       mesh=_sc_mesh(), scratch_types=[], name="sc_dispatch")
    def run(x_hbm, i_hbm, o_hbm):
        def body(x_vmem, i_vmem):
            pltpu.sync_copy(x_vmem, o_hbm.at[i_vmem.at[0]])

        pltpu.emit_pipeline(
            body,
            grid=(m // w,),
            in_specs=[pl.BlockSpec((w, d), index_map=lambda i: (i % steps_per_pass, 0)),
                      pl.BlockSpec((1, w), index_map=lambda i: (0, i))],
            out_specs=[],
            core_axis_name=("core", "subcore"),
            dimension_semantics=(pltpu.PARALLEL,),
            trace_scopes=False,
        )(x_hbm, i_hbm)

    return run(x, idx.reshape(1, m))


def _sc_gather_rows(table, idx):
    d = table.shape[1]
    m = idx.shape[0]
    w = SC_WINDOW
    assert m % w == 0

    @functools.partial(pl.kernel, out_type=jax.ShapeDtypeStruct((m, d), table.dtype),
                       mesh=_sc_mesh(), scratch_types=[], name="sc_gather")
    def run(t_hbm, i_hbm, o_hbm):
        def body(i_vmem, o_vmem):
            pltpu.sync_copy(t_hbm.at[i_vmem.at[0]], o_vmem)

        pltpu.emit_pipeline(
            body,
            grid=(m // w,),
            in_specs=[pl.BlockSpec((1, w), index_map=lambda i: (0, i))],
            out_specs=[pl.BlockSpec((w, d), index_map=lambda i: (i, 0))],
            core_axis_name=("core", "subcore"),
            dimension_semantics=(pltpu.PARALLEL,),
            trace_scopes=False,
        )(i_hbm, o_hbm)

    return run(table, idx.reshape(1, m))


def _fill_kernel(fs_ref, fl_ref, xs_in, xs_ref, zeros, sem):
    del xs_in
    zb = zeros.shape[0]
    zeros[...] = jnp.zeros_like(zeros)

    n_regions = fs_ref.shape[0]
    rows_per_chunk = xs_ref.shape[0] // ROW_CHUNKS

    def region(e, fn):
        start = fs_ref[e % n_regions] + (e // n_regions) * rows_per_chunk
        length = fl_ref[e % n_regions]
        head = jnp.minimum(length, (SUBLANES - start % SUBLANES) % SUBLANES)
        mid = start + head
        mid_len = (length - head) // SUBLANES * SUBLANES
        n_full = mid_len // zb
        rem = mid_len - n_full * zb
        part = mid + n_full * zb
        tail = mid + mid_len
        n_tail = length - head - mid_len

        def single(base):
            def body(r, _):
                fn(pltpu.make_async_copy(zeros.at[pl.ds(0, 1)],
                                         xs_ref.at[pl.ds(base + r, 1)], sem))
                return 0
            return body
        lax.fori_loop(0, head, single(start), 0)
        lax.fori_loop(0, n_tail, single(tail), 0)

        def full(c, _):
            off = pl.multiple_of(mid + c * zb, SUBLANES)
            fn(pltpu.make_async_copy(zeros, xs_ref.at[pl.ds(off, zb)], sem))
            return 0
        lax.fori_loop(0, n_full, full, 0)
        bit = zb // 2
        while bit >= SUBLANES:
            off = pl.multiple_of(part + (rem - rem % (2 * bit)), SUBLANES)

            @pl.when((rem & bit) != 0)
            def _(bit=bit, off=off):
                fn(pltpu.make_async_copy(zeros.at[pl.ds(0, bit)],
                                         xs_ref.at[pl.ds(off, bit)], sem))
            bit //= 2

    for fn in (lambda cp: cp.start(), lambda cp: cp.wait()):
        def body(e, _, fn=fn):
            region(e, fn)
            return 0
        lax.fori_loop(0, n_regions * ROW_CHUNKS, body, 0)


def _fill_unrouted(xs, fill_start, fill_len):
    n_rows, d = xs.shape
    return pl.pallas_call(
        _fill_kernel,
        grid_spec=pltpu.PrefetchScalarGridSpec(
            num_scalar_prefetch=2,
            grid=(1,),
            in_specs=[pl.BlockSpec(memory_space=pl.ANY)],
            out_specs=pl.BlockSpec(memory_space=pl.ANY),
            scratch_shapes=[pltpu.VMEM((ZERO_ROWS, d), xs.dtype), pltpu.SemaphoreType.DMA]),
        out_shape=jax.ShapeDtypeStruct((n_rows, d), xs.dtype),
        input_output_aliases={2: 0},
        compiler_params=pltpu.CompilerParams(dimension_semantics=("arbitrary",),
                                             vmem_limit_bytes=VMEM_LIMIT),
        name="fill_unrouted",
    )(fill_start, fill_len, xs)


def _expert_kernel(be_ref, nb_ref, xs_ref, wu_ref, wd_ref, ys_ref, wu_bf, wd_bf):
    b = pl.program_id(0)
    live = b < nb_ref[0]

    @pl.when(live & ((b == 0) | (be_ref[b] != be_ref[jnp.maximum(b, 1) - 1])))
    def _():
        wu_bf[...] = wu_ref[0, 0].astype(BF16)
        wd_bf[...] = wd_ref[0, 0].astype(BF16)

    @pl.when(live)
    def _():
        xb = _unpack_rows(xs_ref).astype(BF16)
        hidden = jnp.dot(xb, wu_bf[...], preferred_element_type=F32)
        a = hidden[:, :EXPERT_FF]
        g = hidden[:, EXPERT_FF:]
        act = (g * _sigmoid(g) * a).astype(BF16)
        _pack_rows(jnp.dot(act, wd_bf[...], preferred_element_type=F32), ys_ref)

    @pl.when(jnp.logical_not(live))
    def _():
        ys_ref[...] = jnp.zeros_like(ys_ref)


def _experts(xs, w_up, w_down, layer, block_expert, n_used):
    _, n_rows, d = xs.shape
    n_blocks = n_rows // EXPERT_BLOCK

    def live(b, nb):
        return jnp.minimum(b, nb[0] - 1)

    wspec = lambda w: pl.BlockSpec((1, 1) + w.shape[2:],
                                   lambda b, be, nb: (layer, be[live(b, nb)], 0, 0))
    return pl.pallas_call(
        _expert_kernel,
        grid_spec=pltpu.PrefetchScalarGridSpec(
            num_scalar_prefetch=2,
            grid=(n_blocks,),
            in_specs=[pl.BlockSpec((ROW_CHUNKS, EXPERT_BLOCK, d), lambda b, be, nb: (0, live(b, nb), 0)),
                      wspec(w_up), wspec(w_down)],
            out_specs=pl.BlockSpec((ROW_CHUNKS, EXPERT_BLOCK, d), lambda b, be, nb: (0, b, 0)),
            scratch_shapes=[pltpu.VMEM(w_up.shape[2:], BF16), pltpu.VMEM(w_down.shape[2:], BF16)]),
        out_shape=jax.ShapeDtypeStruct(xs.shape, xs.dtype),
        compiler_params=pltpu.CompilerParams(dimension_semantics=("arbitrary",),
                                             vmem_limit_bytes=VMEM_LIMIT),
        name="experts",
    )(block_expert, n_used, xs, w_up, w_down)


def _combine_kernel(x1_ref, route_ref, y0_ref, y1_ref, g2_ref, b2_ref, o_ref):
    route = route_ref[...].T
    m = route[:, 4:5] * _unpack_rows(y0_ref.at[0]) + route[:, 5:6] * _unpack_rows(y1_ref.at[0])
    o_ref[...] = _layer_norm(DEEPNORM_ALPHA * x1_ref[...] + m, g2_ref[...], b2_ref[...])


def _combine(x1, route, yk, g2, b2):
    t, d = x1.shape
    dc = yk.shape[-1]
    tm = min(TM_ROUTE, t)
    steps = t // tm
    const = lambda shape: pl.BlockSpec(shape, lambda i: (0,) * len(shape))
    return pl.pallas_call(
        _combine_kernel,
        grid=(steps,),
        in_specs=[pl.BlockSpec((tm, d), lambda i: (i, 0)),
                  pl.BlockSpec((ROUTE_ROWS, tm), lambda i: (0, i)),
                  pl.BlockSpec((1, ROW_CHUNKS, tm, dc), lambda i: (0, 0, i, 0)),
                  pl.BlockSpec((1, ROW_CHUNKS, tm, dc), lambda i: (1, 0, i, 0)),
                  const(g2.shape), const(b2.shape)],
        out_specs=pl.BlockSpec((tm, d), lambda i: (i, 0)),
        out_shape=jax.ShapeDtypeStruct((t, d), F32),
        compiler_params=pltpu.CompilerParams(dimension_semantics=("arbitrary",),
                                             vmem_limit_bytes=VMEM_LIMIT),
        name="combine",
    )(x1, route, yk, yk, g2, b2)


def _routing_tables(route, counts, n_blocks):
    t = route.shape[1]
    cnt = counts[:, 0].astype(jnp.int32)
    padded = (cnt + EXPERT_BLOCK - 1) // EXPERT_BLOCK * EXPERT_BLOCK
    pad_end = jnp.cumsum(padded)
    pad_start = pad_end - padded
    experts = route[0:2].astype(jnp.int32)
    ranks = route[2:4].astype(jnp.int32)
    ids = jnp.arange(N_EXPERTS, dtype=jnp.int32)
    base = jnp.sum(jnp.where(experts[:, :, None] == ids, pad_start, 0), axis=-1)
    n_rows = n_blocks * EXPERT_BLOCK
    chunk_base = jnp.arange(ROW_CHUNKS, dtype=jnp.int32) * n_rows
    pos = ((base + ranks)[:, None, :] + chunk_base[None, :, None]).reshape(2 * ROW_CHUNKS * t)
    blk = jnp.arange(n_blocks, dtype=jnp.int32) * EXPERT_BLOCK
    block_expert = jnp.minimum(jnp.sum((pad_end[None, :] <= blk[:, None]).astype(jnp.int32), axis=1),
                               N_EXPERTS - 1)
    n_used = (pad_end[-1:] // EXPERT_BLOCK).astype(jnp.int32)
    fill_start = jnp.concatenate([pad_start + cnt, pad_end[-1:]]).astype(jnp.int32)
    fill_len = jnp.concatenate([padded - cnt, n_rows - pad_end[-1:]]).astype(jnp.int32)
    return pos, block_expert, n_used, fill_start, fill_len


def kernel(x, w_in, b_forget, conv_w, conv_b, conv_ln_g, conv_ln_b, sc_w, w_branch, w_gate, b_gate,
           w_out, ln1_g, ln1_b, w_up, w_down, ln2_g, ln2_b, w_router, b_router):
    batch, seq, d = x.shape
    t = batch * seq
    depth = w_in.shape[0]
    x2 = x.reshape(t, d)

    nf = ATT_HEADS
    w_main = jnp.concatenate([w_in[:, :, :2 * WIDTH], w_in[:, :, 3 * WIDTH + nf:]], axis=2).astype(BF16)
    w_vt = jnp.swapaxes(w_in[:, :, 2 * WIDTH:3 * WIDTH], 1, 2).astype(BF16)
    w_f = jnp.pad(w_in[:, :, 3 * WIDTH:3 * WIDTH + nf], ((0, 0), (0, 0), (0, LANES - nf))).astype(BF16)
    b_f = jnp.pad(b_forget, ((0, 0), (0, LANES - nf)))[:, None, :]
    wg = w_gate.astype(BF16)
    wb = w_branch.astype(BF16)
    wo = w_out.astype(BF16)
    wr_t = w_router.T
    wr_hi = wr_t.astype(BF16)
    wr_lo = (wr_t - wr_hi.astype(F32)).astype(BF16)
    wr = jnp.concatenate([wr_hi, wr_hi, wr_lo], axis=1)
    br = b_router[:, None]

    def layer(xh, l):
        th = xh.shape[0]
        n_blocks = (2 * th + N_EXPERTS * (EXPERT_BLOCK - 1) + EXPERT_BLOCK - 1) // EXPERT_BLOCK
        n_rows = n_blocks * EXPERT_BLOCK
        qf, kf, vt, br_conv, br_sc = _in_proj(
            xh, w_main[l], w_vt[l], w_f[l], b_f[l], sc_w[l], conv_w[l], conv_b[l][None, :],
            conv_ln_g[l][None, :], conv_ln_b[l][None, :], seq)
        br_att = _fox_attention(qf, kf, vt, th // seq, seq)
        x1, x1p, route, counts = _merge(xh, br_att, br_conv, br_sc, wg[l], b_gate[l][:, None, :],
                                        wb[l], wo[l], ln1_g[l][None, :], ln1_b[l][None, :], wr, br)
        pos, block_expert, n_used, fill_start, fill_len = _routing_tables(route, counts, n_blocks)
        xs = _sc_scatter_rows(x1p.reshape(ROW_CHUNKS * th, CHUNK), pos, ROW_CHUNKS * n_rows)
        xs = _fill_unrouted(xs, fill_start, fill_len).reshape(ROW_CHUNKS, n_rows, CHUNK)
        ys = _experts(xs, w_up, w_down, l, block_expert, n_used)
        yk = _sc_gather_rows(ys.reshape(ROW_CHUNKS * n_rows, CHUNK), pos)
        return _combine(x1, route, yk.reshape(2, ROW_CHUNKS, th, CHUNK),
                        ln2_g[l][None, :], ln2_b[l][None, :])

    for l in range(depth):
        x2 = layer(x2, l)
    return x2.reshape(batch, seq, d)
```

```python
import functools

import numpy as np
import jax
import jax.numpy as jnp
from jax import lax
from jax.experimental import pallas as pl
from jax.experimental.pallas import tpu as pltpu
from jax.experimental.pallas import tpu_sc as plsc

F32 = jnp.float32
BF16 = jnp.bfloat16

ATT_HEADS = 8
ATT_HEAD_DIM = 64
WIDTH = 512
CONV_TAPS = 31
SC_TAPS = 3
N_EXPERTS = 32
EXPERTS_PER_GROUP = 4
EXPERT_FF = 512
DEPTH = 4
DEEPNORM_ALPHA = (2 * DEPTH) ** 0.25
LN_EPS = 1e-5

LANES = 128
SUBLANES = 8
VMEM_LIMIT = 56 * 1024 * 1024

TM_PROJ = 512
TM_CONV = 512
CONV_HALO = 32
CONV_CHUNK = 64
TQ = 512
TK = 512
ATT_TILES_PER_BLOCK = 4
ATT_SUM_ROWS = 16
EXPERT_BLOCK = 512
TM_ROUTE = 512
SC_WINDOW = 128
ROW_CHUNKS = 2
CHUNK = 256
ROUTE_ROWS = 8
ZERO_ROWS = 256
NEG_BIG = -1e30
LOG2E = 1.4426950408889634


def _sigmoid(x):
    return 1.0 / (1.0 + jnp.exp(-x))


def _log_sigmoid(x):
    return jnp.minimum(x, 0.0) - jnp.log(1.0 + jnp.exp(-jnp.abs(x)))


def _split3(x):
    hi = x.astype(BF16)
    r1 = x - hi.astype(F32)
    mid = r1.astype(BF16)
    lo = (r1 - mid.astype(F32)).astype(BF16)
    return hi, mid, lo


def _pack_rows(x, ref):
    half = x.shape[1] // 2
    bits = pltpu.bitcast(x, jnp.uint32)
    bits = bits + (jnp.uint32(0x7FFF) + ((bits >> 16) & jnp.uint32(1)))
    words = (bits[:, :half] & jnp.uint32(0xFFFF0000)) | (bits[:, half:] >> 16)
    for c in range(ROW_CHUNKS):
        ref[c] = words[:, c * CHUNK:(c + 1) * CHUNK]


def _unpack_rows(ref):
    words = jnp.concatenate([ref[c] for c in range(ROW_CHUNKS)], axis=1)
    hi = pltpu.bitcast(words & jnp.uint32(0xFFFF0000), F32)
    lo = pltpu.bitcast(words << 16, F32)
    return jnp.concatenate([hi, lo], axis=1)


def _pack_pieces(x):
    lane = lax.broadcasted_iota(jnp.int32, x.shape, 1)
    hi, mid, lo = (p.astype(F32) for p in _split3(x))
    packed = jnp.where(lane < ATT_HEADS, hi,
                       jnp.where(lane < 2 * ATT_HEADS, pltpu.roll(mid, ATT_HEADS, axis=1),
                                 jnp.where(lane < 3 * ATT_HEADS,
                                           pltpu.roll(lo, 2 * ATT_HEADS, axis=1), 0.0)))
    return packed.astype(BF16)


def _layer_norm(y, g, b):
    mu = jnp.mean(y, axis=-1, keepdims=True)
    yc = y - mu
    var = jnp.mean(yc * yc, axis=-1, keepdims=True)
    return yc * lax.rsqrt(var + LN_EPS) * g + b


def _bias_placement():
    pairs = ATT_HEADS // 2
    place = np.zeros((LANES, 2 * pairs * LANES), np.float32)
    ones = np.zeros((1, 2 * pairs * LANES), np.float32)
    for p in range(pairs):
        for h in range(2):
            for piece in range(3):
                src = piece * ATT_HEADS + 2 * p + h
                place[src, p * LANES + 6 * h + 3 + piece] = 1.0
                place[src, (pairs + p) * LANES + 6 * h + piece] = -1.0
                ones[0, p * LANES + 6 * h + piece] = 1.0
                ones[0, (pairs + p) * LANES + 6 * h + 3 + piece] = 1.0
    return jnp.asarray(place, BF16), jnp.asarray(ones, F32)


def _in_proj_kernel(x_ref, wm_ref, wvt_ref, wf_ref, bf_ref, scw_ref, tri_ref, place_ref, ones_ref,
                    cw_ref, cb_ref, cg_ref, cbeta_ref,
                    qf_ref, kf_ref, vt_ref, brconv_ref, brsc_ref,
                    carry_c, carry_p, ext, pre, *, tiles_per_seq):
    i = pl.program_id(0)
    tm = x_ref.shape[0]

    @pl.when(i % tiles_per_seq == 0)
    def _():
        carry_c[...] = jnp.zeros_like(carry_c)
        carry_p[...] = jnp.zeros_like(carry_p)
        _conv31_reset(ext, tm)

    xb = x_ref[...].astype(BF16)

    def mm(j):
        return jnp.dot(xb, wm_ref[:, j * WIDTH:(j + 1) * WIDTH], preferred_element_type=F32)

    ext[CONV_HALO:CONV_HALO + tm, :] = mm(2) * _sigmoid(mm(3))
    for c0 in range(0, tm, CONV_CHUNK):
        _conv31_rows(c0, cw_ref, ext, pre)
    _conv31_finish(tm, cb_ref, cg_ref, cbeta_ref, brconv_ref, ext, pre)

    q = (mm(0) * (ATT_HEAD_DIM ** -0.5 * LOG2E)).astype(BF16)
    k = mm(1).astype(BF16)
    vt_ref[...] = lax.dot_general(wvt_ref[...], xb, (((1,), (1,)), ((), ())),
                                  preferred_element_type=F32).astype(BF16)

    p = mm(5) * mm(6)
    row = lax.broadcasted_iota(jnp.int32, p.shape, 0)
    prev = carry_p[...]
    p1 = jnp.where(row == 0, prev[7:8, :], pltpu.roll(p, 1, axis=0))
    p2 = pltpu.roll(p, 2, axis=0)
    p2 = jnp.where(row == 0, prev[6:7, :], jnp.where(row == 1, prev[7:8, :], p2))
    y = scw_ref[0:1, :] * p2 + scw_ref[1:2, :] * p1 + scw_ref[2:3, :] * p
    brsc_ref[...] = (mm(4) * y).astype(BF16)
    carry_p[...] = p[tm - SUBLANES:tm, :]

    zf = jnp.dot(xb, wf_ref[...], preferred_element_type=F32)
    ls = _log_sigmoid(zf + bf_ref[...])
    sums = jnp.dot(tri_ref[...], _pack_pieces(ls), preferred_element_type=F32)
    c = (sums + pltpu.roll(sums, LANES - ATT_HEADS, axis=1)
         + pltpu.roll(sums, LANES - 2 * ATT_HEADS, axis=1)) + carry_c[...]
    carry_c[...] = c[tm - 1:tm, :]

    bias = (jnp.dot(_pack_pieces(c * LOG2E), place_ref[...], preferred_element_type=F32)
            + ones_ref[...]).astype(BF16)
    pairs = ATT_HEADS // 2
    for p in range(pairs):
        lo_, hi_ = p * LANES, (p + 1) * LANES
        qf_ref[:, 2 * lo_:2 * lo_ + LANES] = q[:, lo_:hi_]
        qf_ref[:, 2 * lo_ + LANES:2 * hi_] = bias[:, lo_:hi_]
        kf_ref[:, 2 * lo_:2 * lo_ + LANES] = k[:, lo_:hi_]
        kf_ref[:, 2 * lo_ + LANES:2 * hi_] = bias[:, (pairs + p) * LANES:(pairs + p + 1) * LANES]


def _in_proj(x2, wm, wvt, wf, bf, scw, cw, cb, cg, cbeta, seq):
    t, d = x2.shape
    tm = min(TM_PROJ, seq)
    assert seq % tm == 0 and t % seq == 0 and tm % CONV_CHUNK == 0
    tri = jnp.tril(jnp.ones((tm, tm), F32)).astype(BF16)
    place, ones = _bias_placement()
    const = lambda shape: pl.BlockSpec(shape, lambda i: (0,) * len(shape))
    rows = lambda w: pl.BlockSpec((tm, w), lambda i: (i, 0))
    return pl.pallas_call(
        functools.partial(_in_proj_kernel, tiles_per_seq=seq // tm),
        grid=(t // tm,),
        in_specs=[rows(d), const(wm.shape), const(wvt.shape), const(wf.shape), const(bf.shape),
                  const(scw.shape), const(tri.shape), const(place.shape), const(ones.shape),
                  const(cw.shape), const(cb.shape), const(cg.shape), const(cbeta.shape)],
        out_specs=[rows(2 * WIDTH), rows(2 * WIDTH), pl.BlockSpec((WIDTH, tm), lambda i: (0, i)),
                   rows(WIDTH), rows(WIDTH)],
        out_shape=[jax.ShapeDtypeStruct((t, 2 * WIDTH), BF16),
                   jax.ShapeDtypeStruct((t, 2 * WIDTH), BF16),
                   jax.ShapeDtypeStruct((WIDTH, t), BF16),
                   jax.ShapeDtypeStruct((t, WIDTH), BF16),
                   jax.ShapeDtypeStruct((t, WIDTH), BF16)],
        scratch_shapes=[pltpu.VMEM((1, LANES), F32), pltpu.VMEM((SUBLANES, WIDTH), F32),
                        pltpu.VMEM((CONV_HALO + tm + SUBLANES, WIDTH), F32),
                        pltpu.VMEM((tm, WIDTH), F32)],
        compiler_params=pltpu.CompilerParams(dimension_semantics=("arbitrary",),
                                             vmem_limit_bytes=VMEM_LIMIT),
        name="in_proj",
    )(x2, wm, wvt, wf, bf, scw, tri, place, ones, cw, cb, cg, cbeta)


def _conv31_reset(ext, tm):
    ext[0:CONV_HALO, :] = jnp.zeros((CONV_HALO, WIDTH), F32)
    ext[CONV_HALO + tm:CONV_HALO + tm + SUBLANES, :] = jnp.zeros((SUBLANES, WIDTH), F32)


def _conv31_rows(c0, w_ref, ext, pre):
    base = CONV_HALO - (CONV_TAPS - 1)
    span = CONV_CHUNK + SUBLANES
    half = WIDTH // 2
    for l0 in (0, half):
        y = None
        for r in range(SUBLANES):
            z = None
            for a in range((base + CONV_TAPS - 1) // SUBLANES + 1):
                k = SUBLANES * a + r - base
                if 0 <= k < CONV_TAPS:
                    term = (w_ref[k:k + 1, l0:l0 + half]
                            * ext[c0 + SUBLANES * a:c0 + SUBLANES * a + span, l0:l0 + half])
                    z = term if z is None else z + term
            zr = z[r:r + CONV_CHUNK, :]
            y = zr if y is None else y + zr
        pre[c0:c0 + CONV_CHUNK, l0:l0 + half] = y


def _conv31_finish(tm, cb_ref, g_ref, b_ref, o_ref, ext, pre):
    for c0 in range(0, tm, CONV_CHUNK):
        y = _layer_norm(pre[c0:c0 + CONV_CHUNK, :] + cb_ref[...], g_ref[...], b_ref[...])
        o_ref[c0:c0 + CONV_CHUNK, :] = (y * _sigmoid(y)).astype(BF16)
    ext[0:CONV_HALO, :] = ext[tm:tm + CONV_HALO, :]


def _fox_kernel(qf_ref, kf_ref, vt_ref, o_ref, acc0_ref, acc1_ref, *, tq, tk):
    def q_tile(qi, _):
        _fox_q_tile(qi, qf_ref, kf_ref, vt_ref, o_ref, acc0_ref, acc1_ref, tq=tq, tk=tk)
        return 0

    lax.fori_loop(0, qf_ref.shape[0] // tq, q_tile, 0)


def _fox_q_tile(qi, qf_ref, kf_ref, vt_ref, o_ref, acc0_ref, acc1_ref, *, tq, tk):
    q_rows = pl.ds(pl.multiple_of(qi * tq, tq), tq)
    lane = lax.broadcasted_iota(jnp.int32, (tq, 2 * LANES), 1)
    bias_lane = lane - LANES
    qfull = qf_ref[q_rows, :]
    zero = jnp.zeros_like(qfull)
    qs = [jnp.where(((lane >= h * ATT_HEAD_DIM) & (lane < (h + 1) * ATT_HEAD_DIM))
                    | ((bias_lane >= 6 * h) & (bias_lane < 6 * h + 6)), qfull, zero)
          for h in range(2)]
    q_lo = qi * tq
    accs = (acc0_ref, acc1_ref)
    for acc in accs:
        acc[...] = jnp.zeros(acc.shape, F32)

    def tiles(jobs, carry):
        nt = lambda k, q: lax.dot_general(k, q, (((1,), (1,)), ((), ())),
                                          preferred_element_type=F32)
        work = []
        for j, masked in jobs:
            start = pl.multiple_of(j * tk, tk)
            if masked:
                ss = [(nt(kf_ref[pl.ds(start, half), :], qs[h]),
                       nt(kf_ref[pl.ds(start + half, half), :], qs[h][half:, :]))
                      for h in range(2)]
            else:
                kfull = kf_ref[pl.ds(start, tk), :]
                ss = [nt(kfull, qs[h]) for h in range(2)]
            work.append((start, masked, ss))

        def v_rows(h, first, n):
            return jnp.concatenate(
                [vt_ref[h * ATT_HEAD_DIM:(h + 1) * ATT_HEAD_DIM, pl.ds(first, n)],
                 ones_rows(n)], axis=0)

        carry = list(carry)
        for start, masked, ss in work:
            for h in range(2):
                m_old = carry[h]
                if masked:
                    s0, s1 = ss[h]
                    causal = lambda s, k0, q0: jnp.where(
                        k0 + lax.broadcasted_iota(jnp.int32, s.shape, 0)
                        <= q0 + lax.broadcasted_iota(jnp.int32, s.shape, 1), s, NEG_BIG)
                    s0 = causal(s0, start, q_lo)
                    s1 = causal(s1, start + half, q_lo + half)
                    mx = jnp.max(s0, axis=0, keepdims=True)
                    mx = jnp.concatenate(
                        [mx[:, :half],
                         jnp.maximum(mx[:, half:], jnp.max(s1, axis=0, keepdims=True))], axis=1)
                    m_new = jnp.maximum(m_old, mx)
                    a = jnp.exp2(m_old - m_new)
                    p0 = jnp.exp2(s0 - m_new).astype(BF16)
                    p1 = jnp.exp2(s1 - m_new[:, half:]).astype(BF16)
                    accs[h][...] = a * accs[h][...] + jnp.dot(v_rows(h, start, half), p0,
                                                              preferred_element_type=F32)
                    accs[h][:, half:] = accs[h][:, half:] + jnp.dot(
                        v_rows(h, start + half, half), p1, preferred_element_type=F32)
                else:
                    m_new = jnp.maximum(m_old, jnp.max(ss[h], axis=0, keepdims=True))
                    a = jnp.exp2(m_old - m_new)
                    p = jnp.exp2(ss[h] - m_new).astype(BF16)
                    accs[h][...] = a * accs[h][...] + jnp.dot(v_rows(h, start, tk), p,
                                                              preferred_element_type=F32)
                carry[h] = m_new
        return tuple(carry)

    assert tq == tk
    half = tk // 2

    ones_rows = lambda n: jnp.where(lax.broadcasted_iota(jnp.int32, (ATT_SUM_ROWS, n), 0) == 0,
                                    1.0, 0.0).astype(BF16)
    init = tuple(jnp.full((1, tq), NEG_BIG, F32) for _ in range(2))
    n_full = q_lo // tk
    group = ATT_TILES_PER_BLOCK
    carry = lax.fori_loop(
        0, n_full // group,
        lambda j, c: tiles([(group * j + i, False) for i in range(group)], c), init)
    rest = n_full % group

    def tail(n):
        return lambda c: tiles([(n_full - n + i, False) for i in range(n)] + [(n_full, True)], c)

    lax.switch(rest, [tail(n) for n in range(group)], carry)
    d = ATT_HEAD_DIM
    out_t = jnp.concatenate([acc[0:d, :] / acc[d:d + 1, :] for acc in accs], axis=0)
    o_ref[q_rows, :] = out_t.T.astype(BF16)


def _fox_attention(qf, kf, vt, batch, seq):
    t = qf.shape[0]
    tq = min(TQ, seq)
    tk = min(TK, seq)
    assert seq % tq == 0 and seq % tk == 0 and tk % tq == 0
    pairs = ATT_HEADS // 2
    return pl.pallas_call(
        functools.partial(_fox_kernel, tq=tq, tk=tk),
        grid=(batch, pairs),
        in_specs=[pl.BlockSpec((seq, 2 * LANES), lambda b, hp: (b, hp)),
                  pl.BlockSpec((seq, 2 * LANES), lambda b, hp: (b, hp)),
                  pl.BlockSpec((LANES, seq), lambda b, hp: (hp, b))],
        out_specs=pl.BlockSpec((seq, LANES), lambda b, hp: (b, hp)),
        out_shape=jax.ShapeDtypeStruct((t, WIDTH), BF16),
        scratch_shapes=[pltpu.VMEM((ATT_HEAD_DIM + ATT_SUM_ROWS, tq), F32),
                        pltpu.VMEM((ATT_HEAD_DIM + ATT_SUM_ROWS, tq), F32)],
        compiler_params=pltpu.CompilerParams(
            dimension_semantics=("arbitrary", "arbitrary"),
            vmem_limit_bytes=VMEM_LIMIT),
        name="fox_attn",
    )(qf, kf, vt)


def _partner(x, row, bit):
    n = x.shape[0]
    up = pltpu.roll(x, n - bit, axis=0)
    down = pltpu.roll(x, bit, axis=0)
    return jnp.where((row & bit) == 0, up, down)


def _merge_kernel(x_ref, att_ref, conv_ref, sc_ref, wg_ref, bg_ref, wb_ref, wo_ref,
                  g1_ref, b1_ref, wr_ref, br_ref, triu_ref,
                  x1_ref, x1p_ref, route_ref, counts_ref, carry):
    i = pl.program_id(0)

    @pl.when(i == 0)
    def _():
        carry[...] = jnp.zeros_like(carry)

    x = x_ref[...]
    xb = x.astype(BF16)
    merged = None
    for n, br_ref_n in enumerate((att_ref, conv_ref, sc_ref)):
        gate = _sigmoid(jnp.dot(xb, wg_ref[n], preferred_element_type=F32) + bg_ref[n])
        term = gate * jnp.dot(br_ref_n[...], wb_ref[n], preferred_element_type=F32)
        merged = term if merged is None else merged + term
    h = jnp.dot(merged.astype(BF16), wo_ref[...], preferred_element_type=F32)
    x1 = _layer_norm(DEEPNORM_ALPHA * x + h, g1_ref[...], b1_ref[...])
    x1_ref[...] = x1
    _pack_rows(x1, x1p_ref)

    x_hi = x1.astype(BF16)
    x_lo = (x1 - x_hi.astype(F32)).astype(BF16)
    logits = lax.dot_general(wr_ref[...], jnp.concatenate([x_hi, x_lo, x_hi], axis=1),
                             (((1,), (1,)), ((), ())), preferred_element_type=F32) + br_ref[...]
    mx = jnp.max(logits, axis=0, keepdims=True)
    ex = jnp.exp(logits - mx)
    p = ex / jnp.sum(ex, axis=0, keepdims=True)

    row = lax.broadcasted_iota(jnp.int32, p.shape, 0)
    sub = row & (EXPERTS_PER_GROUP - 1)
    rank = jnp.zeros(p.shape, jnp.int32)
    for d in range(1, EXPERTS_PER_GROUP):
        below = pltpu.roll(p, d, axis=0)
        above = pltpu.roll(p, N_EXPERTS - d, axis=0)
        rank = rank + jnp.where((sub >= d) & (below >= p), 1, 0)
        rank = rank + jnp.where((sub + d < EXPERTS_PER_GROUP) & (above > p), 1, 0)
    top2 = rank < 2
    score = jnp.where(top2, p, 0.0)
    score = score + _partner(score, row, 1)
    score = score + _partner(score, row, 2)
    best = jnp.max(score, axis=0, keepdims=True)
    group = row >> 2
    best_group = jnp.min(jnp.where(score == best, group, N_EXPERTS), axis=0, keepdims=True)
    sel = top2 & (group == best_group)
    gate = jnp.where(sel, p / best, 0.0)

    sel_b = jnp.where(sel, 1.0, 0.0).astype(BF16)
    before = jnp.dot(sel_b, triu_ref[...], preferred_element_type=F32) + carry[...]
    tm = x.shape[0]
    new_carry = before[:, tm - 1:tm] + sel_b[:, tm - 1:tm].astype(F32)
    carry[...] = new_carry
    counts_ref[...] = new_carry

    e_lo = jnp.min(jnp.where(sel, row, N_EXPERTS), axis=0, keepdims=True)
    e_hi = jnp.max(jnp.where(sel, row, -1), axis=0, keepdims=True)
    is_lo = row == e_lo
    is_hi = row == e_hi
    pick = lambda m, a: jnp.sum(jnp.where(m, a, 0.0), axis=0, keepdims=True)
    zero = jnp.zeros_like(mx)
    route_ref[...] = jnp.concatenate(
        [e_lo.astype(F32), e_hi.astype(F32), pick(is_lo, before), pick(is_hi, before),
         pick(is_lo, gate), pick(is_hi, gate), zero, zero], axis=0)


def _merge(x2, att, conv, sc, wg, bg, wb, wo, g1, b1, wr, br):
    t, d = x2.shape
    tm = min(TM_PROJ, t)
    assert t % tm == 0
    triu = jnp.triu(jnp.ones((tm, tm), F32), 1).astype(BF16)
    const = lambda shape: pl.BlockSpec(shape, lambda i: (0,) * len(shape),
                                       pipeline_mode=pl.Buffered(1))
    rows = lambda w: pl.BlockSpec((tm, w), lambda i: (i, 0))
    return pl.pallas_call(
        _merge_kernel,
        grid=(t // tm,),
        in_specs=[rows(d), rows(WIDTH), rows(WIDTH), rows(WIDTH),
                  const(wg.shape), const(bg.shape), const(wb.shape), const(wo.shape),
                  const(g1.shape), const(b1.shape), const(wr.shape), const(br.shape),
                  const(triu.shape)],
        out_specs=[rows(d), pl.BlockSpec((ROW_CHUNKS, tm, CHUNK), lambda i: (0, i, 0)),
                   pl.BlockSpec((ROUTE_ROWS, tm), lambda i: (0, i)),
                   pl.BlockSpec((N_EXPERTS, 1), lambda i: (0, 0))],
        out_shape=[jax.ShapeDtypeStruct((t, d), F32),
                   jax.ShapeDtypeStruct((ROW_CHUNKS, t, CHUNK), jnp.uint32),
                   jax.ShapeDtypeStruct((ROUTE_ROWS, t), F32),
                   jax.ShapeDtypeStruct((N_EXPERTS, 1), F32)],
        scratch_shapes=[pltpu.VMEM((N_EXPERTS, 1), F32)],
        compiler_params=pltpu.CompilerParams(dimension_semantics=("arbitrary",),
                                             vmem_limit_bytes=VMEM_LIMIT),
        name="merge",
    )(x2, att, conv, sc, wg, bg, wb, wo, g1, b1, wr, br, triu)


def _sc_mesh():
    return plsc.VectorSubcoreMesh(core_axis_name="core", subcore_axis_name="subcore")


def _sc_scatter_rows(x, idx, n_rows):
    t, d = x.shape
    m = idx.shape[0]
    w = SC_WINDOW
    assert t % w == 0 and m % t == 0
    steps_per_pass = t // w

    @functools.partial(pl.kernel, out_type=jax.ShapeDtypeStruct((n_rows, d), x.dtype),
                       mesh=_sc_mesh(), scratch_types=[], name="sc_dispatch")
    def run(x_hbm, i_hbm, o_hbm):
        def body(x_vmem, i_vmem):
            pltpu.sync_copy(x_vmem, o_hbm.at[i_vmem.at[0]])

        pltpu.emit_pipeline(
            body,
            grid=(m // w,),
            in_specs=[pl.BlockSpec((w, d), index_map=lambda i: (i % steps_per_pass, 0)),
                      pl.BlockSpec((1, w), index_map=lambda i: (0, i))],
            out_specs=[],
            core_axis_name=("core", "subcore"),
            dimension_semantics=(pltpu.PARALLEL,),
            trace_scopes=False,
        )(x_hbm, i_hbm)

    return run(x, idx.reshape(1, m))


def _sc_gather_rows(table, idx):
    d = table.shape[1]
    m = idx.shape[0]
    w = SC_WINDOW
    assert m % w == 0

    @functools.partial(pl.kernel, out_type=jax.ShapeDtypeStruct((m, d), table.dtype),
                       mesh=_sc_mesh(), scratch_types=[], name="sc_gather")
    def run(t_hbm, i_hbm, o_hbm):
        def body(i_vmem, o_vmem):
            pltpu.sync_copy(t_hbm.at[i_vmem.at[0]], o_vmem)

        pltpu.emit_pipeline(
            body,
            grid=(m // w,),
            in_specs=[pl.BlockSpec((1, w), index_map=lambda i: (0, i))],
            out_specs=[pl.BlockSpec((w, d), index_map=lambda i: (i, 0))],
            core_axis_name=("core", "subcore"),
            dimension_semantics=(pltpu.PARALLEL,),
            trace_scopes=False,
        )(i_hbm, o_hbm)

    return run(table, idx.reshape(1, m))


def _fill_kernel(fs_ref, fl_ref, xs_in, xs_ref, zeros, sem):
    del xs_in
    zb = zeros.shape[0]
    zeros[...] = jnp.zeros_like(zeros)

    n_regions = fs_ref.shape[0]
    rows_per_chunk = xs_ref.shape[0] // ROW_CHUNKS

    def region(e, fn):
        start = fs_ref[e % n_regions] + (e // n_regions) * rows_per_chunk
        length = fl_ref[e % n_regions]
        head = jnp.minimum(length, (SUBLANES - start % SUBLANES) % SUBLANES)
        mid = start + head
        mid_len = (length - head) // SUBLANES * SUBLANES
        n_full = mid_len // zb
        rem = mid_len - n_full * zb
        part = mid + n_full * zb
        tail = mid + mid_len
        n_tail = length - head - mid_len

        def single(base):
            def body(r, _):
                fn(pltpu.make_async_copy(zeros.at[pl.ds(0, 1)],
                                         xs_ref.at[pl.ds(base + r, 1)], sem))
                return 0
            return body
        lax.fori_loop(0, head, single(start), 0)
        lax.fori_loop(0, n_tail, single(tail), 0)

        def full(c, _):
            off = pl.multiple_of(mid + c * zb, SUBLANES)
            fn(pltpu.make_async_copy(zeros, xs_ref.at[pl.ds(off, zb)], sem))
            return 0
        lax.fori_loop(0, n_full, full, 0)
        bit = zb // 2
        while bit >= SUBLANES:
            off = pl.multiple_of(part + (rem - rem % (2 * bit)), SUBLANES)

            @pl.when((rem & bit) != 0)
            def _(bit=bit, off=off):
                fn(pltpu.make_async_copy(zeros.at[pl.ds(0, bit)],
                                         xs_ref.at[pl.ds(off, bit)], sem))
            bit //= 2

    for fn in (lambda cp: cp.start(), lambda cp: cp.wait()):
        def body(e, _, fn=fn):
            region(e, fn)
            return 0
        lax.fori_loop(0, n_regions * ROW_CHUNKS, body, 0)


def _fill_unrouted(xs, fill_start, fill_len):
    n_rows, d = xs.shape
    return pl.pallas_call(
        _fill_kernel,
        grid_spec=pltpu.PrefetchScalarGridSpec(
            num_scalar_prefetch=2,
            grid=(1,),
            in_specs=[pl.BlockSpec(memory_space=pl.ANY)],
            out_specs=pl.BlockSpec(memory_space=pl.ANY),
            scratch_shapes=[pltpu.VMEM((ZERO_ROWS, d), xs.dtype), pltpu.SemaphoreType.DMA]),
        out_shape=jax.ShapeDtypeStruct((n_rows, d), xs.dtype),
        input_output_aliases={2: 0},
        compiler_params=pltpu.CompilerParams(dimension_semantics=("arbitrary",),
                                             vmem_limit_bytes=VMEM_LIMIT),
        name="fill_unrouted",
    )(fill_start, fill_len, xs)


def _expert_kernel(be_ref, nb_ref, xs_ref, wu_ref, wd_ref, ys_ref, wu_bf, wd_bf):
    b = pl.program_id(0)
    live = b < nb_ref[0]

    @pl.when(live & ((b == 0) | (be_ref[b] != be_ref[jnp.maximum(b, 1) - 1])))
    def _():
        wu_bf[...] = wu_ref[0, 0].astype(BF16)
        wd_bf[...] = wd_ref[0, 0].astype(BF16)

    @pl.when(live)
    def _():
        xb = _unpack_rows(xs_ref).astype(BF16)
        y = None
        for c0 in range(0, EXPERT_FF, EXPERT_FF // 2):
            c1 = c0 + EXPERT_FF // 2
            a = jnp.dot(xb, wu_bf[:, c0:c1], preferred_element_type=F32)
            g = jnp.dot(xb, wu_bf[:, EXPERT_FF + c0:EXPERT_FF + c1], preferred_element_type=F32)
            act = (g * _sigmoid(g) * a).astype(BF16)
            part = jnp.dot(act, wd_bf[c0:c1, :], preferred_element_type=F32)
            y = part if y is None else y + part
        _pack_rows(y, ys_ref)

    @pl.when(jnp.logical_not(live))
    def _():
        ys_ref[...] = jnp.zeros_like(ys_ref)


def _experts(xs, w_up, w_down, layer, block_expert, n_used):
    _, n_rows, d = xs.shape
    n_blocks = n_rows // EXPERT_BLOCK

    def live(b, nb):
        return jnp.minimum(b, nb[0] - 1)

    wspec = lambda w: pl.BlockSpec((1, 1) + w.shape[2:],
                                   lambda b, be, nb: (layer, be[live(b, nb)], 0, 0))
    return pl.pallas_call(
        _expert_kernel,
        grid_spec=pltpu.PrefetchScalarGridSpec(
            num_scalar_prefetch=2,
            grid=(n_blocks,),
            in_specs=[pl.BlockSpec((ROW_CHUNKS, EXPERT_BLOCK, d), lambda b, be, nb: (0, live(b, nb), 0)),
                      wspec(w_up), wspec(w_down)],
            out_specs=pl.BlockSpec((ROW_CHUNKS, EXPERT_BLOCK, d), lambda b, be, nb: (0, b, 0)),
            scratch_shapes=[pltpu.VMEM(w_up.shape[2:], BF16), pltpu.VMEM(w_down.shape[2:], BF16)]),
        out_shape=jax.ShapeDtypeStruct(xs.shape, xs.dtype),
        compiler_params=pltpu.CompilerParams(dimension_semantics=("arbitrary",),
                                             vmem_limit_bytes=VMEM_LIMIT),
        name="experts",
    )(block_expert, n_used, xs, w_up, w_down)


def _combine_kernel(x1_ref, route_ref, y0_ref, y1_ref, g2_ref, b2_ref, o_ref):
    route = route_ref[...].T
    m = route[:, 4:5] * _unpack_rows(y0_ref.at[0]) + route[:, 5:6] * _unpack_rows(y1_ref.at[0])
    o_ref[...] = _layer_norm(DEEPNORM_ALPHA * x1_ref[...] + m, g2_ref[...], b2_ref[...])


def _combine(x1, route, yk, g2, b2):
    t, d = x1.shape
    dc = yk.shape[-1]
    tm = min(TM_ROUTE, t)
    steps = t // tm
    const = lambda shape: pl.BlockSpec(shape, lambda i: (0,) * len(shape))
    return pl.pallas_call(
        _combine_kernel,
        grid=(steps,),
        in_specs=[pl.BlockSpec((tm, d), lambda i: (i, 0)),
                  pl.BlockSpec((ROUTE_ROWS, tm), lambda i: (0, i)),
                  pl.BlockSpec((1, ROW_CHUNKS, tm, dc), lambda i: (0, 0, i, 0)),
                  pl.BlockSpec((1, ROW_CHUNKS, tm, dc), lambda i: (1, 0, i, 0)),
                  const(g2.shape), const(b2.shape)],
        out_specs=pl.BlockSpec((tm, d), lambda i: (i, 0)),
        out_shape=jax.ShapeDtypeStruct((t, d), F32),
        compiler_params=pltpu.CompilerParams(dimension_semantics=("arbitrary",),
                                             vmem_limit_bytes=VMEM_LIMIT),
        name="combine",
    )(x1, route, yk, yk, g2, b2)


def _routing_tables(route, counts, n_blocks):
    t = route.shape[1]
    cnt = counts[:, 0].astype(jnp.int32)
    padded = (cnt + EXPERT_BLOCK - 1) // EXPERT_BLOCK * EXPERT_BLOCK
    pad_end = jnp.cumsum(padded)
    pad_start = pad_end - padded
    experts = route[0:2].astype(jnp.int32)
    ranks = route[2:4].astype(jnp.int32)
    ids = jnp.arange(N_EXPERTS, dtype=jnp.int32)
    base = jnp.sum(jnp.where(experts[:, :, None] == ids, pad_start, 0), axis=-1)
    n_rows = n_blocks * EXPERT_BLOCK
    chunk_base = jnp.arange(ROW_CHUNKS, dtype=jnp.int32) * n_rows
    pos = ((base + ranks)[:, None, :] + chunk_base[None, :, None]).reshape(2 * ROW_CHUNKS * t)
    blk = jnp.arange(n_blocks, dtype=jnp.int32) * EXPERT_BLOCK
    block_expert = jnp.minimum(jnp.sum((pad_end[None, :] <= blk[:, None]).astype(jnp.int32), axis=1),
                               N_EXPERTS - 1)
    n_used = (pad_end[-1:] // EXPERT_BLOCK).astype(jnp.int32)
    fill_start = jnp.concatenate([pad_start + cnt, pad_end[-1:]]).astype(jnp.int32)
    fill_len = jnp.concatenate([padded - cnt, n_rows - pad_end[-1:]]).astype(jnp.int32)
    return pos, block_expert, n_used, fill_start, fill_len


def kernel(x, w_in, b_forget, conv_w, conv_b, conv_ln_g, conv_ln_b, sc_w, w_branch, w_gate, b_gate,
           w_out, ln1_g, ln1_b, w_up, w_down, ln2_g, ln2_b, w_router, b_router):
    batch, seq, d = x.shape
    t = batch * seq
    depth = w_in.shape[0]
    x2 = x.reshape(t, d)

    nf = ATT_HEADS
    w_main = jnp.concatenate([w_in[:, :, :2 * WIDTH], w_in[:, :, 3 * WIDTH + nf:]], axis=2).astype(BF16)
    w_vt = jnp.swapaxes(w_in[:, :, 2 * WIDTH:3 * WIDTH], 1, 2).astype(BF16)
    w_f = jnp.pad(w_in[:, :, 3 * WIDTH:3 * WIDTH + nf], ((0, 0), (0, 0), (0, LANES - nf))).astype(BF16)
    b_f = jnp.pad(b_forget, ((0, 0), (0, LANES - nf)))[:, None, :]
    wg = w_gate.astype(BF16)
    wb = w_branch.astype(BF16)
    wo = w_out.astype(BF16)
    wr_t = w_router.T
    wr_hi = wr_t.astype(BF16)
    wr_lo = (wr_t - wr_hi.astype(F32)).astype(BF16)
    wr = jnp.concatenate([wr_hi, wr_hi, wr_lo], axis=1)
    br = b_router[:, None]

    def layer(xh, l):
        th = xh.shape[0]
        n_blocks = (2 * th + N_EXPERTS * (EXPERT_BLOCK - 1) + EXPERT_BLOCK - 1) // EXPERT_BLOCK
        n_rows = n_blocks * EXPERT_BLOCK
        qf, kf, vt, br_conv, br_sc = _in_proj(
            xh, w_main[l], w_vt[l], w_f[l], b_f[l], sc_w[l], conv_w[l], conv_b[l][None, :],
            conv_ln_g[l][None, :], conv_ln_b[l][None, :], seq)
        br_att = _fox_attention(qf, kf, vt, th // seq, seq)
        x1, x1p, route, counts = _merge(xh, br_att, br_conv, br_sc, wg[l], b_gate[l][:, None, :],
                                        wb[l], wo[l], ln1_g[l][None, :], ln1_b[l][None, :], wr, br)
        pos, block_expert, n_used, fill_start, fill_len = _routing_tables(route, counts, n_blocks)
        xs = _sc_scatter_rows(x1p.reshape(ROW_CHUNKS * th, CHUNK), pos, ROW_CHUNKS * n_rows)
        xs = _fill_unrouted(xs, fill_start, fill_len).reshape(ROW_CHUNKS, n_rows, CHUNK)
        ys = _experts(xs, w_up, w_down, l, block_expert, n_used)
        yk = _sc_gather_rows(ys.reshape(ROW_CHUNKS * n_rows, CHUNK), pos)
        return _combine(x1, route, yk.reshape(2, ROW_CHUNKS, th, CHUNK),
                        ln2_g[l][None, :], ln2_b[l][None, :])

    for l in range(depth):
        x2 = layer(x2, l)
    return x2.reshape(batch, seq, d)
```

```python
import functools

import numpy as np
import jax
import jax.numpy as jnp
from jax import lax
from jax.experimental import pallas as pl
from jax.experimental.pallas import tpu as pltpu
from jax.experimental.pallas import tpu_sc as plsc

F32 = jnp.float32
BF16 = jnp.bfloat16

ATT_HEADS = 8
ATT_HEAD_DIM = 64
WIDTH = 512
CONV_TAPS = 31
SC_TAPS = 3
N_EXPERTS = 32
EXPERTS_PER_GROUP = 4
EXPERT_FF = 512
DEPTH = 4
DEEPNORM_ALPHA = (2 * DEPTH) ** 0.25
LN_EPS = 1e-5

LANES = 128
SUBLANES = 8
VMEM_LIMIT = 56 * 1024 * 1024

TM_PROJ = 512
TM_CONV = 512
CONV_HALO = 32
CONV_CHUNK = 64
TQ = 512
TK = 512
ATT_TILES_PER_BLOCK = 4
ATT_SUM_ROWS = 16
EXPERT_BLOCK = 512
TM_ROUTE = 512
SC_WINDOW = 128
ROW_CHUNKS = 2
CHUNK = 256
ROUTE_ROWS = 8
ZERO_ROWS = 256
NEG_BIG = -1e30
LOG2E = 1.4426950408889634


def _sigmoid(x):
    return 1.0 / (1.0 + jnp.exp(-x))


def _log_sigmoid(x):
    return jnp.minimum(x, 0.0) - jnp.log(1.0 + jnp.exp(-jnp.abs(x)))


def _split3(x):
    hi = x.astype(BF16)
    r1 = x - hi.astype(F32)
    mid = r1.astype(BF16)
    lo = (r1 - mid.astype(F32)).astype(BF16)
    return hi, mid, lo


def _pack_rows(x, ref):
    half = x.shape[1] // 2
    bits = pltpu.bitcast(x, jnp.uint32)
    bits = bits + (jnp.uint32(0x7FFF) + ((bits >> 16) & jnp.uint32(1)))
    words = (bits[:, :half] & jnp.uint32(0xFFFF0000)) | (bits[:, half:] >> 16)
    for c in range(ROW_CHUNKS):
        ref[c] = words[:, c * CHUNK:(c + 1) * CHUNK]


def _unpack_rows(ref):
    words = jnp.concatenate([ref[c] for c in range(ROW_CHUNKS)], axis=1)
    hi = pltpu.bitcast(words & jnp.uint32(0xFFFF0000), F32)
    lo = pltpu.bitcast(words << 16, F32)
    return jnp.concatenate([hi, lo], axis=1)


def _pack_pieces(x):
    lane = lax.broadcasted_iota(jnp.int32, x.shape, 1)
    hi, mid, lo = (p.astype(F32) for p in _split3(x))
    packed = jnp.where(lane < ATT_HEADS, hi,
                       jnp.where(lane < 2 * ATT_HEADS, pltpu.roll(mid, ATT_HEADS, axis=1),
                                 jnp.where(lane < 3 * ATT_HEADS,
                                           pltpu.roll(lo, 2 * ATT_HEADS, axis=1), 0.0)))
    return packed.astype(BF16)


def _layer_norm(y, g, b):
    mu = jnp.mean(y, axis=-1, keepdims=True)
    yc = y - mu
    var = jnp.mean(yc * yc, axis=-1, keepdims=True)
    return yc * lax.rsqrt(var + LN_EPS) * g + b


def _bias_placement():
    pairs = ATT_HEADS // 2
    place = np.zeros((LANES, 2 * pairs * LANES), np.float32)
    ones = np.zeros((1, 2 * pairs * LANES), np.float32)
    for p in range(pairs):
        for h in range(2):
            for piece in range(3):
                src = piece * ATT_HEADS + 2 * p + h
                place[src, p * LANES + 6 * h + 3 + piece] = 1.0
                place[src, (pairs + p) * LANES + 6 * h + piece] = -1.0
                ones[0, p * LANES + 6 * h + piece] = 1.0
                ones[0, (pairs + p) * LANES + 6 * h + 3 + piece] = 1.0
    return jnp.asarray(place, BF16), jnp.asarray(ones, F32)


def _in_proj_kernel(x_ref, wm_ref, wvt_ref, wf_ref, bf_ref, scw_ref, tri_ref, place_ref, ones_ref,
                    cw_ref, cb_ref, cg_ref, cbeta_ref,
                    qf_ref, kf_ref, vt_ref, brconv_ref, brsc_ref,
                    carry_c, carry_p, ext, pre, *, tiles_per_seq):
    i = pl.program_id(0)
    tm = x_ref.shape[0]

    @pl.when(i % tiles_per_seq == 0)
    def _():
        carry_c[...] = jnp.zeros_like(carry_c)
        carry_p[...] = jnp.zeros_like(carry_p)
        _conv31_reset(ext, tm)

    xb = x_ref[...].astype(BF16)

    def mm(j):
        return jnp.dot(xb, wm_ref[:, j * WIDTH:(j + 1) * WIDTH], preferred_element_type=F32)

    ext[CONV_HALO:CONV_HALO + tm, :] = mm(2) * _sigmoid(mm(3))
    for c0 in range(0, tm, CONV_CHUNK):
        _conv31_rows(c0, cw_ref, ext, pre)
    _conv31_finish(tm, cb_ref, cg_ref, cbeta_ref, brconv_ref, ext, pre)

    q = (mm(0) * (ATT_HEAD_DIM ** -0.5 * LOG2E)).astype(BF16)
    k = mm(1).astype(BF16)
    vt_ref[...] = lax.dot_general(wvt_ref[...], xb, (((1,), (1,)), ((), ())),
                                  preferred_element_type=F32).astype(BF16)

    p = mm(5) * mm(6)
    row = lax.broadcasted_iota(jnp.int32, p.shape, 0)
    prev = carry_p[...]
    p1 = jnp.where(row == 0, prev[7:8, :], pltpu.roll(p, 1, axis=0))
    p2 = pltpu.roll(p, 2, axis=0)
    p2 = jnp.where(row == 0, prev[6:7, :], jnp.where(row == 1, prev[7:8, :], p2))
    y = scw_ref[0:1, :] * p2 + scw_ref[1:2, :] * p1 + scw_ref[2:3, :] * p
    brsc_ref[...] = (mm(4) * y).astype(BF16)
    carry_p[...] = p[tm - SUBLANES:tm, :]

    zf = jnp.dot(xb, wf_ref[...], preferred_element_type=F32)
    ls = _log_sigmoid(zf + bf_ref[...])
    sums = jnp.dot(tri_ref[...], _pack_pieces(ls), preferred_element_type=F32)
    c = (sums + pltpu.roll(sums, LANES - ATT_HEADS, axis=1)
         + pltpu.roll(sums, LANES - 2 * ATT_HEADS, axis=1)) + carry_c[...]
    carry_c[...] = c[tm - 1:tm, :]

    bias = (jnp.dot(_pack_pieces(c * LOG2E), place_ref[...], preferred_element_type=F32)
            + ones_ref[...]).astype(BF16)
    pairs = ATT_HEADS // 2
    for p in range(pairs):
        lo_, hi_ = p * LANES, (p + 1) * LANES
        qf_ref[:, 2 * lo_:2 * lo_ + LANES] = q[:, lo_:hi_]
        qf_ref[:, 2 * lo_ + LANES:2 * hi_] = bias[:, lo_:hi_]
        kf_ref[:, 2 * lo_:2 * lo_ + LANES] = k[:, lo_:hi_]
        kf_ref[:, 2 * lo_ + LANES:2 * hi_] = bias[:, (pairs + p) * LANES:(pairs + p + 1) * LANES]


def _in_proj(x2, wm, wvt, wf, bf, scw, cw, cb, cg, cbeta, seq):
    t, d = x2.shape
    tm = min(TM_PROJ, seq)
    assert seq % tm == 0 and t % seq == 0 and tm % CONV_CHUNK == 0
    tri = jnp.tril(jnp.ones((tm, tm), F32)).astype(BF16)
    place, ones = _bias_placement()
    const = lambda shape: pl.BlockSpec(shape, lambda i: (0,) * len(shape))
    rows = lambda w: pl.BlockSpec((tm, w), lambda i: (i, 0))
    return pl.pallas_call(
        functools.partial(_in_proj_kernel, tiles_per_seq=seq // tm),
        grid=(t // tm,),
        in_specs=[rows(d), const(wm.shape), const(wvt.shape), const(wf.shape), const(bf.shape),
                  const(scw.shape), const(tri.shape), const(place.shape), const(ones.shape),
                  const(cw.shape), const(cb.shape), const(cg.shape), const(cbeta.shape)],
        out_specs=[rows(2 * WIDTH), rows(2 * WIDTH), pl.BlockSpec((WIDTH, tm), lambda i: (0, i)),
                   rows(WIDTH), rows(WIDTH)],
        out_shape=[jax.ShapeDtypeStruct((t, 2 * WIDTH), BF16),
                   jax.ShapeDtypeStruct((t, 2 * WIDTH), BF16),
                   jax.ShapeDtypeStruct((WIDTH, t), BF16),
                   jax.ShapeDtypeStruct((t, WIDTH), BF16),
                   jax.ShapeDtypeStruct((t, WIDTH), BF16)],
        scratch_shapes=[pltpu.VMEM((1, LANES), F32), pltpu.VMEM((SUBLANES, WIDTH), F32),
                        pltpu.VMEM((CONV_HALO + tm + SUBLANES, WIDTH), F32),
                        pltpu.VMEM((tm, WIDTH), F32)],
        compiler_params=pltpu.CompilerParams(dimension_semantics=("arbitrary",),
                                             vmem_limit_bytes=VMEM_LIMIT),
        name="in_proj",
    )(x2, wm, wvt, wf, bf, scw, tri, place, ones, cw, cb, cg, cbeta)


def _conv31_reset(ext, tm):
    ext[0:CONV_HALO, :] = jnp.zeros((CONV_HALO, WIDTH), F32)
    ext[CONV_HALO + tm:CONV_HALO + tm + SUBLANES, :] = jnp.zeros((SUBLANES, WIDTH), F32)


def _conv31_rows(c0, w_ref, ext, pre):
    base = CONV_HALO - (CONV_TAPS - 1)
    span = CONV_CHUNK + SUBLANES
    half = WIDTH // 2
    for l0 in (0, half):
        y = None
        for r in range(SUBLANES):
            z = None
            for a in range((base + CONV_TAPS - 1) // SUBLANES + 1):
                k = SUBLANES * a + r - base
                if 0 <= k < CONV_TAPS:
                    term = (w_ref[k:k + 1, l0:l0 + half]
                            * ext[c0 + SUBLANES * a:c0 + SUBLANES * a + span, l0:l0 + half])
                    z = term if z is None else z + term
            zr = z[r:r + CONV_CHUNK, :]
            y = zr if y is None else y + zr
        pre[c0:c0 + CONV_CHUNK, l0:l0 + half] = y


def _conv31_finish(tm, cb_ref, g_ref, b_ref, o_ref, ext, pre):
    for c0 in range(0, tm, CONV_CHUNK):
        y = _layer_norm(pre[c0:c0 + CONV_CHUNK, :] + cb_ref[...], g_ref[...], b_ref[...])
        o_ref[c0:c0 + CONV_CHUNK, :] = (y * _sigmoid(y)).astype(BF16)
    ext[0:CONV_HALO, :] = ext[tm:tm + CONV_HALO, :]


def _fox_kernel(qf_ref, kf_ref, vt_ref, o_ref, acc0_ref, acc1_ref, *, tq, tk):
    def q_tile(qi, _):
        _fox_q_tile(qi, qf_ref, kf_ref, vt_ref, o_ref, acc0_ref, acc1_ref, tq=tq, tk=tk)
        return 0

    lax.fori_loop(0, qf_ref.shape[0] // tq, q_tile, 0)


def _fox_q_tile(qi, qf_ref, kf_ref, vt_ref, o_ref, acc0_ref, acc1_ref, *, tq, tk):
    q_rows = pl.ds(pl.multiple_of(qi * tq, tq), tq)
    lane = lax.broadcasted_iota(jnp.int32, (tq, 2 * LANES), 1)
    bias_lane = lane - LANES
    qfull = qf_ref[q_rows, :]
    zero = jnp.zeros_like(qfull)
    qs = [jnp.where(((lane >= h * ATT_HEAD_DIM) & (lane < (h + 1) * ATT_HEAD_DIM))
                    | ((bias_lane >= 6 * h) & (bias_lane < 6 * h + 6)), qfull, zero)
          for h in range(2)]
    q_lo = qi * tq
    accs = (acc0_ref, acc1_ref)
    for acc in accs:
        acc[...] = jnp.zeros(acc.shape, F32)

    def tiles(jobs, carry):
        nt = lambda k, q: lax.dot_general(k, q, (((1,), (1,)), ((), ())),
                                          preferred_element_type=F32)
        work = []
        for j, masked in jobs:
            start = pl.multiple_of(j * tk, tk)
            if masked:
                ss = [(nt(kf_ref[pl.ds(start, half), :], qs[h]),
                       nt(kf_ref[pl.ds(start + half, half), :], qs[h][half:, :]))
                      for h in range(2)]
            else:
                kfull = kf_ref[pl.ds(start, tk), :]
                ss = [nt(kfull, qs[h]) for h in range(2)]
            work.append((start, masked, ss))

        def v_rows(h, first, n):
            return jnp.concatenate(
                [vt_ref[h * ATT_HEAD_DIM:(h + 1) * ATT_HEAD_DIM, pl.ds(first, n)],
                 ones_rows(n)], axis=0)

        carry = list(carry)
        for start, masked, ss in work:
            for h in range(2):
                m_old = carry[h]
                if masked:
                    s0, s1 = ss[h]
                    causal = lambda s, k0, q0: jnp.where(
                        k0 + lax.broadcasted_iota(jnp.int32, s.shape, 0)
                        <= q0 + lax.broadcasted_iota(jnp.int32, s.shape, 1), s, NEG_BIG)
                    s0 = causal(s0, start, q_lo)
                    s1 = causal(s1, start + half, q_lo + half)
                    mx = jnp.max(s0, axis=0, keepdims=True)
                    mx = jnp.concatenate(
                        [mx[:, :half],
                         jnp.maximum(mx[:, half:], jnp.max(s1, axis=0, keepdims=True))], axis=1)
                    m_new = jnp.maximum(m_old, mx)
                    a = jnp.exp2(m_old - m_new)
                    p0 = jnp.exp2(s0 - m_new).astype(BF16)
                    p1 = jnp.exp2(s1 - m_new[:, half:]).astype(BF16)
                    accs[h][...] = a * accs[h][...] + jnp.dot(v_rows(h, start, half), p0,
                                                              preferred_element_type=F32)
                    accs[h][:, half:] = accs[h][:, half:] + jnp.dot(
                        v_rows(h, start + half, half), p1, preferred_element_type=F32)
                else:
                    m_new = jnp.maximum(m_old, jnp.max(ss[h], axis=0, keepdims=True))
                    a = jnp.exp2(m_old - m_new)
                    p = jnp.exp2(ss[h] - m_new).astype(BF16)
                    accs[h][...] = a * accs[h][...] + jnp.dot(v_rows(h, start, tk), p,
                                                              preferred_element_type=F32)
                carry[h] = m_new
        return tuple(carry)

    assert tq == tk
    half = tk // 2

    ones_rows = lambda n: jnp.where(lax.broadcasted_iota(jnp.int32, (ATT_SUM_ROWS, n), 0) == 0,
                                    1.0, 0.0).astype(BF16)
    init = tuple(jnp.full((1, tq), NEG_BIG, F32) for _ in range(2))
    n_full = q_lo // tk
    group = ATT_TILES_PER_BLOCK
    carry = lax.fori_loop(
        0, n_full // group,
        lambda j, c: tiles([(group * j + i, False) for i in range(group)], c), init)
    rest = n_full % group

    def tail(n):
        return lambda c: tiles([(n_full - n + i, False) for i in range(n)] + [(n_full, True)], c)

    lax.switch(rest, [tail(n) for n in range(group)], carry)
    d = ATT_HEAD_DIM
    out_t = jnp.concatenate([acc[0:d, :] / acc[d:d + 1, :] for acc in accs], axis=0)
    o_ref[q_rows, :] = out_t.T.astype(BF16)


def _fox_attention(qf, kf, vt, batch, seq):
    t = qf.shape[0]
    tq = min(TQ, seq)
    tk = min(TK, seq)
    assert seq % tq == 0 and seq % tk == 0 and tk % tq == 0
    pairs = ATT_HEADS // 2
    return pl.pallas_call(
        functools.partial(_fox_kernel, tq=tq, tk=tk),
        grid=(batch, pairs),
        in_specs=[pl.BlockSpec((seq, 2 * LANES), lambda b, hp: (b, hp)),
                  pl.BlockSpec((seq, 2 * LANES), lambda b, hp: (b, hp)),
                  pl.BlockSpec((LANES, seq), lambda b, hp: (hp, b))],
        out_specs=pl.BlockSpec((seq, LANES), lambda b, hp: (b, hp)),
        out_shape=jax.ShapeDtypeStruct((t, WIDTH), BF16),
        scratch_shapes=[pltpu.VMEM((ATT_HEAD_DIM + ATT_SUM_ROWS, tq), F32),
                        pltpu.VMEM((ATT_HEAD_DIM + ATT_SUM_ROWS, tq), F32)],
        compiler_params=pltpu.CompilerParams(
            dimension_semantics=("arbitrary", "arbitrary"),
            vmem_limit_bytes=VMEM_LIMIT),
        name="fox_attn",
    )(qf, kf, vt)


def _partner(x, row, bit):
    n = x.shape[0]
    up = pltpu.roll(x, n - bit, axis=0)
    down = pltpu.roll(x, bit, axis=0)
    return jnp.where((row & bit) == 0, up, down)


def _merge_kernel(x_ref, att_ref, conv_ref, sc_ref, wg_ref, bg_ref, wb_ref, wo_ref,
                  g1_ref, b1_ref, wr_ref, br_ref, triu_ref,
                  x1_ref, x1p_ref, route_ref, counts_ref, carry):
    i = pl.program_id(0)

    @pl.when(i == 0)
    def _():
        carry[...] = jnp.zeros_like(carry)

    x = x_ref[...]
    xb = x.astype(BF16)
    merged = None
    for n, br_ref_n in enumerate((att_ref, conv_ref, sc_ref)):
        gate = _sigmoid(jnp.dot(xb, wg_ref[n], preferred_element_type=F32) + bg_ref[n])
        term = gate * jnp.dot(br_ref_n[...], wb_ref[n], preferred_element_type=F32)
        merged = term if merged is None else merged + term
    h = jnp.dot(merged.astype(BF16), wo_ref[...], preferred_element_type=F32)
    x1 = _layer_norm(DEEPNORM_ALPHA * x + h, g1_ref[...], b1_ref[...])
    x1_ref[...] = x1
    _pack_rows(x1, x1p_ref)

    x_hi = x1.astype(BF16)
    x_lo = (x1 - x_hi.astype(F32)).astype(BF16)
    logits = lax.dot_general(wr_ref[...], jnp.concatenate([x_hi, x_lo, x_hi], axis=1),
                             (((1,), (1,)), ((), ())), preferred_element_type=F32) + br_ref[...]
    mx = jnp.max(logits, axis=0, keepdims=True)
    ex = jnp.exp(logits - mx)
    p = ex / jnp.sum(ex, axis=0, keepdims=True)

    row = lax.broadcasted_iota(jnp.int32, p.shape, 0)
    sub = row & (EXPERTS_PER_GROUP - 1)
    rank = jnp.zeros(p.shape, jnp.int32)
    for d in range(1, EXPERTS_PER_GROUP):
        below = pltpu.roll(p, d, axis=0)
        above = pltpu.roll(p, N_EXPERTS - d, axis=0)
        rank = rank + jnp.where((sub >= d) & (below >= p), 1, 0)
        rank = rank + jnp.where((sub + d < EXPERTS_PER_GROUP) & (above > p), 1, 0)
    top2 = rank < 2
    score = jnp.where(top2, p, 0.0)
    score = score + _partner(score, row, 1)
    score = score + _partner(score, row, 2)
    best = jnp.max(score, axis=0, keepdims=True)
    group = row >> 2
    best_group = jnp.min(jnp.where(score == best, group, N_EXPERTS), axis=0, keepdims=True)
    sel = top2 & (group == best_group)
    gate = jnp.where(sel, p / best, 0.0)

    sel_b = jnp.where(sel, 1.0, 0.0).astype(BF16)
    before = jnp.dot(sel_b, triu_ref[...], preferred_element_type=F32) + carry[...]
    tm = x.shape[0]
    new_carry = before[:, tm - 1:tm] + sel_b[:, tm - 1:tm].astype(F32)
    carry[...] = new_carry
    counts_ref[...] = new_carry

    e_lo = jnp.min(jnp.where(sel, row, N_EXPERTS), axis=0, keepdims=True)
    e_hi = jnp.max(jnp.where(sel, row, -1), axis=0, keepdims=True)
    is_lo = row == e_lo
    is_hi = row == e_hi
    pick = lambda m, a: jnp.sum(jnp.where(m, a, 0.0), axis=0, keepdims=True)
    zero = jnp.zeros_like(mx)
    route_ref[...] = jnp.concatenate(
        [e_lo.astype(F32), e_hi.astype(F32), pick(is_lo, before), pick(is_hi, before),
         pick(is_lo, gate), pick(is_hi, gate), zero, zero], axis=0)


def _merge(x2, att, conv, sc, wg, bg, wb, wo, g1, b1, wr, br):
    t, d = x2.shape
    tm = min(TM_PROJ, t)
    assert t % tm == 0
    triu = jnp.triu(jnp.ones((tm, tm), F32), 1).astype(BF16)
    const = lambda shape: pl.BlockSpec(shape, lambda i: (0,) * len(shape),
                                       pipeline_mode=pl.Buffered(1))
    rows = lambda w: pl.BlockSpec((tm, w), lambda i: (i, 0))
    return pl.pallas_call(
        _merge_kernel,
        grid=(t // tm,),
        in_specs=[rows(d), rows(WIDTH), rows(WIDTH), rows(WIDTH),
                  const(wg.shape), const(bg.shape), const(wb.shape), const(wo.shape),
                  const(g1.shape), const(b1.shape), const(wr.shape), const(br.shape),
                  const(triu.shape)],
        out_specs=[rows(d), pl.BlockSpec((ROW_CHUNKS, tm, CHUNK), lambda i: (0, i, 0)),
                   pl.BlockSpec((ROUTE_ROWS, tm), lambda i: (0, i)),
                   pl.BlockSpec((N_EXPERTS, 1), lambda i: (0, 0))],
        out_shape=[jax.ShapeDtypeStruct((t, d), F32),
                   jax.ShapeDtypeStruct((ROW_CHUNKS, t, CHUNK), jnp.uint32),
                   jax.ShapeDtypeStruct((ROUTE_ROWS, t), F32),
                   jax.ShapeDtypeStruct((N_EXPERTS, 1), F32)],
        scratch_shapes=[pltpu.VMEM((N_EXPERTS, 1), F32)],
        compiler_params=pltpu.CompilerParams(dimension_semantics=("arbitrary",),
                                             vmem_limit_bytes=VMEM_LIMIT),
        name="merge",
    )(x2, att, conv, sc, wg, bg, wb, wo, g1, b1, wr, br, triu)


def _sc_mesh():
    return plsc.VectorSubcoreMesh(core_axis_name="core", subcore_axis_name="subcore")


def _sc_scatter_rows(x, idx, n_rows):
    t, d = x.shape
    m = idx.shape[0]
    w = SC_WINDOW
    assert t % w == 0 and m == 2 * t
    steps = t // w

    @functools.partial(pl.kernel, out_type=jax.ShapeDtypeStruct((n_rows, d), x.dtype),
                       mesh=_sc_mesh(), scratch_types=[], name="sc_dispatch")
    def run(x_hbm, i_hbm, o_hbm):
        def body(x_vmem, first_vmem, second_vmem):
            pltpu.sync_copy(x_vmem, o_hbm.at[first_vmem.at[0]])
            pltpu.sync_copy(x_vmem, o_hbm.at[second_vmem.at[0]])

        pltpu.emit_pipeline(
            body,
            grid=(steps,),
            in_specs=[pl.BlockSpec((w, d), index_map=lambda i: (i, 0)),
                      pl.BlockSpec((1, w), index_map=lambda i: (0, i)),
                      pl.BlockSpec((1, w), index_map=lambda i: (0, steps + i))],
            out_specs=[],
            core_axis_name=("core", "subcore"),
            dimension_semantics=(pltpu.PARALLEL,),
            trace_scopes=False,
        )(x_hbm, i_hbm, i_hbm)

    return run(x, idx.reshape(1, m))


def _sc_gather_rows(table, idx):
    d = table.shape[1]
    m = idx.shape[0]
    w = SC_WINDOW
    assert m % w == 0

    @functools.partial(pl.kernel, out_type=jax.ShapeDtypeStruct((m, d), table.dtype),
                       mesh=_sc_mesh(), scratch_types=[], name="sc_gather")
    def run(t_hbm, i_hbm, o_hbm):
        def body(i_vmem, o_vmem):
            pltpu.sync_copy(t_hbm.at[i_vmem.at[0]], o_vmem)

        pltpu.emit_pipeline(
            body,
            grid=(m // w,),
            in_specs=[pl.BlockSpec((1, w), index_map=lambda i: (0, i))],
            out_specs=[pl.BlockSpec((w, d), index_map=lambda i: (i, 0))],
            core_axis_name=("core", "subcore"),
            dimension_semantics=(pltpu.PARALLEL,),
            trace_scopes=False,
        )(i_hbm, o_hbm)

    return run(table, idx.reshape(1, m))


def _fill_kernel(fs_ref, fl_ref, xs_in, xs_ref, zeros, sem):
    del xs_in
    zb = zeros.shape[0]
    zeros[...] = jnp.zeros_like(zeros)

    n_regions = fs_ref.shape[0]
    rows_per_chunk = xs_ref.shape[0] // ROW_CHUNKS

    def region(e, fn):
        start = fs_ref[e % n_regions] + (e // n_regions) * rows_per_chunk
        length = fl_ref[e % n_regions]
        head = jnp.minimum(length, (SUBLANES - start % SUBLANES) % SUBLANES)
        mid = start + head
        mid_len = (length - head) // SUBLANES * SUBLANES
        n_full = mid_len // zb
        rem = mid_len - n_full * zb
        part = mid + n_full * zb
        tail = mid + mid_len
        n_tail = length - head - mid_len

        def single(base):
            def body(r, _):
                fn(pltpu.make_async_copy(zeros.at[pl.ds(0, 1)],
                                         xs_ref.at[pl.ds(base + r, 1)], sem))
                return 0
            return body
        lax.fori_loop(0, head, single(start), 0)
        lax.fori_loop(0, n_tail, single(tail), 0)

        def full(c, _):
            off = pl.multiple_of(mid + c * zb, SUBLANES)
            fn(pltpu.make_async_copy(zeros, xs_ref.at[pl.ds(off, zb)], sem))
            return 0
        lax.fori_loop(0, n_full, full, 0)
        bit = zb // 2
        while bit >= SUBLANES:
            off = pl.multiple_of(part + (rem - rem % (2 * bit)), SUBLANES)

            @pl.when((rem & bit) != 0)
            def _(bit=bit, off=off):
                fn(pltpu.make_async_copy(zeros.at[pl.ds(0, bit)],
                                         xs_ref.at[pl.ds(off, bit)], sem))
            bit //= 2

    for fn in (lambda cp: cp.start(), lambda cp: cp.wait()):
        def body(e, _, fn=fn):
            region(e, fn)
            return 0
        lax.fori_loop(0, n_regions * ROW_CHUNKS, body, 0)


def _fill_unrouted(xs, fill_start, fill_len):
    n_rows, d = xs.shape
    return pl.pallas_call(
        _fill_kernel,
        grid_spec=pltpu.PrefetchScalarGridSpec(
            num_scalar_prefetch=2,
            grid=(1,),
            in_specs=[pl.BlockSpec(memory_space=pl.ANY)],
            out_specs=pl.BlockSpec(memory_space=pl.ANY),
            scratch_shapes=[pltpu.VMEM((ZERO_ROWS, d), xs.dtype), pltpu.SemaphoreType.DMA]),
        out_shape=jax.ShapeDtypeStruct((n_rows, d), xs.dtype),
        input_output_aliases={2: 0},
        compiler_params=pltpu.CompilerParams(dimension_semantics=("arbitrary",),
                                             vmem_limit_bytes=VMEM_LIMIT),
        name="fill_unrouted",
    )(fill_start, fill_len, xs)


def _expert_kernel(be_ref, nb_ref, xs_ref, wu_ref, wd_ref, ys_ref, wu_bf, wd_bf):
    b = pl.program_id(0)
    live = b < nb_ref[0]

    @pl.when(live & ((b == 0) | (be_ref[b] != be_ref[jnp.maximum(b, 1) - 1])))
    def _():
        wu_bf[...] = wu_ref[0, 0].astype(BF16)
        wd_bf[...] = wd_ref[0, 0].astype(BF16)

    @pl.when(live)
    def _():
        xb = _unpack_rows(xs_ref).astype(BF16)
        y = None
        for c0 in range(0, EXPERT_FF, EXPERT_FF // 2):
            c1 = c0 + EXPERT_FF // 2
            a = jnp.dot(xb, wu_bf[:, c0:c1], preferred_element_type=F32)
            g = jnp.dot(xb, wu_bf[:, EXPERT_FF + c0:EXPERT_FF + c1], preferred_element_type=F32)
            act = (g * _sigmoid(g) * a).astype(BF16)
            part = jnp.dot(act, wd_bf[c0:c1, :], preferred_element_type=F32)
            y = part if y is None else y + part
        _pack_rows(y, ys_ref)

    @pl.when(jnp.logical_not(live))
    def _():
        ys_ref[...] = jnp.zeros_like(ys_ref)


def _experts(xs, w_up, w_down, layer, block_expert, n_used):
    _, n_rows, d = xs.shape
    n_blocks = n_rows // EXPERT_BLOCK

    def live(b, nb):
        return jnp.minimum(b, nb[0] - 1)

    wspec = lambda w: pl.BlockSpec((1, 1) + w.shape[2:],
                                   lambda b, be, nb: (layer, be[live(b, nb)], 0, 0))
    return pl.pallas_call(
        _expert_kernel,
        grid_spec=pltpu.PrefetchScalarGridSpec(
            num_scalar_prefetch=2,
            grid=(n_blocks,),
            in_specs=[pl.BlockSpec((ROW_CHUNKS, EXPERT_BLOCK, d), lambda b, be, nb: (0, live(b, nb), 0)),
                      wspec(w_up), wspec(w_down)],
            out_specs=pl.BlockSpec((ROW_CHUNKS, EXPERT_BLOCK, d), lambda b, be, nb: (0, b, 0)),
            scratch_shapes=[pltpu.VMEM(w_up.shape[2:], BF16), pltpu.VMEM(w_down.shape[2:], BF16)]),
        out_shape=jax.ShapeDtypeStruct(xs.shape, xs.dtype),
        compiler_params=pltpu.CompilerParams(dimension_semantics=("arbitrary",),
                                             vmem_limit_bytes=VMEM_LIMIT),
        name="experts",
    )(block_expert, n_used, xs, w_up, w_down)


def _combine_kernel(x1_ref, route_ref, y0_ref, y1_ref, g2_ref, b2_ref, o_ref):
    route = route_ref[...].T
    m = route[:, 4:5] * _unpack_rows(y0_ref.at[0]) + route[:, 5:6] * _unpack_rows(y1_ref.at[0])
    o_ref[...] = _layer_norm(DEEPNORM_ALPHA * x1_ref[...] + m, g2_ref[...], b2_ref[...])


def _combine(x1, route, yk, g2, b2):
    t, d = x1.shape
    dc = yk.shape[-1]
    tm = min(TM_ROUTE, t)
    steps = t // tm
    const = lambda shape: pl.BlockSpec(shape, lambda i: (0,) * len(shape))
    return pl.pallas_call(
        _combine_kernel,
        grid=(steps,),
        in_specs=[pl.BlockSpec((tm, d), lambda i: (i, 0)),
                  pl.BlockSpec((ROUTE_ROWS, tm), lambda i: (0, i)),
                  pl.BlockSpec((1, ROW_CHUNKS, tm, dc), lambda i: (0, 0, i, 0)),
                  pl.BlockSpec((1, ROW_CHUNKS, tm, dc), lambda i: (1, 0, i, 0)),
                  const(g2.shape), const(b2.shape)],
        out_specs=pl.BlockSpec((tm, d), lambda i: (i, 0)),
        out_shape=jax.ShapeDtypeStruct((t, d), F32),
        compiler_params=pltpu.CompilerParams(dimension_semantics=("arbitrary",),
                                             vmem_limit_bytes=VMEM_LIMIT),
        name="combine",
    )(x1, route, yk, yk, g2, b2)


def _routing_tables(route, counts, n_blocks):
    t = route.shape[1]
    cnt = counts[:, 0].astype(jnp.int32)
    padded = (cnt + EXPERT_BLOCK - 1) // EXPERT_BLOCK * EXPERT_BLOCK
    pad_end = jnp.cumsum(padded)
    pad_start = pad_end - padded
    experts = route[0:2].astype(jnp.int32)
    ranks = route[2:4].astype(jnp.int32)
    ids = jnp.arange(N_EXPERTS, dtype=jnp.int32)
    base = jnp.sum(jnp.where(experts[:, :, None] == ids, pad_start, 0), axis=-1)
    n_rows = n_blocks * EXPERT_BLOCK
    chunk_base = jnp.arange(ROW_CHUNKS, dtype=jnp.int32) * n_rows
    pos = ((base + ranks)[:, None, :] + chunk_base[None, :, None]).reshape(2 * ROW_CHUNKS * t)
    blk = jnp.arange(n_blocks, dtype=jnp.int32) * EXPERT_BLOCK
    block_expert = jnp.minimum(jnp.sum((pad_end[None, :] <= blk[:, None]).astype(jnp.int32), axis=1),
                               N_EXPERTS - 1)
    n_used = (pad_end[-1:] // EXPERT_BLOCK).astype(jnp.int32)
    fill_start = jnp.concatenate([pad_start + cnt, pad_end[-1:]]).astype(jnp.int32)
    fill_len = jnp.concatenate([padded - cnt, n_rows - pad_end[-1:]]).astype(jnp.int32)
    return pos, block_expert, n_used, fill_start, fill_len


def kernel(x, w_in, b_forget, conv_w, conv_b, conv_ln_g, conv_ln_b, sc_w, w_branch, w_gate, b_gate,
           w_out, ln1_g, ln1_b, w_up, w_down, ln2_g, ln2_b, w_router, b_router):
    batch, seq, d = x.shape
    t = batch * seq
    depth = w_in.shape[0]
    x2 = x.reshape(t, d)

    nf = ATT_HEADS
    w_main = jnp.concatenate([w_in[:, :, :2 * WIDTH], w_in[:, :, 3 * WIDTH + nf:]], axis=2).astype(BF16)
    w_vt = jnp.swapaxes(w_in[:, :, 2 * WIDTH:3 * WIDTH], 1, 2).astype(BF16)
    w_f = jnp.pad(w_in[:, :, 3 * WIDTH:3 * WIDTH + nf], ((0, 0), (0, 0), (0, LANES - nf))).astype(BF16)
    b_f = jnp.pad(b_forget, ((0, 0), (0, LANES - nf)))[:, None, :]
    wg = w_gate.astype(BF16)
    wb = w_branch.astype(BF16)
    wo = w_out.astype(BF16)
    wr_t = w_router.T
    wr_hi = wr_t.astype(BF16)
    wr_lo = (wr_t - wr_hi.astype(F32)).astype(BF16)
    wr = jnp.concatenate([wr_hi, wr_hi, wr_lo], axis=1)
    br = b_router[:, None]

    def layer(xh, l):
        th = xh.shape[0]
        n_blocks = (2 * th + N_EXPERTS * (EXPERT_BLOCK - 1) + EXPERT_BLOCK - 1) // EXPERT_BLOCK
        n_rows = n_blocks * EXPERT_BLOCK
        qf, kf, vt, br_conv, br_sc = _in_proj(
            xh, w_main[l], w_vt[l], w_f[l], b_f[l], sc_w[l], conv_w[l], conv_b[l][None, :],
            conv_ln_g[l][None, :], conv_ln_b[l][None, :], seq)
        br_att = _fox_attention(qf, kf, vt, th // seq, seq)
        x1, x1p, route, counts = _merge(xh, br_att, br_conv, br_sc, wg[l], b_gate[l][:, None, :],
                                        wb[l], wo[l], ln1_g[l][None, :], ln1_b[l][None, :], wr, br)
        pos, block_expert, n_used, fill_start, fill_len = _routing_tables(route, counts, n_blocks)
        xs = _sc_scatter_rows(x1p.reshape(ROW_CHUNKS * th, CHUNK), pos, ROW_CHUNKS * n_rows)
        xs = _fill_unrouted(xs, fill_start, fill_len).reshape(ROW_CHUNKS, n_rows, CHUNK)
        ys = _experts(xs, w_up, w_down, l, block_expert, n_used)
        yk = _sc_gather_rows(ys.reshape(ROW_CHUNKS * n_rows, CHUNK), pos)
        return _combine(x1, route, yk.reshape(2, ROW_CHUNKS, th, CHUNK),
                        ln2_g[l][None, :], ln2_b[l][None, :])

    for l in range(depth):
        x2 = layer(x2, l)
    return x2.reshape(batch, seq, d)
```

```python
import functools

import numpy as np
import jax
import jax.numpy as jnp
from jax import lax
from jax.experimental import pallas as pl
from jax.experimental.pallas import tpu as pltpu
from jax.experimental.pallas import tpu_sc as plsc

F32 = jnp.float32
BF16 = jnp.bfloat16

ATT_HEADS = 8
ATT_HEAD_DIM = 64
WIDTH = 512
CONV_TAPS = 31
SC_TAPS = 3
N_EXPERTS = 32
EXPERTS_PER_GROUP = 4
EXPERT_FF = 512
DEPTH = 4
DEEPNORM_ALPHA = (2 * DEPTH) ** 0.25
LN_EPS = 1e-5

LANES = 128
SUBLANES = 8
VMEM_LIMIT = 56 * 1024 * 1024

TM_PROJ = 512
CONV_HALO = 32
CONV_CHUNK = 64
TQ = 512
TK = 512
ATT_TILES_PER_BLOCK = 4
ATT_SUM_ROWS = 16
EXPERT_BLOCK = 512
TM_ROUTE = 1024
SC_WINDOW = 128
ROW_CHUNKS = 2
CHUNK = 256
ROUTE_ROWS = 8
ZERO_ROWS = 256
NEG_BIG = -1e30
LOG2E = 1.4426950408889634


def _sigmoid(x):
    return 1.0 / (1.0 + jnp.exp(-x))


def _log_sigmoid(x):
    return jnp.minimum(x, 0.0) - jnp.log(1.0 + jnp.exp(-jnp.abs(x)))


def _split3(x):
    hi = x.astype(BF16)
    r1 = x - hi.astype(F32)
    mid = r1.astype(BF16)
    lo = (r1 - mid.astype(F32)).astype(BF16)
    return hi, mid, lo


def _pack_rows(x, ref):
    half = x.shape[1] // 2
    bits = pltpu.bitcast(x, jnp.uint32)
    bits = bits + (jnp.uint32(0x7FFF) + ((bits >> 16) & jnp.uint32(1)))
    words = (bits[:, :half] & jnp.uint32(0xFFFF0000)) | (bits[:, half:] >> 16)
    for c in range(ROW_CHUNKS):
        ref[c] = words[:, c * CHUNK:(c + 1) * CHUNK]


def _unpack_rows(ref):
    words = jnp.concatenate([ref[c] for c in range(ROW_CHUNKS)], axis=1)
    hi = pltpu.bitcast(words & jnp.uint32(0xFFFF0000), F32)
    lo = pltpu.bitcast(words << 16, F32)
    return jnp.concatenate([hi, lo], axis=1)


def _pack_pieces(x):
    lane = lax.broadcasted_iota(jnp.int32, x.shape, 1)
    hi, mid, lo = (p.astype(F32) for p in _split3(x))
    packed = jnp.where(lane < ATT_HEADS, hi,
                       jnp.where(lane < 2 * ATT_HEADS, pltpu.roll(mid, ATT_HEADS, axis=1),
                                 jnp.where(lane < 3 * ATT_HEADS,
                                           pltpu.roll(lo, 2 * ATT_HEADS, axis=1), 0.0)))
    return packed.astype(BF16)


def _layer_norm(y, g, b):
    mu = jnp.mean(y, axis=-1, keepdims=True)
    yc = y - mu
    var = jnp.mean(yc * yc, axis=-1, keepdims=True)
    return yc * lax.rsqrt(var + LN_EPS) * g + b


def _bias_placement():
    pairs = ATT_HEADS // 2
    place = np.zeros((LANES, 2 * pairs * LANES), np.float32)
    ones = np.zeros((1, 2 * pairs * LANES), np.float32)
    for p in range(pairs):
        for h in range(2):
            for piece in range(3):
                src = piece * ATT_HEADS + 2 * p + h
                place[src, p * LANES + 6 * h + 3 + piece] = 1.0
                place[src, (pairs + p) * LANES + 6 * h + piece] = -1.0
                ones[0, p * LANES + 6 * h + piece] = 1.0
                ones[0, (pairs + p) * LANES + 6 * h + 3 + piece] = 1.0
    return jnp.asarray(place, BF16), jnp.asarray(ones, F32)


def _in_proj_kernel(x_ref, wm_ref, wvt_ref, wf_ref, bf_ref, scw_ref, tri_ref, place_ref, ones_ref,
                    cw_ref, cb_ref, cg_ref, cbeta_ref,
                    qf_ref, kf_ref, vt_ref, brconv_ref, brsc_ref,
                    carry_c, carry_p, ext, pre, *, tiles_per_seq):
    i = pl.program_id(0)
    tm = x_ref.shape[0]

    @pl.when(i % tiles_per_seq == 0)
    def _():
        carry_c[...] = jnp.zeros_like(carry_c)
        carry_p[...] = jnp.zeros_like(carry_p)
        _conv31_reset(ext, tm)

    xb = x_ref[...].astype(BF16)

    def mm(j):
        return jnp.dot(xb, wm_ref[:, j * WIDTH:(j + 1) * WIDTH], preferred_element_type=F32)

    ext[CONV_HALO:CONV_HALO + tm, :] = mm(2) * _sigmoid(mm(3))
    for c0 in range(0, tm, CONV_CHUNK):
        _conv31_rows(c0, cw_ref, ext, pre)
    _conv31_finish(tm, cb_ref, cg_ref, cbeta_ref, brconv_ref, ext, pre)

    q = (mm(0) * (ATT_HEAD_DIM ** -0.5 * LOG2E)).astype(BF16)
    k = mm(1).astype(BF16)
    vt_ref[...] = lax.dot_general(wvt_ref[...], xb, (((1,), (1,)), ((), ())),
                                  preferred_element_type=F32).astype(BF16)

    p = mm(5) * mm(6)
    row = lax.broadcasted_iota(jnp.int32, p.shape, 0)
    prev = carry_p[...]
    p1 = jnp.where(row == 0, prev[7:8, :], pltpu.roll(p, 1, axis=0))
    p2 = pltpu.roll(p, 2, axis=0)
    p2 = jnp.where(row == 0, prev[6:7, :], jnp.where(row == 1, prev[7:8, :], p2))
    y = scw_ref[0:1, :] * p2 + scw_ref[1:2, :] * p1 + scw_ref[2:3, :] * p
    brsc_ref[...] = (mm(4) * y).astype(BF16)
    carry_p[...] = p[tm - SUBLANES:tm, :]

    zf = jnp.dot(xb, wf_ref[...], preferred_element_type=F32)
    ls = _log_sigmoid(zf + bf_ref[...])
    sums = jnp.dot(tri_ref[...], _pack_pieces(ls), preferred_element_type=F32)
    c = (sums + pltpu.roll(sums, LANES - ATT_HEADS, axis=1)
         + pltpu.roll(sums, LANES - 2 * ATT_HEADS, axis=1)) + carry_c[...]
    carry_c[...] = c[tm - 1:tm, :]

    bias = (jnp.dot(_pack_pieces(c * LOG2E), place_ref[...], preferred_element_type=F32)
            + ones_ref[...]).astype(BF16)
    pairs = ATT_HEADS // 2
    for p in range(pairs):
        lo_, hi_ = p * LANES, (p + 1) * LANES
        qf_ref[:, 2 * lo_:2 * lo_ + LANES] = q[:, lo_:hi_]
        qf_ref[:, 2 * lo_ + LANES:2 * hi_] = bias[:, lo_:hi_]
        kf_ref[:, 2 * lo_:2 * lo_ + LANES] = k[:, lo_:hi_]
        kf_ref[:, 2 * lo_ + LANES:2 * hi_] = bias[:, (pairs + p) * LANES:(pairs + p + 1) * LANES]


def _in_proj(x2, wm, wvt, wf, bf, scw, cw, cb, cg, cbeta, seq):
    t, d = x2.shape
    tm = min(TM_PROJ, seq)
    assert seq % tm == 0 and t % seq == 0 and tm % CONV_CHUNK == 0
    tri = jnp.tril(jnp.ones((tm, tm), F32)).astype(BF16)
    place, ones = _bias_placement()
    const = lambda shape: pl.BlockSpec(shape, lambda i: (0,) * len(shape))
    rows = lambda w: pl.BlockSpec((tm, w), lambda i: (i, 0))
    return pl.pallas_call(
        functools.partial(_in_proj_kernel, tiles_per_seq=seq // tm),
        grid=(t // tm,),
        in_specs=[rows(d), const(wm.shape), const(wvt.shape), const(wf.shape), const(bf.shape),
                  const(scw.shape), const(tri.shape), const(place.shape), const(ones.shape),
                  const(cw.shape), const(cb.shape), const(cg.shape), const(cbeta.shape)],
        out_specs=[rows(2 * WIDTH), rows(2 * WIDTH), pl.BlockSpec((WIDTH, tm), lambda i: (0, i)),
                   rows(WIDTH), rows(WIDTH)],
        out_shape=[jax.ShapeDtypeStruct((t, 2 * WIDTH), BF16),
                   jax.ShapeDtypeStruct((t, 2 * WIDTH), BF16),
                   jax.ShapeDtypeStruct((WIDTH, t), BF16),
                   jax.ShapeDtypeStruct((t, WIDTH), BF16),
                   jax.ShapeDtypeStruct((t, WIDTH), BF16)],
        scratch_shapes=[pltpu.VMEM((1, LANES), F32), pltpu.VMEM((SUBLANES, WIDTH), F32),
                        pltpu.VMEM((CONV_HALO + tm + SUBLANES, WIDTH), F32),
                        pltpu.VMEM((tm, WIDTH), F32)],
        compiler_params=pltpu.CompilerParams(dimension_semantics=("arbitrary",),
                                             vmem_limit_bytes=VMEM_LIMIT),
        name="in_proj",
    )(x2, wm, wvt, wf, bf, scw, tri, place, ones, cw, cb, cg, cbeta)


def _conv31_reset(ext, tm):
    ext[0:CONV_HALO, :] = jnp.zeros((CONV_HALO, WIDTH), F32)
    ext[CONV_HALO + tm:CONV_HALO + tm + SUBLANES, :] = jnp.zeros((SUBLANES, WIDTH), F32)


def _conv31_rows(c0, w_ref, ext, pre):
    base = CONV_HALO - (CONV_TAPS - 1)
    span = CONV_CHUNK + SUBLANES
    half = WIDTH // 2
    for l0 in (0, half):
        y = None
        for r in range(SUBLANES):
            z = None
            for a in range((base + CONV_TAPS - 1) // SUBLANES + 1):
                k = SUBLANES * a + r - base
                if 0 <= k < CONV_TAPS:
                    term = (w_ref[k:k + 1, l0:l0 + half]
                            * ext[c0 + SUBLANES * a:c0 + SUBLANES * a + span, l0:l0 + half])
                    z = term if z is None else z + term
            zr = z[r:r + CONV_CHUNK, :]
            y = zr if y is None else y + zr
        pre[c0:c0 + CONV_CHUNK, l0:l0 + half] = y


def _conv31_finish(tm, cb_ref, g_ref, b_ref, o_ref, ext, pre):
    for c0 in range(0, tm, CONV_CHUNK):
        y = _layer_norm(pre[c0:c0 + CONV_CHUNK, :] + cb_ref[...], g_ref[...], b_ref[...])
        o_ref[c0:c0 + CONV_CHUNK, :] = (y * _sigmoid(y)).astype(BF16)
    ext[0:CONV_HALO, :] = ext[tm:tm + CONV_HALO, :]


def _fox_kernel(qf_ref, kf_ref, vt_ref, o_ref, acc0_ref, acc1_ref, *, tq, tk):
    def q_tile(qi, _):
        _fox_q_tile(qi, qf_ref, kf_ref, vt_ref, o_ref, acc0_ref, acc1_ref, tq=tq, tk=tk)
        return 0

    lax.fori_loop(0, qf_ref.shape[0] // tq, q_tile, 0)


def _fox_q_tile(qi, qf_ref, kf_ref, vt_ref, o_ref, acc0_ref, acc1_ref, *, tq, tk):
    q_rows = pl.ds(pl.multiple_of(qi * tq, tq), tq)
    lane = lax.broadcasted_iota(jnp.int32, (tq, 2 * LANES), 1)
    bias_lane = lane - LANES
    qfull = qf_ref[q_rows, :]
    zero = jnp.zeros_like(qfull)
    qs = [jnp.where(((lane >= h * ATT_HEAD_DIM) & (lane < (h + 1) * ATT_HEAD_DIM))
                    | ((bias_lane >= 6 * h) & (bias_lane < 6 * h + 6)), qfull, zero)
          for h in range(2)]
    q_lo = qi * tq
    accs = (acc0_ref, acc1_ref)
    for acc in accs:
        acc[...] = jnp.zeros(acc.shape, F32)

    def tiles(jobs, carry):
        nt = lambda k, q: lax.dot_general(k, q, (((1,), (1,)), ((), ())),
                                          preferred_element_type=F32)
        work = []
        for j, masked in jobs:
            start = pl.multiple_of(j * tk, tk)
            if masked:
                ss = [(nt(kf_ref[pl.ds(start, half), :], qs[h]),
                       nt(kf_ref[pl.ds(start + half, half), :], qs[h][half:, :]))
                      for h in range(2)]
            else:
                kfull = kf_ref[pl.ds(start, tk), :]
                ss = [nt(kfull, qs[h]) for h in range(2)]
            work.append((start, masked, ss))

        def v_rows(h, first, n):
            return jnp.concatenate(
                [vt_ref[h * ATT_HEAD_DIM:(h + 1) * ATT_HEAD_DIM, pl.ds(first, n)],
                 ones_rows(n)], axis=0)

        carry = list(carry)
        for start, masked, ss in work:
            for h in range(2):
                m_old = carry[h]
                if masked:
                    s0, s1 = ss[h]
                    causal = lambda s, k0, q0: jnp.where(
                        k0 + lax.broadcasted_iota(jnp.int32, s.shape, 0)
                        <= q0 + lax.broadcasted_iota(jnp.int32, s.shape, 1), s, NEG_BIG)
                    s0 = causal(s0, start, q_lo)
                    s1 = causal(s1, start + half, q_lo + half)
                    mx = jnp.max(s0, axis=0, keepdims=True)
                    mx = jnp.concatenate(
                        [mx[:, :half],
                         jnp.maximum(mx[:, half:], jnp.max(s1, axis=0, keepdims=True))], axis=1)
                    m_new = jnp.maximum(m_old, mx)
                    a = jnp.exp2(m_old - m_new)
                    p0 = jnp.exp2(s0 - m_new).astype(BF16)
                    p1 = jnp.exp2(s1 - m_new[:, half:]).astype(BF16)
                    accs[h][...] = a * accs[h][...] + jnp.dot(v_rows(h, start, half), p0,
                                                              preferred_element_type=F32)
                    accs[h][:, half:] = accs[h][:, half:] + jnp.dot(
                        v_rows(h, start + half, half), p1, preferred_element_type=F32)
                else:
                    m_new = jnp.maximum(m_old, jnp.max(ss[h], axis=0, keepdims=True))
                    a = jnp.exp2(m_old - m_new)
                    p = jnp.exp2(ss[h] - m_new).astype(BF16)
                    accs[h][...] = a * accs[h][...] + jnp.dot(v_rows(h, start, tk), p,
                                                              preferred_element_type=F32)
                carry[h] = m_new
        return tuple(carry)

    assert tq == tk
    half = tk // 2

    ones_rows = lambda n: jnp.where(lax.broadcasted_iota(jnp.int32, (ATT_SUM_ROWS, n), 0) == 0,
                                    1.0, 0.0).astype(BF16)
    init = tuple(jnp.full((1, tq), NEG_BIG, F32) for _ in range(2))
    n_full = q_lo // tk
    group = ATT_TILES_PER_BLOCK
    carry = lax.fori_loop(
        0, n_full // group,
        lambda j, c: tiles([(group * j + i, False) for i in range(group)], c), init)
    rest = n_full % group

    def tail(n):
        return lambda c: tiles([(n_full - n + i, False) for i in range(n)] + [(n_full, True)], c)

    lax.switch(rest, [tail(n) for n in range(group)], carry)
    d = ATT_HEAD_DIM
    out_t = jnp.concatenate([acc[0:d, :] / acc[d:d + 1, :] for acc in accs], axis=0)
    o_ref[q_rows, :] = out_t.T.astype(BF16)


def _fox_attention(qf, kf, vt, batch, seq):
    t = qf.shape[0]
    tq = min(TQ, seq)
    tk = min(TK, seq)
    assert seq % tq == 0 and seq % tk == 0 and tk % tq == 0
    pairs = ATT_HEADS // 2
    return pl.pallas_call(
        functools.partial(_fox_kernel, tq=tq, tk=tk),
        grid=(batch, pairs),
        in_specs=[pl.BlockSpec((seq, 2 * LANES), lambda b, hp: (b, hp)),
                  pl.BlockSpec((seq, 2 * LANES), lambda b, hp: (b, hp)),
                  pl.BlockSpec((LANES, seq), lambda b, hp: (hp, b))],
        out_specs=pl.BlockSpec((seq, LANES), lambda b, hp: (b, hp)),
        out_shape=jax.ShapeDtypeStruct((t, WIDTH), BF16),
        scratch_shapes=[pltpu.VMEM((ATT_HEAD_DIM + ATT_SUM_ROWS, tq), F32),
                        pltpu.VMEM((ATT_HEAD_DIM + ATT_SUM_ROWS, tq), F32)],
        compiler_params=pltpu.CompilerParams(
            dimension_semantics=("arbitrary", "arbitrary"),
            vmem_limit_bytes=VMEM_LIMIT),
        name="fox_attn",
    )(qf, kf, vt)


def _partner(x, row, bit):
    n = x.shape[0]
    up = pltpu.roll(x, n - bit, axis=0)
    down = pltpu.roll(x, bit, axis=0)
    return jnp.where((row & bit) == 0, up, down)


def _merge_kernel(x_ref, att_ref, conv_ref, sc_ref, wg_ref, bg_ref, wb_ref, wo_ref,
                  g1_ref, b1_ref, wr_ref, br_ref, triu_ref,
                  x1_ref, x1p_ref, route_ref, counts_ref, carry):
    i = pl.program_id(0)

    @pl.when(i == 0)
    def _():
        carry[...] = jnp.zeros_like(carry)

    x = x_ref[...]
    xb = x.astype(BF16)
    merged = None
    for n, br_ref_n in enumerate((att_ref, conv_ref, sc_ref)):
        gate = _sigmoid(jnp.dot(xb, wg_ref[n], preferred_element_type=F32) + bg_ref[n])
        term = gate * jnp.dot(br_ref_n[...], wb_ref[n], preferred_element_type=F32)
        merged = term if merged is None else merged + term
    h = jnp.dot(merged.astype(BF16), wo_ref[...], preferred_element_type=F32)
    x1 = _layer_norm(DEEPNORM_ALPHA * x + h, g1_ref[...], b1_ref[...])
    x1_ref[...] = x1
    _pack_rows(x1, x1p_ref)

    x_hi = x1.astype(BF16)
    x_lo = (x1 - x_hi.astype(F32)).astype(BF16)
    logits = lax.dot_general(wr_ref[...], jnp.concatenate([x_hi, x_lo, x_hi], axis=1),
                             (((1,), (1,)), ((), ())), preferred_element_type=F32) + br_ref[...]
    mx = jnp.max(logits, axis=0, keepdims=True)
    ex = jnp.exp(logits - mx)
    p = ex / jnp.sum(ex, axis=0, keepdims=True)

    row = lax.broadcasted_iota(jnp.int32, p.shape, 0)
    sub = row & (EXPERTS_PER_GROUP - 1)
    rank = jnp.zeros(p.shape, jnp.int32)
    for d in range(1, EXPERTS_PER_GROUP):
        below = pltpu.roll(p, d, axis=0)
        above = pltpu.roll(p, N_EXPERTS - d, axis=0)
        rank = rank + jnp.where((sub >= d) & (below >= p), 1, 0)
        rank = rank + jnp.where((sub + d < EXPERTS_PER_GROUP) & (above > p), 1, 0)
    top2 = rank < 2
    score = jnp.where(top2, p, 0.0)
    score = score + _partner(score, row, 1)
    score = score + _partner(score, row, 2)
    best = jnp.max(score, axis=0, keepdims=True)
    group = row >> 2
    best_group = jnp.min(jnp.where(score == best, group, N_EXPERTS), axis=0, keepdims=True)
    sel = top2 & (group == best_group)
    gate = jnp.where(sel, p / best, 0.0)

    sel_b = jnp.where(sel, 1.0, 0.0).astype(BF16)
    before = jnp.dot(sel_b, triu_ref[...], preferred_element_type=F32) + carry[...]
    tm = x.shape[0]
    new_carry = before[:, tm - 1:tm] + sel_b[:, tm - 1:tm].astype(F32)
    carry[...] = new_carry
    counts_ref[...] = new_carry

    e_lo = jnp.min(jnp.where(sel, row, N_EXPERTS), axis=0, keepdims=True)
    e_hi = jnp.max(jnp.where(sel, row, -1), axis=0, keepdims=True)
    is_lo = row == e_lo
    is_hi = row == e_hi
    pick = lambda m, a: jnp.sum(jnp.where(m, a, 0.0), axis=0, keepdims=True)
    zero = jnp.zeros_like(mx)
    route_ref[...] = jnp.concatenate(
        [e_lo.astype(F32), e_hi.astype(F32), pick(is_lo, before), pick(is_hi, before),
         pick(is_lo, gate), pick(is_hi, gate), zero, zero], axis=0)


def _merge(x2, att, conv, sc, wg, bg, wb, wo, g1, b1, wr, br):
    t, d = x2.shape
    tm = min(TM_PROJ, t)
    assert t % tm == 0
    triu = jnp.triu(jnp.ones((tm, tm), F32), 1).astype(BF16)
    const = lambda shape: pl.BlockSpec(shape, lambda i: (0,) * len(shape),
                                       pipeline_mode=pl.Buffered(1))
    rows = lambda w: pl.BlockSpec((tm, w), lambda i: (i, 0))
    return pl.pallas_call(
        _merge_kernel,
        grid=(t // tm,),
        in_specs=[rows(d), rows(WIDTH), rows(WIDTH), rows(WIDTH),
                  const(wg.shape), const(bg.shape), const(wb.shape), const(wo.shape),
                  const(g1.shape), const(b1.shape), const(wr.shape), const(br.shape),
                  const(triu.shape)],
        out_specs=[rows(d), pl.BlockSpec((ROW_CHUNKS, tm, CHUNK), lambda i: (0, i, 0)),
                   pl.BlockSpec((ROUTE_ROWS, tm), lambda i: (0, i)),
                   pl.BlockSpec((N_EXPERTS, 1), lambda i: (0, 0))],
        out_shape=[jax.ShapeDtypeStruct((t, d), F32),
                   jax.ShapeDtypeStruct((ROW_CHUNKS, t, CHUNK), jnp.uint32),
                   jax.ShapeDtypeStruct((ROUTE_ROWS, t), F32),
                   jax.ShapeDtypeStruct((N_EXPERTS, 1), F32)],
        scratch_shapes=[pltpu.VMEM((N_EXPERTS, 1), F32)],
        compiler_params=pltpu.CompilerParams(dimension_semantics=("arbitrary",),
                                             vmem_limit_bytes=VMEM_LIMIT),
        name="merge",
    )(x2, att, conv, sc, wg, bg, wb, wo, g1, b1, wr, br, triu)


def _sc_mesh():
    return plsc.VectorSubcoreMesh(core_axis_name="core", subcore_axis_name="subcore")


def _sc_scatter_rows(x, idx, n_rows):
    t, d = x.shape
    m = idx.shape[0]
    w = SC_WINDOW
    assert t % w == 0 and m == 2 * t
    steps = t // w

    @functools.partial(pl.kernel, out_type=jax.ShapeDtypeStruct((n_rows, d), x.dtype),
                       mesh=_sc_mesh(), scratch_types=[], name="sc_dispatch")
    def run(x_hbm, i_hbm, o_hbm):
        def body(x_vmem, first_vmem, second_vmem):
            pltpu.sync_copy(x_vmem, o_hbm.at[first_vmem.at[0]])
            pltpu.sync_copy(x_vmem, o_hbm.at[second_vmem.at[0]])

        pltpu.emit_pipeline(
            body,
            grid=(steps,),
            in_specs=[pl.BlockSpec((w, d), index_map=lambda i: (i, 0)),
                      pl.BlockSpec((1, w), index_map=lambda i: (0, i)),
                      pl.BlockSpec((1, w), index_map=lambda i: (0, steps + i))],
            out_specs=[],
            core_axis_name=("core", "subcore"),
            dimension_semantics=(pltpu.PARALLEL,),
            trace_scopes=False,
        )(x_hbm, i_hbm, i_hbm)

    return run(x, idx.reshape(1, m))


def _sc_gather_rows(table, idx):
    d = table.shape[1]
    m = idx.shape[0]
    w = SC_WINDOW
    assert m % w == 0

    @functools.partial(pl.kernel, out_type=jax.ShapeDtypeStruct((m, d), table.dtype),
                       mesh=_sc_mesh(), scratch_types=[], name="sc_gather")
    def run(t_hbm, i_hbm, o_hbm):
        def body(i_vmem, o_vmem):
            pltpu.sync_copy(t_hbm.at[i_vmem.at[0]], o_vmem)

        pltpu.emit_pipeline(
            body,
            grid=(m // w,),
            in_specs=[pl.BlockSpec((1, w), index_map=lambda i: (0, i))],
            out_specs=[pl.BlockSpec((w, d), index_map=lambda i: (i, 0))],
            core_axis_name=("core", "subcore"),
            dimension_semantics=(pltpu.PARALLEL,),
            trace_scopes=False,
        )(i_hbm, o_hbm)

    return run(table, idx.reshape(1, m))


def _fill_kernel(fs_ref, fl_ref, xs_in, xs_ref, zeros, sem):
    del xs_in
    zb = zeros.shape[0]
    zeros[...] = jnp.zeros_like(zeros)

    n_regions = fs_ref.shape[0]
    rows_per_chunk = xs_ref.shape[0] // ROW_CHUNKS

    def region(e, fn):
        start = fs_ref[e % n_regions] + (e // n_regions) * rows_per_chunk
        length = fl_ref[e % n_regions]
        head = jnp.minimum(length, (SUBLANES - start % SUBLANES) % SUBLANES)
        mid = start + head
        mid_len = (length - head) // SUBLANES * SUBLANES
        n_full = mid_len // zb
        rem = mid_len - n_full * zb
        part = mid + n_full * zb
        tail = mid + mid_len
        n_tail = length - head - mid_len

        def single(base):
            def body(r, _):
                fn(pltpu.make_async_copy(zeros.at[pl.ds(0, 1)],
                                         xs_ref.at[pl.ds(base + r, 1)], sem))
                return 0
            return body
        lax.fori_loop(0, head, single(start), 0)
        lax.fori_loop(0, n_tail, single(tail), 0)

        def full(c, _):
            off = pl.multiple_of(mid + c * zb, SUBLANES)
            fn(pltpu.make_async_copy(zeros, xs_ref.at[pl.ds(off, zb)], sem))
            return 0
        lax.fori_loop(0, n_full, full, 0)
        bit = zb // 2
        while bit >= SUBLANES:
            off = pl.multiple_of(part + (rem - rem % (2 * bit)), SUBLANES)

            @pl.when((rem & bit) != 0)
            def _(bit=bit, off=off):
                fn(pltpu.make_async_copy(zeros.at[pl.ds(0, bit)],
                                         xs_ref.at[pl.ds(off, bit)], sem))
            bit //= 2

    for fn in (lambda cp: cp.start(), lambda cp: cp.wait()):
        def body(e, _, fn=fn):
            region(e, fn)
            return 0
        lax.fori_loop(0, n_regions * ROW_CHUNKS, body, 0)


def _fill_unrouted(xs, fill_start, fill_len):
    n_rows, d = xs.shape
    return pl.pallas_call(
        _fill_kernel,
        grid_spec=pltpu.PrefetchScalarGridSpec(
            num_scalar_prefetch=2,
            grid=(1,),
            in_specs=[pl.BlockSpec(memory_space=pl.ANY)],
            out_specs=pl.BlockSpec(memory_space=pl.ANY),
            scratch_shapes=[pltpu.VMEM((ZERO_ROWS, d), xs.dtype), pltpu.SemaphoreType.DMA]),
        out_shape=jax.ShapeDtypeStruct((n_rows, d), xs.dtype),
        input_output_aliases={2: 0},
        compiler_params=pltpu.CompilerParams(dimension_semantics=("arbitrary",),
                                             vmem_limit_bytes=VMEM_LIMIT),
        name="fill_unrouted",
    )(fill_start, fill_len, xs)


def _expert_kernel(be_ref, nb_ref, xs_ref, wu_ref, wd_ref, ys_ref, wu_bf, wd_bf):
    b = pl.program_id(0)
    live = b < nb_ref[0]

    @pl.when(live & ((b == 0) | (be_ref[b] != be_ref[jnp.maximum(b, 1) - 1])))
    def _():
        wu_bf[...] = wu_ref[0, 0].astype(BF16)
        wd_bf[...] = wd_ref[0, 0].astype(BF16)

    @pl.when(live)
    def _():
        xb = _unpack_rows(xs_ref).astype(BF16)
        y = None
        for c0 in range(0, EXPERT_FF, EXPERT_FF // 2):
            c1 = c0 + EXPERT_FF // 2
            a = jnp.dot(xb, wu_bf[:, c0:c1], preferred_element_type=F32)
            g = jnp.dot(xb, wu_bf[:, EXPERT_FF + c0:EXPERT_FF + c1], preferred_element_type=F32)
            act = (g * _sigmoid(g) * a).astype(BF16)
            part = jnp.dot(act, wd_bf[c0:c1, :], preferred_element_type=F32)
            y = part if y is None else y + part
        _pack_rows(y, ys_ref)

    @pl.when(jnp.logical_not(live))
    def _():
        ys_ref[...] = jnp.zeros_like(ys_ref)


def _experts(xs, w_up, w_down, layer, block_expert, n_used):
    _, n_rows, d = xs.shape
    n_blocks = n_rows // EXPERT_BLOCK

    def live(b, nb):
        return jnp.minimum(b, nb[0] - 1)

    wspec = lambda w: pl.BlockSpec((1, 1) + w.shape[2:],
                                   lambda b, be, nb: (layer, be[live(b, nb)], 0, 0))
    return pl.pallas_call(
        _expert_kernel,
        grid_spec=pltpu.PrefetchScalarGridSpec(
            num_scalar_prefetch=2,
            grid=(n_blocks,),
            in_specs=[pl.BlockSpec((ROW_CHUNKS, EXPERT_BLOCK, d), lambda b, be, nb: (0, live(b, nb), 0)),
                      wspec(w_up), wspec(w_down)],
            out_specs=pl.BlockSpec((ROW_CHUNKS, EXPERT_BLOCK, d), lambda b, be, nb: (0, b, 0)),
            scratch_shapes=[pltpu.VMEM(w_up.shape[2:], BF16), pltpu.VMEM(w_down.shape[2:], BF16)]),
        out_shape=jax.ShapeDtypeStruct(xs.shape, xs.dtype),
        compiler_params=pltpu.CompilerParams(dimension_semantics=("arbitrary",),
                                             vmem_limit_bytes=VMEM_LIMIT),
        name="experts",
    )(block_expert, n_used, xs, w_up, w_down)


def _combine_kernel(x1_ref, route_ref, y0_ref, y1_ref, g2_ref, b2_ref, o_ref):
    route = route_ref[...].T
    m = route[:, 4:5] * _unpack_rows(y0_ref.at[0]) + route[:, 5:6] * _unpack_rows(y1_ref.at[0])
    o_ref[...] = _layer_norm(DEEPNORM_ALPHA * x1_ref[...] + m, g2_ref[...], b2_ref[...])


def _combine(x1, route, yk, g2, b2):
    t, d = x1.shape
    dc = yk.shape[-1]
    tm = min(TM_ROUTE, t)
    steps = t // tm
    const = lambda shape: pl.BlockSpec(shape, lambda i: (0,) * len(shape))
    return pl.pallas_call(
        _combine_kernel,
        grid=(steps,),
        in_specs=[pl.BlockSpec((tm, d), lambda i: (i, 0)),
                  pl.BlockSpec((ROUTE_ROWS, tm), lambda i: (0, i)),
                  pl.BlockSpec((1, ROW_CHUNKS, tm, dc), lambda i: (0, 0, i, 0)),
                  pl.BlockSpec((1, ROW_CHUNKS, tm, dc), lambda i: (1, 0, i, 0)),
                  const(g2.shape), const(b2.shape)],
        out_specs=pl.BlockSpec((tm, d), lambda i: (i, 0)),
        out_shape=jax.ShapeDtypeStruct((t, d), F32),
        compiler_params=pltpu.CompilerParams(dimension_semantics=("arbitrary",),
                                             vmem_limit_bytes=VMEM_LIMIT),
        name="combine",
    )(x1, route, yk, yk, g2, b2)


def _routing_tables(route, counts, n_blocks):
    t = route.shape[1]
    cnt = counts[:, 0].astype(jnp.int32)
    padded = (cnt + EXPERT_BLOCK - 1) // EXPERT_BLOCK * EXPERT_BLOCK
    pad_end = jnp.cumsum(padded)
    pad_start = pad_end - padded
    experts = route[0:2].astype(jnp.int32)
    ranks = route[2:4].astype(jnp.int32)
    ids = jnp.arange(N_EXPERTS, dtype=jnp.int32)
    base = jnp.sum(jnp.where(experts[:, :, None] == ids, pad_start, 0), axis=-1)
    n_rows = n_blocks * EXPERT_BLOCK
    chunk_base = jnp.arange(ROW_CHUNKS, dtype=jnp.int32) * n_rows
    pos = ((base + ranks)[:, None, :] + chunk_base[None, :, None]).reshape(2 * ROW_CHUNKS * t)
    blk = jnp.arange(n_blocks, dtype=jnp.int32) * EXPERT_BLOCK
    block_expert = jnp.minimum(jnp.sum((pad_end[None, :] <= blk[:, None]).astype(jnp.int32), axis=1),
                               N_EXPERTS - 1)
    n_used = (pad_end[-1:] // EXPERT_BLOCK).astype(jnp.int32)
    fill_start = jnp.concatenate([pad_start + cnt, pad_end[-1:]]).astype(jnp.int32)
    fill_len = jnp.concatenate([padded - cnt, n_rows - pad_end[-1:]]).astype(jnp.int32)
    return pos, block_expert, n_used, fill_start, fill_len


def kernel(x, w_in, b_forget, conv_w, conv_b, conv_ln_g, conv_ln_b, sc_w, w_branch, w_gate, b_gate,
           w_out, ln1_g, ln1_b, w_up, w_down, ln2_g, ln2_b, w_router, b_router):
    batch, seq, d = x.shape
    t = batch * seq
    depth = w_in.shape[0]
    x2 = x.reshape(t, d)

    nf = ATT_HEADS
    w_main = jnp.concatenate([w_in[:, :, :2 * WIDTH], w_in[:, :, 3 * WIDTH + nf:]], axis=2).astype(BF16)
    w_vt = jnp.swapaxes(w_in[:, :, 2 * WIDTH:3 * WIDTH], 1, 2).astype(BF16)
    w_f = jnp.pad(w_in[:, :, 3 * WIDTH:3 * WIDTH + nf], ((0, 0), (0, 0), (0, LANES - nf))).astype(BF16)
    b_f = jnp.pad(b_forget, ((0, 0), (0, LANES - nf)))[:, None, :]
    wg = w_gate.astype(BF16)
    wb = w_branch.astype(BF16)
    wo = w_out.astype(BF16)
    wr_t = w_router.T
    wr_hi = wr_t.astype(BF16)
    wr_lo = (wr_t - wr_hi.astype(F32)).astype(BF16)
    wr = jnp.concatenate([wr_hi, wr_hi, wr_lo], axis=1)
    br = b_router[:, None]

    def layer(xh, l):
        th = xh.shape[0]
        n_blocks = (2 * th + N_EXPERTS * (EXPERT_BLOCK - 1) + EXPERT_BLOCK - 1) // EXPERT_BLOCK
        n_rows = n_blocks * EXPERT_BLOCK
        qf, kf, vt, br_conv, br_sc = _in_proj(
            xh, w_main[l], w_vt[l], w_f[l], b_f[l], sc_w[l], conv_w[l], conv_b[l][None, :],
            conv_ln_g[l][None, :], conv_ln_b[l][None, :], seq)
        br_att = _fox_attention(qf, kf, vt, th // seq, seq)
        x1, x1p, route, counts = _merge(xh, br_att, br_conv, br_sc, wg[l], b_gate[l][:, None, :],
                                        wb[l], wo[l], ln1_g[l][None, :], ln1_b[l][None, :], wr, br)
        pos, block_expert, n_used, fill_start, fill_len = _routing_tables(route, counts, n_blocks)
        xs = _sc_scatter_rows(x1p.reshape(ROW_CHUNKS * th, CHUNK), pos, ROW_CHUNKS * n_rows)
        xs = _fill_unrouted(xs, fill_start, fill_len).reshape(ROW_CHUNKS, n_rows, CHUNK)
        ys = _experts(xs, w_up, w_down, l, block_expert, n_used)
        yk = _sc_gather_rows(ys.reshape(ROW_CHUNKS * n_rows, CHUNK), pos)
        return _combine(x1, route, yk.reshape(2, ROW_CHUNKS, th, CHUNK),
                        ln2_g[l][None, :], ln2_b[l][None, :])

    for l in range(depth):
        x2 = layer(x2, l)
    return x2.reshape(batch, seq, d)
```
